```python
import math
import jax, jax.numpy as jnp
from jax import lax
import numpy as np

D_MODEL = 1024
BATCH = 16
SEQ = 4096
DEPTH = 1
DEC_BATCH = 32
DEC_SEQ = 64
PAST_LEN = 4096

CHUNK = 64
Q_BLOCK = 128
HEAD_DIM = 64
N_HEADS_A = 8
N_KV_A = 2
N_IDX_HEADS = 8
IDX_DIM = 64
TOPK_MAX = 256
N_HEADS_B = 8
MIX_WIDTH = (N_HEADS_A + N_HEADS_B) * HEAD_DIM
ROT_DIM = HEAD_DIM // 4
ROPE_THETA = 500000.0
D_FF = 2816
LN_EPS = 1e-5
ALPHA = (2.0 * DEPTH) ** 0.25
BETA = (8.0 * DEPTH) ** -0.25
COL_SIZES = (N_HEADS_A * HEAD_DIM, N_KV_A * HEAD_DIM, N_KV_A * HEAD_DIM,
             N_IDX_HEADS * IDX_DIM, IDX_DIM, N_IDX_HEADS,
             N_HEADS_B * HEAD_DIM, N_HEADS_B * HEAD_DIM, N_HEADS_B * HEAD_DIM, N_HEADS_B)
N_IN = sum(COL_SIZES)

kernel_name = "hybrid_dsa_fox_streaming_step"


def layer_norm(x, g, b):
    xf = x.astype(jnp.float32)
    mu = jnp.mean(xf, axis=-1, keepdims=True)
    var = jnp.mean(jnp.square(xf - mu), axis=-1, keepdims=True)
    return ((xf - mu) * lax.rsqrt(var + LN_EPS) * g.astype(jnp.float32) + b.astype(jnp.float32)).astype(x.dtype)


def swiglu(x, w_gate, w_up, w_down):
    return (jax.nn.silu(x @ w_gate) * (x @ w_up)) @ w_down


def macaron_half(x, g, b, w_gate, w_up, w_down):
    return layer_norm(ALPHA * x + 0.5 * swiglu(x, w_gate, w_up, w_down), g, b)


def partial_rope(x, pos):
    half = ROT_DIM // 2
    inv_freq = ROPE_THETA ** (-jnp.arange(half, dtype=jnp.float32) * 2.0 / ROT_DIM)
    ang = pos.astype(jnp.float32)[:, None] * inv_freq[None, :]
    cos = jnp.cos(ang)[None, :, None, :].astype(x.dtype)
    sin = jnp.sin(ang)[None, :, None, :].astype(x.dtype)
    x1 = x[..., :half]
    x2 = x[..., half:ROT_DIM]
    return jnp.concatenate([x1 * cos - x2 * sin, x2 * cos + x1 * sin, x[..., ROT_DIM:]], axis=-1)


def project(h, pos, w_in, b_f):
    B, T = h.shape[0], h.shape[1]
    z = h @ w_in
    parts = []
    off = 0
    for s in COL_SIZES:
        parts.append(z[..., off:off + s])
        off += s
    q_a, k_a, v_a, q_i, k_i, w_i, q_b, k_b, v_b, f_b = parts
    q_a = partial_rope(q_a.reshape(B, T, N_HEADS_A, HEAD_DIM), pos)
    k_a = partial_rope(k_a.reshape(B, T, N_KV_A, HEAD_DIM), pos)
    v_a = v_a.reshape(B, T, N_KV_A, HEAD_DIM)
    q_i = partial_rope(q_i.reshape(B, T, N_IDX_HEADS, IDX_DIM), pos)
    k_i = partial_rope(k_i.reshape(B, T, 1, IDX_DIM), pos)[:, :, 0]
    w_i = w_i * (N_IDX_HEADS ** -0.5)
    q_b = q_b.reshape(B, T, N_HEADS_B, HEAD_DIM)
    k_b = k_b.reshape(B, T, N_HEADS_B, HEAD_DIM)
    v_b = v_b.reshape(B, T, N_HEADS_B, HEAD_DIM)
    logf_b = jax.nn.log_sigmoid((f_b + b_f).astype(jnp.float32))
    return q_a, k_a, v_a, q_i, k_i, w_i, q_b, k_b, v_b, logf_b


def dsa_block(q, qi, wi, qpos, k, v, ki, k_sel):
    B, Tq = q.shape[0], q.shape[1]
    S = k.shape[1]
    rel = jax.nn.relu(jnp.einsum('bthi,bsi->bths', qi, ki) * (IDX_DIM ** -0.5))
    score = jnp.einsum('bth,bths->bts', wi, rel).astype(jnp.float32)
    qchunk = qpos // CHUNK
    admissible = (jnp.arange(S, dtype=jnp.int32) // CHUNK)[None, :] <= qchunk[:, None]
    score = jnp.where(admissible[None], score, -jnp.inf)
    _, idx = lax.top_k(score, k_sel)
    valid = (idx // CHUNK) <= qchunk[None, :, None]
    kg = jax.vmap(lambda kk, ii: kk[ii])(k, idx)
    vg = jax.vmap(lambda vv, ii: vv[ii])(v, idx)
    qg = q.reshape(B, Tq, N_KV_A, N_HEADS_A // N_KV_A, HEAD_DIM)
    logits = jnp.einsum('bthgd,btnhd->bthgn', qg, kg).astype(jnp.float32) * (HEAD_DIM ** -0.5)
    logits = jnp.where(valid[:, :, None, None, :], logits, -jnp.inf)
    p = jax.nn.softmax(logits, axis=-1).astype(v.dtype)
    o = jnp.einsum('bthgn,btnhd->bthgd', p, vg)
    return o.reshape(B, Tq, N_HEADS_A * HEAD_DIM)


def fox_block(q, cq, qpos, k, v, ck_t):
    B, Tq = q.shape[0], q.shape[1]
    S = k.shape[1]
    logits = jnp.einsum('bthd,bshd->bhts', q, k).astype(jnp.float32) * (HEAD_DIM ** -0.5)
    bias = jnp.transpose(cq, (0, 2, 1))[..., :, None] - ck_t[..., None, :]
    causal = jnp.arange(S, dtype=jnp.int32)[None, :] <= qpos[:, None]
    logits = jnp.where(causal, logits + bias, -jnp.inf)
    p = jax.nn.softmax(logits, axis=-1).astype(v.dtype)
    o = jnp.einsum('bhts,bshd->bthd', p, v)
    return o.reshape(B, Tq, N_HEADS_B * HEAD_DIM)


def mixer_prompt(q_a, k_a, v_a, q_i, k_i, w_i, q_b, k_b, v_b, logf_b):
    B, T = q_a.shape[0], q_a.shape[1]
    nb = T // Q_BLOCK
    k_sel = min(TOPK_MAX, T // 4)
    pos = jnp.arange(T, dtype=jnp.int32)
    cum = jnp.cumsum(logf_b, axis=1)
    ck_t = jnp.transpose(cum, (0, 2, 1))

    def blocks(a):
        return jnp.moveaxis(a.reshape((B, nb, Q_BLOCK) + a.shape[2:]), 1, 0)

    def body(xs):
        qa, qi, wi, qb, cq, qpos = xs
        oa = dsa_block(qa, qi, wi, qpos, k_a, v_a, k_i, k_sel)
        ob = fox_block(qb, cq, qpos, k_b, v_b, ck_t)
        return jnp.concatenate([oa, ob], axis=-1)

    out = lax.map(body, (blocks(q_a), blocks(q_i), blocks(w_i), blocks(q_b), blocks(cum),
                         pos.reshape(nb, Q_BLOCK)))
    return jnp.moveaxis(out, 0, 1).reshape(B, T, MIX_WIDTH)


def mixer_sample(q_a, k_a, v_a, q_i, k_i, w_i, q_b, k_b, v_b, logf_b, pos,
                 c_k_a, c_v_a, c_kidx, c_k_b, c_v_b, c_logf):
    past = c_k_a.shape[1]
    ka_all = jnp.concatenate([c_k_a, k_a], axis=1)
    va_all = jnp.concatenate([c_v_a, v_a], axis=1)
    ki_all = jnp.concatenate([c_kidx, k_i], axis=1)
    kb_all = jnp.concatenate([c_k_b, k_b], axis=1)
    vb_all = jnp.concatenate([c_v_b, v_b], axis=1)
    L = ka_all.shape[1]
    k_sel = min(TOPK_MAX, L // 4)
    cum = jnp.cumsum(jnp.concatenate([c_logf.astype(jnp.float32), logf_b], axis=1), axis=1)
    oa = dsa_block(q_a, q_i, w_i, pos, ka_all, va_all, ki_all, k_sel)
    ob = fox_block(q_b, cum[:, past:], pos, kb_all, vb_all, jnp.transpose(cum, (0, 2, 1)))
    return jnp.concatenate([oa, ob], axis=-1)


def setup_inputs(seed: int = 0) -> dict:
    key = jax.random.key(seed)
    ks = jax.random.split(key, 24)
    f32 = jnp.float32
    nrm = lambda k, shape, scale: jax.random.normal(k, shape, f32) * scale
    cache_rows = (DEPTH, DEC_BATCH, PAST_LEN)
    return {
        "x_prompt": nrm(ks[0], (BATCH, SEQ, D_MODEL), 1.0),
        "x_sample": nrm(ks[1], (DEC_BATCH, DEC_SEQ, D_MODEL), 1.0),
        "cache_k_a": nrm(ks[2], cache_rows + (N_KV_A, HEAD_DIM), 1.0),
        "cache_v_a": nrm(ks[3], cache_rows + (N_KV_A, HEAD_DIM), 1.0),
        "cache_kidx_a": nrm(ks[4], cache_rows + (IDX_DIM,), 1.0),
        "cache_k_b": nrm(ks[5], cache_rows + (N_HEADS_B, HEAD_DIM), 1.0),
        "cache_v_b": nrm(ks[6], cache_rows + (N_HEADS_B, HEAD_DIM), 1.0),
        "cache_logf_b": jax.nn.log_sigmoid(jax.random.uniform(ks[7], cache_rows + (N_HEADS_B,), f32, 1.0, 5.0)
                                           + nrm(ks[8], cache_rows + (N_HEADS_B,), 1.0)),
        "w_in": nrm(ks[9], (DEPTH, D_MODEL, N_IN), D_MODEL ** -0.5),
        "b_f": jax.random.uniform(ks[10], (DEPTH, N_HEADS_B), f32, 1.0, 5.0),
        "w_out": nrm(ks[11], (DEPTH, MIX_WIDTH, D_MODEL), BETA * MIX_WIDTH ** -0.5),
        "ln1_g": 1.0 + nrm(ks[12], (DEPTH, D_MODEL), 0.05),
        "ln1_b": nrm(ks[13], (DEPTH, D_MODEL), 0.02),
        "ffn1_w_gate": nrm(ks[14], (DEPTH, D_MODEL, D_FF), D_MODEL ** -0.5),
        "ffn1_w_up": nrm(ks[15], (DEPTH, D_MODEL, D_FF), D_MODEL ** -0.5),
        "ffn1_w_down": nrm(ks[16], (DEPTH, D_FF, D_MODEL), BETA * D_FF ** -0.5),
        "ln2_g": 1.0 + nrm(ks[17], (DEPTH, D_MODEL), 0.05),
        "ln2_b": nrm(ks[18], (DEPTH, D_MODEL), 0.02),
        "ln3_g": 1.0 + nrm(ks[19], (DEPTH, D_MODEL), 0.05),
        "ln3_b": nrm(ks[20], (DEPTH, D_MODEL), 0.02),
        "ffn2_w_gate": nrm(ks[21], (DEPTH, D_MODEL, D_FF), D_MODEL ** -0.5),
        "ffn2_w_up": nrm(ks[22], (DEPTH, D_MODEL, D_FF), D_MODEL ** -0.5),
        "ffn2_w_down": nrm(ks[23], (DEPTH, D_FF, D_MODEL), BETA * D_FF ** -0.5),
    }


def reference(x_prompt, x_sample, cache_k_a, cache_v_a, cache_kidx_a, cache_k_b, cache_v_b, cache_logf_b,
              w_in, b_f, w_out, ln1_g, ln1_b, ffn1_w_gate, ffn1_w_up, ffn1_w_down,
              ln2_g, ln2_b, ln3_g, ln3_b, ffn2_w_gate, ffn2_w_up, ffn2_w_down):
    seq = x_prompt.shape[1]
    dec_seq = x_sample.shape[1]
    past = cache_k_a.shape[2]
    pos_p = jnp.arange(seq, dtype=jnp.int32)
    pos_s = past + jnp.arange(dec_seq, dtype=jnp.int32)
    xp = x_prompt
    xs = x_sample
    rows_p = []
    rows_s = []
    for l in range(DEPTH):
        hp = macaron_half(xp, ln1_g[l], ln1_b[l], ffn1_w_gate[l], ffn1_w_up[l], ffn1_w_down[l])
        qa, ka, va, qi, ki, wi, qb, kb, vb, lfb = project(hp, pos_p, w_in[l], b_f[l])
        mp = mixer_prompt(qa, ka, va, qi, ki, wi, qb, kb, vb, lfb)
        hp = layer_norm(ALPHA * hp + mp @ w_out[l], ln2_g[l], ln2_b[l])
        xp = macaron_half(hp, ln3_g[l], ln3_b[l], ffn2_w_gate[l], ffn2_w_up[l], ffn2_w_down[l])
        rows_p.append((ka, va, ki, kb, vb, lfb))
        hs = macaron_half(xs, ln1_g[l], ln1_b[l], ffn1_w_gate[l], ffn1_w_up[l], ffn1_w_down[l])
        qa, ka, va, qi, ki, wi, qb, kb, vb, lfb = project(hs, pos_s, w_in[l], b_f[l])
        ms = mixer_sample(qa, ka, va, qi, ki, wi, qb, kb, vb, lfb, pos_s,
                          cache_k_a[l], cache_v_a[l], cache_kidx_a[l], cache_k_b[l], cache_v_b[l], cache_logf_b[l])
        hs = layer_norm(ALPHA * hs + ms @ w_out[l], ln2_g[l], ln2_b[l])
        xs = macaron_half(hs, ln3_g[l], ln3_b[l], ffn2_w_gate[l], ffn2_w_up[l], ffn2_w_down[l])
        rows_s.append((ka, va, ki, kb, vb, lfb))
    new_k_a_p, new_v_a_p, new_kidx_p, new_k_b_p, new_v_b_p, new_logf_p = [jnp.stack(a, axis=0) for a in zip(*rows_p)]
    new_k_a_s, new_v_a_s, new_kidx_s, new_k_b_s, new_v_b_s, new_logf_s = [jnp.stack(a, axis=0) for a in zip(*rows_s)]
    y_prompt = xp
    y_sample = xs
    return (y_prompt, y_sample,
            new_k_a_p, new_v_a_p, new_kidx_p, new_k_b_p, new_v_b_p, new_logf_p,
            new_k_a_s, new_v_a_s, new_kidx_s, new_k_b_s, new_v_b_s, new_logf_s)
```

```python
import functools
import math

import jax
import jax.numpy as jnp
from jax import lax
from jax.experimental import pallas as pl
from jax.experimental.pallas import tpu as pltpu

CHUNK = 64
_CHUNK_SHIFT = 6
HEAD_DIM = 64
N_HEADS_A = 8
N_KV_A = 2
N_IDX_HEADS = 8
IDX_DIM = 64
TOPK_MAX = 256
N_HEADS_B = 8
ROT_DIM = HEAD_DIM // 4
ROPE_THETA = 500000.0
LN_EPS = 1e-5
DEPTH = 1
ALPHA = (2.0 * DEPTH) ** 0.25

MXU_DTYPE = jnp.bfloat16

LANES = 128
V7X_VMEM_LIMIT_BYTES = 60 * 1024 * 1024

ROW_TILE = 512
FF_CHUNK = 256
Q_TILE = 256
KEY_BLOCK = 512
SAMPLE_KEY_BLOCK = 256

INT_MIN = -(2 ** 31)
NEG_INF = float("-inf")

_C_QA, _C_KA, _C_VA, _C_QI, _C_KI, _C_WF, _C_QB, _C_KB, _C_VB, _C_END = (
    0, 512, 640, 768, 1280, 1408, 1536, 2048, 2560, 3072)
_QA_HEAD_ORDER = (0, 4, 1, 5, 2, 6, 3, 7)


def _nt_dot(a, b):
    return lax.dot_general(a, b, (((1,), (1,)), ((), ())), preferred_element_type=jnp.float32)


def _dot(a, b):
    return jnp.dot(a, b, preferred_element_type=jnp.float32)


def _layer_norm(x, g, b):
    mu = jnp.mean(x, axis=-1, keepdims=True)
    xc = x - mu
    var = jnp.mean(xc * xc, axis=-1, keepdims=True)
    return xc * lax.rsqrt(var + LN_EPS) * g + b


def _macaron_half(x, wg_ref, wu_ref, wd_ref, g, b, act_ref):
    xb = x.astype(MXU_DTYPE)
    d_ff = wg_ref.shape[1]
    for c in range(d_ff // FF_CHUNK):
        cols = slice(c * FF_CHUNK, (c + 1) * FF_CHUNK)
        gate = _dot(xb, wg_ref[:, cols])
        up = _dot(xb, wu_ref[:, cols])
        act_ref[:, cols] = (gate * jax.nn.sigmoid(gate) * up).astype(MXU_DTYPE)
    down = _dot(act_ref[...], wd_ref[...])
    return _layer_norm(ALPHA * x + 0.5 * down, g, b)


def _ffn_ln_kernel(x_ref, wg_ref, wu_ref, wd_ref, g_ref, b_ref, o_ref, act_ref):
    o_ref[...] = _macaron_half(x_ref[...], wg_ref, wu_ref, wd_ref, g_ref[...], b_ref[...], act_ref)


def _const_spec(shape):
    return pl.BlockSpec(shape, lambda *_: (0,) * len(shape), pipeline_mode=pl.Buffered(1))


def _row_tile(n):
    tm = ROW_TILE
    while n % tm:
        tm //= 2
    return tm


def _ffn_ln(x, wg, wu, wd, g, b):
    n, d = x.shape
    d_ff = wg.shape[1]
    tm = _row_tile(n)
    return pl.pallas_call(
        _ffn_ln_kernel,
        grid=(n // tm,),
        in_specs=[
            pl.BlockSpec((tm, d), lambda i: (i, 0)),
            _const_spec((d, d_ff)), _const_spec((d, d_ff)), _const_spec((d_ff, d)),
            _const_spec((1, d)), _const_spec((1, d)),
        ],
        out_specs=pl.BlockSpec((tm, d), lambda i: (i, 0)),
        out_shape=jax.ShapeDtypeStruct((n, d), jnp.float32),
        scratch_shapes=[pltpu.VMEM((tm, d_ff), MXU_DTYPE)],
        compiler_params=pltpu.CompilerParams(
            dimension_semantics=("arbitrary",), vmem_limit_bytes=V7X_VMEM_LIMIT_BYTES),
        name="ffn_ln",
    )(x, wg, wu, wd, g, b)


def _rope(x, cos, sin):
    lane = lax.broadcasted_iota(jnp.int32, x.shape, 1) & (HEAD_DIM - 1)
    partner = jnp.where(lane < ROT_DIM // 2,
                        pltpu.roll(x, LANES - ROT_DIM // 2, 1),
                        pltpu.roll(x, ROT_DIM // 2, 1))
    return x * cos + partner * sin


def _project_kernel(h_ref, w_ref, bf_ref, cos_ref, sin_ref,
                    qa_ref, ka_ref, va_ref, qi_ref, kidx_ref, wi_ref, qb_ref, kb_ref, vb_ref, logf_ref,
                    kab_ref, vab_ref, kib_ref, kbb_ref, vbb_ref):
    hb = h_ref[...].astype(MXU_DTYPE)
    cos = cos_ref[...]
    sin = sin_ref[...]
    q_scale = HEAD_DIM ** -0.5

    def part(lo, hi):
        return _dot(hb, w_ref[:, lo:hi])

    def rope_groups(z):
        return jnp.concatenate(
            [_rope(z[:, g * LANES:(g + 1) * LANES], cos, sin) for g in range(z.shape[1] // LANES)], axis=1)

    qa_ref[...] = (rope_groups(part(_C_QA, _C_KA)) * q_scale).astype(MXU_DTYPE)
    ka = _rope(part(_C_KA, _C_VA), cos, sin)
    ka_ref[...] = ka
    kab_ref[...] = ka.astype(MXU_DTYPE)
    va = part(_C_VA, _C_QI)
    va_ref[...] = va
    vab_ref[...] = va.astype(MXU_DTYPE)
    qi_ref[...] = (rope_groups(part(_C_QI, _C_KI)) * (IDX_DIM ** -0.5)).astype(MXU_DTYPE)
    ki = _rope(part(_C_KI, _C_WF), cos, sin)
    kidx_ref[...] = ki[:, :IDX_DIM]
    kib_ref[...] = ki.astype(MXU_DTYPE)
    wf = part(_C_WF, _C_QB)
    wi_ref[...] = wf[:, :N_IDX_HEADS] * (N_IDX_HEADS ** -0.5)
    f = wf + bf_ref[...]
    logf = -(jnp.maximum(-f, 0.0) + jnp.log1p(jnp.exp(-jnp.abs(f))))
    logf_ref[...] = pltpu.roll(logf, LANES - N_IDX_HEADS, 1)[:, :N_HEADS_B]
    qb_ref[...] = (part(_C_QB, _C_KB) * q_scale).astype(MXU_DTYPE)
    kb = part(_C_KB, _C_VB)
    kb_ref[...] = kb
    kbb_ref[...] = kb.astype(MXU_DTYPE)
    vb = part(_C_VB, _C_END)
    vb_ref[...] = vb
    vbb_ref[...] = vb.astype(MXU_DTYPE)


def _project(h, w, bf, cos, sin, tm, tab_blocks):
    n, d = h.shape
    f32, bf16 = jnp.float32, MXU_DTYPE
    widths = [(512, bf16), (128, f32), (128, f32), (512, bf16), (IDX_DIM, f32), (N_IDX_HEADS, f32),
              (512, bf16), (512, f32), (512, f32), (N_HEADS_B, f32),
              (128, bf16), (128, bf16), (128, bf16), (512, bf16), (512, bf16)]
    row = lambda wdt: pl.BlockSpec((tm, wdt), lambda i: (i, 0))
    tab = pl.BlockSpec((tm, LANES), lambda i: (i % tab_blocks, 0))
    return pl.pallas_call(
        _project_kernel,
        grid=(n // tm,),
        in_specs=[row(d), _const_spec(w.shape), _const_spec((1, LANES)), tab, tab],
        out_specs=[row(wdt) for wdt, _ in widths],
        out_shape=[jax.ShapeDtypeStruct((n, wdt), dt) for wdt, dt in widths],
        compiler_params=pltpu.CompilerParams(
            dimension_semantics=("arbitrary",), vmem_limit_bytes=V7X_VMEM_LIMIT_BYTES),
        name="project",
    )(h, w, bf, cos, sin)


def _cumsum_kernel(x_ref, o_ref, *, seg):
    r = lax.broadcasted_iota(jnp.int32, (seg, seg), 0)
    c = lax.broadcasted_iota(jnp.int32, (seg, seg), 1)
    tri = (r <= c).astype(jnp.float32)
    carry = jnp.zeros((x_ref.shape[1], 1), jnp.float32)
    for s in range(x_ref.shape[2] // seg):
        cols = slice(s * seg, (s + 1) * seg)
        part = jnp.dot(x_ref[0, :, cols], tri, preferred_element_type=jnp.float32,
                       precision=lax.Precision.HIGHEST) + carry
        o_ref[0, :, cols] = part
        carry = part[:, seg - 1:seg]


def _cumsum_keys(x):
    b, h, s = x.shape
    seg = 2 * LANES if s % (2 * LANES) == 0 else LANES
    assert s % seg == 0
    return pl.pallas_call(
        functools.partial(_cumsum_kernel, seg=seg),
        grid=(b,),
        in_specs=[pl.BlockSpec((1, h, s), lambda i: (i, 0, 0))],
        out_specs=pl.BlockSpec((1, h, s), lambda i: (i, 0, 0)),
        out_shape=jax.ShapeDtypeStruct((b, h, s), jnp.float32),
        compiler_params=pltpu.CompilerParams(dimension_semantics=("arbitrary",)),
        name="cumsum_keys",
    )(x)


def _num_key_blocks(i, tq, kb, q_off):
    return (q_off + (i + 1) * tq + kb - 1) // kb


def _softmax_step(s, m, l, acc, vblk):
    m_new = jnp.maximum(m, jnp.max(s, axis=-1, keepdims=True))
    m_safe = jnp.where(m_new == NEG_INF, 0.0, m_new)
    corr = jnp.exp(m - m_safe)
    p = jnp.exp(s - m_safe)
    l = corr * l + jnp.sum(p, axis=-1, keepdims=True)
    acc = corr * acc + _dot(p.astype(MXU_DTYPE), vblk)
    return m_new, l, acc


def _half_masks(q2):
    lane = lax.broadcasted_iota(jnp.int32, q2.shape, 1)
    zero = jnp.zeros_like(q2)
    return jnp.where(lane < HEAD_DIM, q2, zero), jnp.where(lane >= HEAD_DIM, q2, zero)


def _init_state(tq):
    return (jnp.full((tq, 1), NEG_INF, jnp.float32), jnp.zeros((tq, 1), jnp.float32),
            jnp.zeros((tq, LANES), jnp.float32))


def _merge_halves(st_lo, st_hi):
    lane = lax.broadcasted_iota(jnp.int32, st_lo[2].shape, 1)
    return jnp.where(lane < HEAD_DIM, st_lo[2] / st_lo[1], st_hi[2] / st_hi[1])


def _fox_kernel(q_ref, k_ref, v_ref, ck_ref, o_ref, *, tq, kb, q_off):
    i = pl.program_id(1)
    nb = _num_key_blocks(i, tq, kb, q_off)
    row = q_off + i * tq + lax.broadcasted_iota(jnp.int32, (tq, kb), 0)
    col0 = lax.broadcasted_iota(jnp.int32, (tq, kb), 1)
    for p in range(N_HEADS_B // 2):
        lanes = slice(p * LANES, (p + 1) * LANES)
        q_lo, q_hi = _half_masks(q_ref[:, lanes])

        def body(j, carry, lanes=lanes, q_lo=q_lo, q_hi=q_hi, p=p):
            st_lo, st_hi = carry
            start = pl.multiple_of(j * kb, kb)
            kblk = k_ref[pl.ds(start, kb), lanes]
            vblk = v_ref[pl.ds(start, kb), lanes]
            ck = ck_ref[0, j]
            causal = (col0 + j * kb) <= row
            out = []
            for half, (qm, st) in enumerate(((q_lo, st_lo), (q_hi, st_hi))):
                h = 2 * p + half
                s = _nt_dot(qm, kblk) - ck[h:h + 1, :]
                s = jnp.where(causal, s, NEG_INF)
                out.append(_softmax_step(s, *st, vblk))
            return tuple(out)

        st_lo, st_hi = lax.fori_loop(0, nb, body, (_init_state(tq), _init_state(tq)))
        o_ref[:, lanes] = _merge_halves(st_lo, st_hi).astype(o_ref.dtype)


def _fox(q, k, v, ck, *, batch, tq, kb, q_off):
    n, width = q.shape
    t_q = n // batch
    s_len = k.shape[0] // batch
    nq = t_q // tq
    return pl.pallas_call(
        functools.partial(_fox_kernel, tq=tq, kb=kb, q_off=q_off),
        grid=(batch, nq),
        in_specs=[
            pl.BlockSpec((tq, width), lambda b, i: (b * nq + i, 0)),
            pl.BlockSpec((s_len, width), lambda b, i: (b, 0)),
            pl.BlockSpec((s_len, width), lambda b, i: (b, 0)),
            pl.BlockSpec((1,) + ck.shape[1:], lambda b, i: (b, 0, 0, 0)),
        ],
        out_specs=pl.BlockSpec((tq, width), lambda b, i: (b * nq + i, 0)),
        out_shape=jax.ShapeDtypeStruct((n, width), MXU_DTYPE),
        compiler_params=pltpu.CompilerParams(
            dimension_semantics=("arbitrary", "arbitrary"), vmem_limit_bytes=V7X_VMEM_LIMIT_BYTES),
        name="fox_attention",
    )(q, k, v, ck)


def _dsa_kernel(qi_ref, wi_ref, qa_ref, ki_ref, ka_ref, va_ref, o_ref, key_ref, bias_ref,
                *, tq, kb, q_off, k_sel):
    i = pl.program_id(1)
    nb = _num_key_blocks(i, tq, kb, q_off)
    row = q_off + i * tq + lax.broadcasted_iota(jnp.int32, (tq, kb), 0)
    col0 = lax.broadcasted_iota(jnp.int32, (tq, kb), 1)
    k_f = jnp.float32(k_sel)

    wi = wi_ref[...]
    q_idx = []
    for p in range(N_IDX_HEADS // 2):
        q_idx.extend(_half_masks(qi_ref[:, p * LANES:(p + 1) * LANES]))

    def score_block(j, _):
        start = pl.multiple_of(j * kb, kb)
        kblk = ki_ref[pl.ds(start, kb), :]
        score = jnp.zeros((tq, kb), jnp.float32)
        for h in range(N_IDX_HEADS):
            score = score + wi[:, h:h + 1] * jnp.maximum(_nt_dot(q_idx[h], kblk), 0.0)
        bits = lax.bitcast_convert_type(score, jnp.int32)
        key = jnp.where(bits < 0, bits ^ jnp.int32(0x7FFFFFFF), bits)
        admissible = ((col0 + j * kb) >> _CHUNK_SHIFT) <= (row >> _CHUNK_SHIFT)
        key_ref[j] = jnp.where(admissible, key, jnp.int32(INT_MIN))
        return 0

    lax.fori_loop(0, nb, score_block, 0)

    def count(pred):
        def body(j, acc):
            hit = jnp.where(pred(key_ref[j], col0 + j * kb), 1.0, 0.0)
            part = hit[:, :LANES]
            for g in range(1, kb // LANES):
                part = part + hit[:, g * LANES:(g + 1) * LANES]
            return acc + part
        acc = lax.fori_loop(0, nb, body, jnp.zeros((tq, LANES), jnp.float32))
        return jnp.sum(acc, axis=-1, keepdims=True)

    zero = jnp.zeros((tq, 1), jnp.int32)
    c0 = count(lambda key, idx: key >= zero)
    thr = jnp.where(c0 >= k_f, zero, jnp.int32(INT_MIN))
    cnt = jnp.where(c0 >= k_f, c0, k_f)

    def bisect(b, carry):
        thr, cnt = carry
        cand = thr + (jnp.int32(1) << (jnp.int32(30) - b))
        c = count(lambda key, idx: key >= cand)
        take = c >= k_f
        return jnp.where(take, cand, thr), jnp.where(take, c, cnt)

    thr, cnt = lax.fori_loop(0, 31, bisect, (thr, cnt))

    has_thr = thr > jnp.int32(INT_MIN)
    surplus = jnp.logical_and(has_thr, cnt > k_f)
    big = jnp.int32(2 ** 30)

    def tie_limit():
        need = k_f - count(lambda key, idx: key > thr)

        def step(b, n):
            cand = n + (jnp.int32(1) << (jnp.int32(14) - b))
            c = count(lambda key, idx: jnp.logical_and(key == thr, idx < cand))
            return jnp.where(c < need, cand, n)

        n = lax.fori_loop(0, 15, step, jnp.zeros((tq, 1), jnp.int32))
        return jnp.where(surplus, n + 1, big)

    any_surplus = jnp.max(jnp.where(surplus, 1.0, 0.0)) > 0.0
    limit = lax.cond(any_surplus, tie_limit, lambda: jnp.full((tq, 1), big, jnp.int32))
    limit = jnp.where(has_thr, limit, 0)

    def bias_block(j, _):
        key = key_ref[j]
        sel = jnp.logical_or(key > thr, jnp.logical_and(key == thr, (col0 + j * kb) < limit))
        bias_ref[j] = jnp.where(sel, 0.0, NEG_INF)
        return 0

    lax.fori_loop(0, nb, bias_block, 0)

    for p in range(N_HEADS_A // 2):
        lanes = slice(p * LANES, (p + 1) * LANES)
        q_lo, q_hi = _half_masks(qa_ref[:, lanes])

        def body(j, carry, q_lo=q_lo, q_hi=q_hi):
            start = pl.multiple_of(j * kb, kb)
            kblk = ka_ref[pl.ds(start, kb), :]
            vblk = va_ref[pl.ds(start, kb), :]
            bias = bias_ref[j]
            return tuple(_softmax_step(_nt_dot(qm, kblk) + bias, *st, vblk)
                         for qm, st in zip((q_lo, q_hi), carry))

        st_lo, st_hi = lax.fori_loop(0, nb, body, (_init_state(tq), _init_state(tq)))
        o_ref[:, lanes] = _merge_halves(st_lo, st_hi).astype(o_ref.dtype)


def _dsa(qi, wi, qa, ki, ka, va, *, batch, tq, kb, q_off, k_sel):
    n, width = qa.shape
    t_q = n // batch
    s_len = ka.shape[0] // batch
    nq = t_q // tq
    nblk = s_len // kb
    qspec = lambda wdt: pl.BlockSpec((tq, wdt), lambda b, i: (b * nq + i, 0))
    kspec = pl.BlockSpec((s_len, LANES), lambda b, i: (b, 0))
    return pl.pallas_call(
        functools.partial(_dsa_kernel, tq=tq, kb=kb, q_off=q_off, k_sel=k_sel),
        grid=(batch, nq),
        in_specs=[qspec(width), qspec(N_IDX_HEADS), qspec(width), kspec, kspec, kspec],
        out_specs=qspec(width),
        out_shape=jax.ShapeDtypeStruct((n, width), MXU_DTYPE),
        scratch_shapes=[pltpu.VMEM((nblk, tq, kb), jnp.int32), pltpu.VMEM((nblk, tq, kb), jnp.float32)],
        compiler_params=pltpu.CompilerParams(
            dimension_semantics=("arbitrary", "arbitrary"), vmem_limit_bytes=V7X_VMEM_LIMIT_BYTES),
        name="dsa_attention",
    )(qi, wi, qa, ki, ka, va)


def _post_kernel(h_ref, oa_ref, ob_ref, woa_ref, wob_ref, g2_ref, b2_ref,
                 wg_ref, wu_ref, wd_ref, g3_ref, b3_ref, o_ref, act_ref):
    mix = _dot(oa_ref[...], woa_ref[...]) + _dot(ob_ref[...], wob_ref[...])
    h2 = _layer_norm(ALPHA * h_ref[...] + mix, g2_ref[...], b2_ref[...])
    o_ref[...] = _macaron_half(h2, wg_ref, wu_ref, wd_ref, g3_ref[...], b3_ref[...], act_ref)


def _post(h, oa, ob, woa, wob, g2, b2, wg, wu, wd, g3, b3):
    n, d = h.shape
    d_ff = wg.shape[1]
    tm = _row_tile(n)
    row = lambda wdt: pl.BlockSpec((tm, wdt), lambda i: (i, 0))
    return pl.pallas_call(
        _post_kernel,
        grid=(n // tm,),
        in_specs=[row(d), row(oa.shape[1]), row(ob.shape[1]),
                  _const_spec(woa.shape), _const_spec(wob.shape), _const_spec((1, d)), _const_spec((1, d)),
                  _const_spec((d, d_ff)), _const_spec((d, d_ff)), _const_spec((d_ff, d)),
                  _const_spec((1, d)), _const_spec((1, d))],
        out_specs=row(d),
        out_shape=jax.ShapeDtypeStruct((n, d), jnp.float32),
        scratch_shapes=[pltpu.VMEM((tm, d_ff), MXU_DTYPE)],
        compiler_params=pltpu.CompilerParams(
            dimension_semantics=("arbitrary",), vmem_limit_bytes=V7X_VMEM_LIMIT_BYTES),
        name="post",
    )(h, oa, ob, woa, wob, g2, b2, wg, wu, wd, g3, b3)


def _prepare_w_in(w_in):
    d = w_in.shape[0]
    sizes = (N_HEADS_A * HEAD_DIM, N_KV_A * HEAD_DIM, N_KV_A * HEAD_DIM, N_IDX_HEADS * IDX_DIM, IDX_DIM,
             N_IDX_HEADS, N_HEADS_B * HEAD_DIM, N_HEADS_B * HEAD_DIM, N_HEADS_B * HEAD_DIM, N_HEADS_B)
    offs = [0]
    for s in sizes:
        offs.append(offs[-1] + s)
    qa, ka, va, qi, ki, wi, qb, kb, vb, fb = (w_in[:, offs[k]:offs[k + 1]] for k in range(10))
    qa = qa.reshape(d, N_HEADS_A, HEAD_DIM)[:, jnp.array(_QA_HEAD_ORDER)].reshape(d, -1)
    pad = jnp.zeros((d, LANES - N_IDX_HEADS - N_HEADS_B), w_in.dtype)
    w = jnp.concatenate([qa, ka, va, qi, ki, ki, wi, fb, pad, qb, kb, vb], axis=1)
    assert w.shape[1] == _C_END
    return w.astype(MXU_DTYPE)


def _rope_tables(pos):
    half = ROT_DIM // 2
    inv_freq = ROPE_THETA ** (-jnp.arange(half, dtype=jnp.float32) * 2.0 / ROT_DIM)
    ang = pos.astype(jnp.float32)[:, None] * inv_freq[None, :]
    cos, sin = jnp.cos(ang), jnp.sin(ang)
    ones = jnp.ones((pos.shape[0], HEAD_DIM - ROT_DIM), jnp.float32)
    cos64 = jnp.concatenate([cos, cos, ones], axis=1)
    sin64 = jnp.concatenate([-sin, sin, jnp.zeros_like(ones)], axis=1)
    return jnp.tile(cos64, (1, LANES // HEAD_DIM)), jnp.tile(sin64, (1, LANES // HEAD_DIM))


def _project_tokens(h, w_in_p, bf_p, pos, rows_per_seq):
    n = h.shape[0]
    tm = _row_tile(n)
    cos, sin = _rope_tables(pos)
    if tm <= rows_per_seq:
        assert rows_per_seq % tm == 0
        tab_blocks = rows_per_seq // tm
    else:
        assert tm % rows_per_seq == 0
        cos = jnp.tile(cos, (tm // rows_per_seq, 1))
        sin = jnp.tile(sin, (tm // rows_per_seq, 1))
        tab_blocks = 1
    return _project(h, w_in_p, bf_p, cos, sin, tm, tab_blocks)


def _blocked_cum(logf_keys, kb):
    b, s, h = logf_keys.shape
    cum = _cumsum_keys(jnp.transpose(logf_keys, (0, 2, 1)))
    return jnp.transpose(cum.reshape(b, h, s // kb, kb), (0, 2, 1, 3))


def _pad_keys(x, s_pad):
    return jnp.pad(x, ((0, 0), (0, s_pad - x.shape[1]), (0, 0)))


def kernel(x_prompt, x_sample, cache_k_a, cache_v_a, cache_kidx_a, cache_k_b, cache_v_b, cache_logf_b,
           w_in, b_f, w_out, ln1_g, ln1_b, ffn1_w_gate, ffn1_w_up, ffn1_w_down,
           ln2_g, ln2_b, ln3_g, ln3_b, ffn2_w_gate, ffn2_w_up, ffn2_w_down):
    assert w_in.shape[0] == DEPTH
    bsz, seq, d = x_prompt.shape
    dbs, dseq, _ = x_sample.shape
    past = cache_k_a.shape[2]
    bf16 = MXU_DTYPE

    w_in_p = _prepare_w_in(w_in[0])
    bf_p = jnp.zeros((1, LANES), jnp.float32).at[0, N_IDX_HEADS:N_IDX_HEADS + N_HEADS_B].set(b_f[0])
    order = jnp.array(_QA_HEAD_ORDER)
    w_out_a = w_out[0, :N_HEADS_A * HEAD_DIM].reshape(N_HEADS_A, HEAD_DIM, d)[order].reshape(-1, d).astype(bf16)
    w_out_b = w_out[0, N_HEADS_A * HEAD_DIM:].astype(bf16)
    ffn1 = (ffn1_w_gate[0].astype(bf16), ffn1_w_up[0].astype(bf16), ffn1_w_down[0].astype(bf16))
    ffn2 = (ffn2_w_gate[0].astype(bf16), ffn2_w_up[0].astype(bf16), ffn2_w_down[0].astype(bf16))
    vec = lambda a: a[0].reshape(1, d)

    def layer(x, pos, rows_per_seq, attend):
        n = x.shape[0] * x.shape[1]
        h = _ffn_ln(x.reshape(n, d), *ffn1, vec(ln1_g), vec(ln1_b))
        proj = _project_tokens(h, w_in_p, bf_p, pos, rows_per_seq)
        oa, ob = attend(proj)
        y = _post(h, oa, ob, w_out_a, w_out_b, vec(ln2_g), vec(ln2_b), *ffn2, vec(ln3_g), vec(ln3_b))
        return y.reshape(x.shape), proj

    tq_p = min(Q_TILE, seq)
    kb_p = min(KEY_BLOCK, seq)

    def attend_prompt(proj):
        qa, _, _, qi, _, wi, qb, _, _, logf, kab, vab, kib, kbb, vbb = proj
        ck = _blocked_cum(logf.reshape(bsz, seq, N_HEADS_B), kb_p)
        oa = _dsa(qi, wi, qa, kib, kab, vab, batch=bsz, tq=tq_p, kb=kb_p, q_off=0,
                  k_sel=min(TOPK_MAX, seq // 4))
        ob = _fox(qb, kbb, vbb, ck, batch=bsz, tq=tq_p, kb=kb_p, q_off=0)
        return oa, ob

    y_p, proj_p = layer(x_prompt, jnp.arange(seq, dtype=jnp.int32), seq, attend_prompt)

    total = past + dseq
    kb_s = min(SAMPLE_KEY_BLOCK, past)
    s_pad = -(-total // kb_s) * kb_s

    def attend_sample(proj):
        qa, _, _, qi, _, wi, qb, _, _, logf, kab, vab, kib, kbb, vbb = proj
        new = lambda a: a.reshape(dbs, dseq, -1)
        keys = lambda c, nw: _pad_keys(
            jnp.concatenate([c.reshape(dbs, past, -1).astype(bf16), new(nw)], axis=1), s_pad
        ).reshape(dbs * s_pad, -1)
        kidx2 = jnp.concatenate([cache_kidx_a[0], cache_kidx_a[0]], axis=-1)
        logf_all = _pad_keys(jnp.concatenate([cache_logf_b[0].astype(jnp.float32), new(logf)], axis=1), s_pad)
        ck = _blocked_cum(logf_all, kb_s)
        oa = _dsa(qi, wi, qa, keys(kidx2, kib), keys(cache_k_a[0], kab), keys(cache_v_a[0], vab),
                  batch=dbs, tq=dseq, kb=kb_s, q_off=past, k_sel=min(TOPK_MAX, total // 4))
        ob = _fox(qb, keys(cache_k_b[0], kbb), keys(cache_v_b[0], vbb), ck,
                  batch=dbs, tq=dseq, kb=kb_s, q_off=past)
        return oa, ob

    y_s, proj_s = layer(x_sample, past + jnp.arange(dseq, dtype=jnp.int32), dseq, attend_sample)

    def rows(proj, b, t):
        _, ka, va, _, kidx, _, _, kbv, vbv, logf = proj[:10]
        return (ka.reshape(1, b, t, N_KV_A, HEAD_DIM), va.reshape(1, b, t, N_KV_A, HEAD_DIM),
                kidx.reshape(1, b, t, IDX_DIM), kbv.reshape(1, b, t, N_HEADS_B, HEAD_DIM),
                vbv.reshape(1, b, t, N_HEADS_B, HEAD_DIM), logf.reshape(1, b, t, N_HEADS_B))

    return (y_p, y_s) + rows(proj_p, bsz, seq) + rows(proj_s, dbs, dseq)
```

```python
import functools
import math

import jax
import jax.numpy as jnp
from jax import lax
from jax.experimental import pallas as pl
from jax.experimental.pallas import tpu as pltpu

CHUNK = 64
_CHUNK_SHIFT = 6
HEAD_DIM = 64
N_HEADS_A = 8
N_KV_A = 2
N_IDX_HEADS = 8
IDX_DIM = 64
TOPK_MAX = 256
N_HEADS_B = 8
ROT_DIM = HEAD_DIM // 4
ROPE_THETA = 500000.0
LN_EPS = 1e-5
DEPTH = 1
ALPHA = (2.0 * DEPTH) ** 0.25

MXU_DTYPE = jnp.bfloat16

LANES = 128
V7X_VMEM_LIMIT_BYTES = 60 * 1024 * 1024

ROW_TILE = 512
FF_CHUNK = 256
Q_TILE = 256
KEY_BLOCK = 512
SAMPLE_KEY_BLOCK = 256
ROW_CHUNK = 64

INT_MIN = -(2 ** 31)
NEG_INF = float("-inf")
F32_LOWEST = float(jnp.finfo(jnp.float32).min)
LOG2E = math.log2(math.e)

_C_QA, _C_KA, _C_VA, _C_QI, _C_KI, _C_WF, _C_QB, _C_KB, _C_VB, _C_END = (
    0, 512, 640, 768, 1280, 1408, 1536, 2048, 2560, 3072)
_QA_HEAD_ORDER = (0, 4, 1, 5, 2, 6, 3, 7)


def _nt_dot(a, b):
    return lax.dot_general(a, b, (((1,), (1,)), ((), ())), preferred_element_type=jnp.float32)


def _dot(a, b):
    return jnp.dot(a, b, preferred_element_type=jnp.float32)


def _layer_norm(x, g, b):
    mu = jnp.mean(x, axis=-1, keepdims=True)
    xc = x - mu
    var = jnp.mean(xc * xc, axis=-1, keepdims=True)
    return xc * lax.rsqrt(var + LN_EPS) * g + b


def _macaron_half(x, wg_ref, wu_ref, wd_ref, g, b, act_ref):
    xb = x.astype(MXU_DTYPE)
    d_ff = wg_ref.shape[1]
    for c in range(d_ff // FF_CHUNK):
        cols = slice(c * FF_CHUNK, (c + 1) * FF_CHUNK)
        gate = _dot(xb, wg_ref[:, cols])
        up = _dot(xb, wu_ref[:, cols])
        act_ref[:, cols] = (gate * jax.nn.sigmoid(gate) * up).astype(MXU_DTYPE)
    down = _dot(act_ref[...], wd_ref[...])
    return _layer_norm(ALPHA * x + 0.5 * down, g, b)


def _ffn_ln_kernel(x_ref, wg_ref, wu_ref, wd_ref, g_ref, b_ref, o_ref, act_ref):
    o_ref[...] = _macaron_half(x_ref[...], wg_ref, wu_ref, wd_ref, g_ref[...], b_ref[...], act_ref)


def _const_spec(shape):
    return pl.BlockSpec(shape, lambda *_: (0,) * len(shape), pipeline_mode=pl.Buffered(1))


def _row_tile(n):
    tm = ROW_TILE
    while n % tm:
        tm //= 2
    return tm


def _ffn_ln(x, wg, wu, wd, g, b):
    n, d = x.shape
    d_ff = wg.shape[1]
    tm = _row_tile(n)
    return pl.pallas_call(
        _ffn_ln_kernel,
        grid=(n // tm,),
        in_specs=[
            pl.BlockSpec((tm, d), lambda i: (i, 0)),
            _const_spec((d, d_ff)), _const_spec((d, d_ff)), _const_spec((d_ff, d)),
            _const_spec((1, d)), _const_spec((1, d)),
        ],
        out_specs=pl.BlockSpec((tm, d), lambda i: (i, 0)),
        out_shape=jax.ShapeDtypeStruct((n, d), jnp.float32),
        scratch_shapes=[pltpu.VMEM((tm, d_ff), MXU_DTYPE)],
        compiler_params=pltpu.CompilerParams(
            dimension_semantics=("arbitrary",), vmem_limit_bytes=V7X_VMEM_LIMIT_BYTES),
        name="ffn_ln",
    )(x, wg, wu, wd, g, b)


def _rope(x, cos, sin):
    lane = lax.broadcasted_iota(jnp.int32, x.shape, 1) & (HEAD_DIM - 1)
    partner = jnp.where(lane < ROT_DIM // 2,
                        pltpu.roll(x, LANES - ROT_DIM // 2, 1),
                        pltpu.roll(x, ROT_DIM // 2, 1))
    return x * cos + partner * sin


def _project_kernel(h_ref, w_ref, bf_ref, cos_ref, sin_ref,
                    qa_ref, ka_ref, va_ref, qi_ref, kidx_ref, wi_ref, qb_ref, kb_ref, vb_ref, logf_ref,
                    kab_ref, vab_ref, kib_ref, kbb_ref, vbb_ref):
    hb = h_ref[...].astype(MXU_DTYPE)
    cos = cos_ref[...]
    sin = sin_ref[...]
    q_scale = HEAD_DIM ** -0.5 * LOG2E

    def part(lo, hi):
        return _dot(hb, w_ref[:, lo:hi])

    def rope_groups(z):
        return jnp.concatenate(
            [_rope(z[:, g * LANES:(g + 1) * LANES], cos, sin) for g in range(z.shape[1] // LANES)], axis=1)

    qa_ref[...] = (rope_groups(part(_C_QA, _C_KA)) * q_scale).astype(MXU_DTYPE)
    ka = _rope(part(_C_KA, _C_VA), cos, sin)
    ka_ref[...] = ka
    kab_ref[...] = ka.astype(MXU_DTYPE)
    va = part(_C_VA, _C_QI)
    va_ref[...] = va
    vab_ref[...] = va.astype(MXU_DTYPE)
    qi_ref[...] = (rope_groups(part(_C_QI, _C_KI)) * (IDX_DIM ** -0.5)).astype(MXU_DTYPE)
    ki = _rope(part(_C_KI, _C_WF), cos, sin)
    kidx_ref[...] = ki[:, :IDX_DIM]
    kib_ref[...] = ki.astype(MXU_DTYPE)
    wf = part(_C_WF, _C_QB)
    wi_ref[...] = wf[:, :N_IDX_HEADS] * (N_IDX_HEADS ** -0.5)
    f = wf + bf_ref[...]
    logf = -(jnp.maximum(-f, 0.0) + jnp.log1p(jnp.exp(-jnp.abs(f))))
    logf_ref[...] = pltpu.roll(logf, LANES - N_IDX_HEADS, 1)[:, :N_HEADS_B]
    qb_ref[...] = (part(_C_QB, _C_KB) * q_scale).astype(MXU_DTYPE)
    kb = part(_C_KB, _C_VB)
    kb_ref[...] = kb
    kbb_ref[...] = kb.astype(MXU_DTYPE)
    vb = part(_C_VB, _C_END)
    vb_ref[...] = vb
    vbb_ref[...] = vb.astype(MXU_DTYPE)


def _project(h, w, bf, cos, sin, tm, tab_blocks):
    n, d = h.shape
    f32, bf16 = jnp.float32, MXU_DTYPE
    widths = [(512, bf16), (128, f32), (128, f32), (512, bf16), (IDX_DIM, f32), (N_IDX_HEADS, f32),
              (512, bf16), (512, f32), (512, f32), (N_HEADS_B, f32),
              (128, bf16), (128, bf16), (128, bf16), (512, bf16), (512, bf16)]
    row = lambda wdt: pl.BlockSpec((tm, wdt), lambda i: (i, 0))
    tab = pl.BlockSpec((tm, LANES), lambda i: (i % tab_blocks, 0))
    return pl.pallas_call(
        _project_kernel,
        grid=(n // tm,),
        in_specs=[row(d), _const_spec(w.shape), _const_spec((1, LANES)), tab, tab],
        out_specs=[row(wdt) for wdt, _ in widths],
        out_shape=[jax.ShapeDtypeStruct((n, wdt), dt) for wdt, dt in widths],
        compiler_params=pltpu.CompilerParams(
            dimension_semantics=("arbitrary",), vmem_limit_bytes=V7X_VMEM_LIMIT_BYTES),
        name="project",
    )(h, w, bf, cos, sin)


def _cumsum_kernel(x_ref, o_ref, *, seg):
    r = lax.broadcasted_iota(jnp.int32, (seg, seg), 0)
    c = lax.broadcasted_iota(jnp.int32, (seg, seg), 1)
    tri = (r <= c).astype(jnp.float32)
    carry = jnp.zeros((x_ref.shape[1], 1), jnp.float32)
    for s in range(x_ref.shape[2] // seg):
        cols = slice(s * seg, (s + 1) * seg)
        part = jnp.dot(x_ref[0, :, cols], tri, preferred_element_type=jnp.float32,
                       precision=lax.Precision.HIGHEST) + carry
        o_ref[0, :, cols] = part * LOG2E
        carry = part[:, seg - 1:seg]


def _cumsum_keys(x):
    b, h, s = x.shape
    seg = 2 * LANES if s % (2 * LANES) == 0 else LANES
    assert s % seg == 0
    return pl.pallas_call(
        functools.partial(_cumsum_kernel, seg=seg),
        grid=(b,),
        in_specs=[pl.BlockSpec((1, h, s), lambda i: (i, 0, 0))],
        out_specs=pl.BlockSpec((1, h, s), lambda i: (i, 0, 0)),
        out_shape=jax.ShapeDtypeStruct((b, h, s), jnp.float32),
        compiler_params=pltpu.CompilerParams(dimension_semantics=("arbitrary",)),
        name="cumsum_keys",
    )(x)


def _num_key_blocks(i, tq, kb, q_off):
    return (q_off + (i + 1) * tq + kb - 1) // kb


def _lane_groups(x):
    return [x[:, g * LANES:(g + 1) * LANES] for g in range(x.shape[1] // LANES)]


def _softmax_step(s, h, m_ref, l_ref, acc_ref, v_ones):
    groups = _lane_groups(s)
    smax = functools.reduce(jnp.maximum, groups)
    m_old = m_ref[h]
    m_new = jnp.maximum(m_old, jnp.max(smax, axis=-1, keepdims=True))
    m_safe = jnp.maximum(m_new, F32_LOWEST)
    corr = jnp.exp2(m_old - m_safe)
    p = jnp.concatenate([jnp.exp2(g - m_safe) for g in groups], axis=1).astype(MXU_DTYPE)
    pv = _dot(p, v_ones)
    acc_ref[h] = corr * acc_ref[h] + pv[:, :LANES]
    l_ref[h] = corr * l_ref[h] + pv[:, LANES:]
    m_ref[h] = m_new


def _store_values_and_ones(v_ref, lanes, vo_ref):
    vo_ref[:, :LANES] = v_ref[:, lanes]
    vo_ref[:, LANES:] = jnp.ones((vo_ref.shape[0], LANES), vo_ref.dtype)


def _store_half_masks(q_ref, qm_ref):
    for p in range(q_ref.shape[1] // LANES):
        q2 = q_ref[:, p * LANES:(p + 1) * LANES]
        lane = lax.broadcasted_iota(jnp.int32, q2.shape, 1)
        zero = jnp.zeros_like(q2)
        qm_ref[2 * p] = jnp.where(lane < HEAD_DIM, q2, zero)
        qm_ref[2 * p + 1] = jnp.where(lane >= HEAD_DIM, q2, zero)


def _init_state(m_ref, l_ref, acc_ref):
    m_ref[...] = jnp.full(m_ref.shape, NEG_INF, jnp.float32)
    l_ref[...] = jnp.zeros(l_ref.shape, jnp.float32)
    acc_ref[...] = jnp.zeros(acc_ref.shape, jnp.float32)


def _write_pairs(o_ref, l_ref, acc_ref):
    for p in range(o_ref.shape[1] // LANES):
        outs = [acc_ref[h] / l_ref[h] for h in (2 * p, 2 * p + 1)]
        lane = lax.broadcasted_iota(jnp.int32, outs[0].shape, 1)
        o_ref[:, p * LANES:(p + 1) * LANES] = jnp.where(lane < HEAD_DIM, outs[0], outs[1]).astype(o_ref.dtype)


def _attention_scratch(tq, n_heads):
    return [pltpu.VMEM((n_heads, tq, LANES), MXU_DTYPE),
            pltpu.VMEM((n_heads, tq, LANES), jnp.float32),
            pltpu.VMEM((n_heads, tq, LANES), jnp.float32),
            pltpu.VMEM((n_heads, tq, LANES), jnp.float32)]


def _fox_kernel(q_ref, k_ref, v_ref, ck_ref, o_ref, vo_ref, qm_ref, m_ref, l_ref, acc_ref, *, tq, kb, q_off):
    i = pl.program_id(1)
    nb = _num_key_blocks(i, tq, kb, q_off)
    assert kb % tq == 0 and q_off % tq == 0

    @pl.when(i == 0)
    def _():
        for p in range(N_HEADS_B // 2):
            _store_values_and_ones(v_ref, slice(p * LANES, (p + 1) * LANES), vo_ref.at[p])

    _store_half_masks(q_ref, qm_ref)
    _init_state(m_ref, l_ref, acc_ref)

    def block(j, masked):
        start = pl.multiple_of(j * kb, kb)
        ck = ck_ref[0, j]
        if masked:
            row = q_off + i * tq + lax.broadcasted_iota(jnp.int32, (tq, kb), 0)
            col = j * kb + lax.broadcasted_iota(jnp.int32, (tq, kb), 1)
            causal = col <= row
        for h in range(N_HEADS_B):
            lanes = slice((h // 2) * LANES, (h // 2 + 1) * LANES)
            s = _nt_dot(qm_ref[h], k_ref[pl.ds(start, kb), lanes]) - ck[h:h + 1, :]
            if masked:
                s = jnp.where(causal, s, NEG_INF)
            _softmax_step(s, h, m_ref, l_ref, acc_ref, vo_ref[h // 2, pl.ds(start, kb), :])

    def full_block(j, _):
        block(j, masked=False)
        return 0

    lax.fori_loop(0, nb - 1, full_block, 0)
    block(nb - 1, masked=True)
    _write_pairs(o_ref, l_ref, acc_ref)


def _fox(q, k, v, ck, *, batch, tq, kb, q_off):
    n, width = q.shape
    t_q = n // batch
    s_len = k.shape[0] // batch
    nq = t_q // tq
    return pl.pallas_call(
        functools.partial(_fox_kernel, tq=tq, kb=kb, q_off=q_off),
        grid=(batch, nq),
        in_specs=[
            pl.BlockSpec((tq, width), lambda b, i: (b * nq + i, 0)),
            pl.BlockSpec((s_len, width), lambda b, i: (b, 0)),
            pl.BlockSpec((s_len, width), lambda b, i: (b, 0)),
            pl.BlockSpec((1,) + ck.shape[1:], lambda b, i: (b, 0, 0, 0)),
        ],
        out_specs=pl.BlockSpec((tq, width), lambda b, i: (b * nq + i, 0)),
        out_shape=jax.ShapeDtypeStruct((n, width), MXU_DTYPE),
        scratch_shapes=[pltpu.VMEM((N_HEADS_B // 2, s_len, 2 * LANES), MXU_DTYPE)]
        + _attention_scratch(tq, N_HEADS_B),
        compiler_params=pltpu.CompilerParams(
            dimension_semantics=("arbitrary", "arbitrary"), vmem_limit_bytes=V7X_VMEM_LIMIT_BYTES),
        name="fox_attention",
    )(q, k, v, ck)


def _dsa_kernel(qi_ref, wi_ref, qa_ref, ki_ref, ka_ref, va_ref, o_ref,
                key_ref, bias_ref, thr_ref, cand_ref, lim_ref, cnt_ref, vo_ref, qm_ref, m_ref, l_ref, acc_ref,
                *, tq, kb, q_off, k_sel):
    i = pl.program_id(1)
    nb = _num_key_blocks(i, tq, kb, q_off)
    k_f = jnp.float32(k_sel)
    rc = min(ROW_CHUNK, tq)
    chunks = [slice(r * rc, (r + 1) * rc) for r in range(tq // rc)]
    groups = [slice(g * LANES, (g + 1) * LANES) for g in range(kb // LANES)]

    def replicate(x):
        return jnp.broadcast_to(x, (tq, LANES))

    _store_half_masks(qi_ref, qm_ref)
    wi = wi_ref[...]

    def score_block(j, _):
        start = pl.multiple_of(j * kb, kb)
        kblk = ki_ref[pl.ds(start, kb), :]
        score = jnp.zeros((tq, kb), jnp.float32)
        for h in range(N_IDX_HEADS):
            score = score + wi[:, h:h + 1] * jnp.maximum(_nt_dot(qm_ref[h], kblk), 0.0)
        bits = lax.bitcast_convert_type(score, jnp.int32)
        key = jnp.where(bits < 0, bits ^ jnp.int32(0x7FFFFFFF), bits)
        row = q_off + i * tq + lax.broadcasted_iota(jnp.int32, (tq, kb), 0)
        col = j * kb + lax.broadcasted_iota(jnp.int32, (tq, kb), 1)
        admissible = (col >> _CHUNK_SHIFT) <= (row >> _CHUNK_SHIFT)
        key_ref[j] = jnp.where(admissible, key, jnp.int32(INT_MIN))
        return 0

    lax.fori_loop(0, nb, score_block, 0)

    def count(pred):
        def body(j, acc):
            parts = []
            for rows in chunks:
                hits = [jnp.where(pred(key_ref[j, rows, g], rows, j * kb + g.start), 1.0, 0.0) for g in groups]
                parts.append(functools.reduce(jnp.add, hits))
            return acc + jnp.concatenate(parts, axis=0)
        acc = lax.fori_loop(0, nb, body, jnp.zeros((tq, LANES), jnp.float32))
        return replicate(jnp.sum(acc, axis=-1, keepdims=True))

    def at_least_cand(key, rows, first):
        return key >= cand_ref[rows, :]

    cand_ref[...] = jnp.zeros((tq, LANES), jnp.int32)
    c0 = count(at_least_cand)
    thr_ref[...] = jnp.where(c0 >= k_f, jnp.int32(0), jnp.int32(INT_MIN))
    cnt_ref[...] = jnp.where(c0 >= k_f, c0, k_f)

    def bisect(b, _):
        cand = thr_ref[...] + (jnp.int32(1) << (jnp.int32(30) - b))
        cand_ref[...] = cand
        c = count(at_least_cand)
        take = c >= k_f
        thr_ref[...] = jnp.where(take, cand, thr_ref[...])
        cnt_ref[...] = jnp.where(take, c, cnt_ref[...])
        return 0

    lax.fori_loop(0, 31, bisect, 0)

    thr = thr_ref[...]
    has_thr = thr > jnp.int32(INT_MIN)
    surplus = jnp.logical_and(has_thr, cnt_ref[...] > k_f)
    big = jnp.int32(2 ** 30)

    def tie_limit():
        need = k_f - count(lambda key, rows, first: key > thr_ref[rows, :])
        lim_ref[...] = jnp.zeros((tq, LANES), jnp.int32)

        def tied_before_cand(key, rows, first):
            col = first + lax.broadcasted_iota(jnp.int32, key.shape, 1)
            return jnp.logical_and(key == thr_ref[rows, :], col < cand_ref[rows, :])

        def step(b, _):
            cand = lim_ref[...] + (jnp.int32(1) << (jnp.int32(14) - b))
            cand_ref[...] = cand
            c = count(tied_before_cand)
            lim_ref[...] = jnp.where(c < need, cand, lim_ref[...])
            return 0

        lax.fori_loop(0, 15, step, 0)
        return jnp.where(surplus, lim_ref[...] + 1, big)

    any_surplus = jnp.max(jnp.where(surplus, 1.0, 0.0)) > 0.0
    limit = lax.cond(any_surplus, tie_limit, lambda: jnp.full((tq, LANES), big, jnp.int32))
    lim_ref[...] = jnp.where(has_thr, limit, 0)

    def bias_block(j, _):
        for rows in chunks:
            for g in groups:
                key = key_ref[j, rows, g]
                col = j * kb + g.start + lax.broadcasted_iota(jnp.int32, key.shape, 1)
                tied = jnp.logical_and(key == thr_ref[rows, :], col < lim_ref[rows, :])
                sel = jnp.logical_or(key > thr_ref[rows, :], tied)
                bias_ref[j, rows, g] = jnp.where(sel, 0.0, NEG_INF)
        return 0

    lax.fori_loop(0, nb, bias_block, 0)

    @pl.when(i == 0)
    def _():
        _store_values_and_ones(va_ref, slice(0, LANES), vo_ref)

    _store_half_masks(qa_ref, qm_ref)
    _init_state(m_ref, l_ref, acc_ref)

    def attend(j, _):
        start = pl.multiple_of(j * kb, kb)
        for h in range(N_HEADS_A):
            s = _nt_dot(qm_ref[h], ka_ref[pl.ds(start, kb), :]) + bias_ref[j]
            _softmax_step(s, h, m_ref, l_ref, acc_ref, vo_ref[pl.ds(start, kb), :])
        return 0

    lax.fori_loop(0, nb, attend, 0)
    _write_pairs(o_ref, l_ref, acc_ref)


def _dsa(qi, wi, qa, ki, ka, va, *, batch, tq, kb, q_off, k_sel):
    n, width = qa.shape
    t_q = n // batch
    s_len = ka.shape[0] // batch
    nq = t_q // tq
    nblk = s_len // kb
    qspec = lambda wdt: pl.BlockSpec((tq, wdt), lambda b, i: (b * nq + i, 0))
    kspec = pl.BlockSpec((s_len, LANES), lambda b, i: (b, 0))
    return pl.pallas_call(
        functools.partial(_dsa_kernel, tq=tq, kb=kb, q_off=q_off, k_sel=k_sel),
        grid=(batch, nq),
        in_specs=[qspec(width), qspec(N_IDX_HEADS), qspec(width), kspec, kspec, kspec],
        out_specs=qspec(width),
        out_shape=jax.ShapeDtypeStruct((n, width), MXU_DTYPE),
        scratch_shapes=[pltpu.VMEM((nblk, tq, kb), jnp.int32),
                        pltpu.VMEM((nblk, tq, kb), jnp.float32),
                        pltpu.VMEM((tq, LANES), jnp.int32),
                        pltpu.VMEM((tq, LANES), jnp.int32),
                        pltpu.VMEM((tq, LANES), jnp.int32),
                        pltpu.VMEM((tq, LANES), jnp.float32),
                        pltpu.VMEM((s_len, 2 * LANES), MXU_DTYPE),
                        ] + _attention_scratch(tq, N_HEADS_A),
        compiler_params=pltpu.CompilerParams(
            dimension_semantics=("arbitrary", "arbitrary"), vmem_limit_bytes=V7X_VMEM_LIMIT_BYTES),
        name="dsa_attention",
    )(qi, wi, qa, ki, ka, va)


def _post_kernel(h_ref, oa_ref, ob_ref, woa_ref, wob_ref, g2_ref, b2_ref,
                 wg_ref, wu_ref, wd_ref, g3_ref, b3_ref, o_ref, act_ref):
    mix = _dot(oa_ref[...], woa_ref[...]) + _dot(ob_ref[...], wob_ref[...])
    h2 = _layer_norm(ALPHA * h_ref[...] + mix, g2_ref[...], b2_ref[...])
    o_ref[...] = _macaron_half(h2, wg_ref, wu_ref, wd_ref, g3_ref[...], b3_ref[...], act_ref)


def _post(h, oa, ob, woa, wob, g2, b2, wg, wu, wd, g3, b3):
    n, d = h.shape
    d_ff = wg.shape[1]
    tm = _row_tile(n)
    row = lambda wdt: pl.BlockSpec((tm, wdt), lambda i: (i, 0))
    return pl.pallas_call(
        _post_kernel,
        grid=(n // tm,),
        in_specs=[row(d), row(oa.shape[1]), row(ob.shape[1]),
                  _const_spec(woa.shape), _const_spec(wob.shape), _const_spec((1, d)), _const_spec((1, d)),
                  _const_spec((d, d_ff)), _const_spec((d, d_ff)), _const_spec((d_ff, d)),
                  _const_spec((1, d)), _const_spec((1, d))],
        out_specs=row(d),
        out_shape=jax.ShapeDtypeStruct((n, d), jnp.float32),
        scratch_shapes=[pltpu.VMEM((tm, d_ff), MXU_DTYPE)],
        compiler_params=pltpu.CompilerParams(
            dimension_semantics=("arbitrary",), vmem_limit_bytes=V7X_VMEM_LIMIT_BYTES),
        name="post",
    )(h, oa, ob, woa, wob, g2, b2, wg, wu, wd, g3, b3)


def _prepare_w_in(w_in):
    d = w_in.shape[0]
    sizes = (N_HEADS_A * HEAD_DIM, N_KV_A * HEAD_DIM, N_KV_A * HEAD_DIM, N_IDX_HEADS * IDX_DIM, IDX_DIM,
             N_IDX_HEADS, N_HEADS_B * HEAD_DIM, N_HEADS_B * HEAD_DIM, N_HEADS_B * HEAD_DIM, N_HEADS_B)
    offs = [0]
    for s in sizes:
        offs.append(offs[-1] + s)
    qa, ka, va, qi, ki, wi, qb, kb, vb, fb = (w_in[:, offs[k]:offs[k + 1]] for k in range(10))
    qa = qa.reshape(d, N_HEADS_A, HEAD_DIM)[:, jnp.array(_QA_HEAD_ORDER)].reshape(d, -1)
    pad = jnp.zeros((d, LANES - N_IDX_HEADS - N_HEADS_B), w_in.dtype)
    w = jnp.concatenate([qa, ka, va, qi, ki, ki, wi, fb, pad, qb, kb, vb], axis=1)
    assert w.shape[1] == _C_END
    return w.astype(MXU_DTYPE)


def _rope_tables(pos):
    half = ROT_DIM // 2
    inv_freq = ROPE_THETA ** (-jnp.arange(half, dtype=jnp.float32) * 2.0 / ROT_DIM)
    ang = pos.astype(jnp.float32)[:, None] * inv_freq[None, :]
    cos, sin = jnp.cos(ang), jnp.sin(ang)
    ones = jnp.ones((pos.shape[0], HEAD_DIM - ROT_DIM), jnp.float32)
    cos64 = jnp.concatenate([cos, cos, ones], axis=1)
    sin64 = jnp.concatenate([-sin, sin, jnp.zeros_like(ones)], axis=1)
    return jnp.tile(cos64, (1, LANES // HEAD_DIM)), jnp.tile(sin64, (1, LANES // HEAD_DIM))


def _project_tokens(h, w_in_p, bf_p, pos, rows_per_seq):
    n = h.shape[0]
    tm = _row_tile(n)
    cos, sin = _rope_tables(pos)
    if tm <= rows_per_seq:
        assert rows_per_seq % tm == 0
        tab_blocks = rows_per_seq // tm
    else:
        assert tm % rows_per_seq == 0
        cos = jnp.tile(cos, (tm // rows_per_seq, 1))
        sin = jnp.tile(sin, (tm // rows_per_seq, 1))
        tab_blocks = 1
    return _project(h, w_in_p, bf_p, cos, sin, tm, tab_blocks)


def _blocked_cum(logf_keys, kb):
    b, s, h = logf_keys.shape
    cum = _cumsum_keys(jnp.transpose(logf_keys, (0, 2, 1)))
    return jnp.transpose(cum.reshape(b, h, s // kb, kb), (0, 2, 1, 3))


def _pad_keys(x, s_pad):
    return jnp.pad(x, ((0, 0), (0, s_pad - x.shape[1]), (0, 0)))


def kernel(x_prompt, x_sample, cache_k_a, cache_v_a, cache_kidx_a, cache_k_b, cache_v_b, cache_logf_b,
           w_in, b_f, w_out, ln1_g, ln1_b, ffn1_w_gate, ffn1_w_up, ffn1_w_down,
           ln2_g, ln2_b, ln3_g, ln3_b, ffn2_w_gate, ffn2_w_up, ffn2_w_down):
    assert w_in.shape[0] == DEPTH
    bsz, seq, d = x_prompt.shape
    dbs, dseq, _ = x_sample.shape
    past = cache_k_a.shape[2]
    bf16 = MXU_DTYPE

    w_in_p = _prepare_w_in(w_in[0])
    bf_p = jnp.zeros((1, LANES), jnp.float32).at[0, N_IDX_HEADS:N_IDX_HEADS + N_HEADS_B].set(b_f[0])
    order = jnp.array(_QA_HEAD_ORDER)
    w_out_a = w_out[0, :N_HEADS_A * HEAD_DIM].reshape(N_HEADS_A, HEAD_DIM, d)[order].reshape(-1, d).astype(bf16)
    w_out_b = w_out[0, N_HEADS_A * HEAD_DIM:].astype(bf16)
    ffn1 = (ffn1_w_gate[0].astype(bf16), ffn1_w_up[0].astype(bf16), ffn1_w_down[0].astype(bf16))
    ffn2 = (ffn2_w_gate[0].astype(bf16), ffn2_w_up[0].astype(bf16), ffn2_w_down[0].astype(bf16))
    vec = lambda a: a[0].reshape(1, d)

    def layer(x, pos, rows_per_seq, attend):
        n = x.shape[0] * x.shape[1]
        h = _ffn_ln(x.reshape(n, d), *ffn1, vec(ln1_g), vec(ln1_b))
        proj = _project_tokens(h, w_in_p, bf_p, pos, rows_per_seq)
        oa, ob = attend(proj)
        y = _post(h, oa, ob, w_out_a, w_out_b, vec(ln2_g), vec(ln2_b), *ffn2, vec(ln3_g), vec(ln3_b))
        return y.reshape(x.shape), proj

    tq_p = min(Q_TILE, seq)
    kb_p = min(KEY_BLOCK, seq)

    def attend_prompt(proj):
        qa, _, _, qi, _, wi, qb, _, _, logf, kab, vab, kib, kbb, vbb = proj
        ck = _blocked_cum(logf.reshape(bsz, seq, N_HEADS_B), kb_p)
        oa = _dsa(qi, wi, qa, kib, kab, vab, batch=bsz, tq=tq_p, kb=kb_p, q_off=0,
                  k_sel=min(TOPK_MAX, seq // 4))
        ob = _fox(qb, kbb, vbb, ck, batch=bsz, tq=tq_p, kb=kb_p, q_off=0)
        return oa, ob

    y_p, proj_p = layer(x_prompt, jnp.arange(seq, dtype=jnp.int32), seq, attend_prompt)

    total = past + dseq
    kb_s = min(SAMPLE_KEY_BLOCK, past)
    s_pad = -(-total // kb_s) * kb_s

    def attend_sample(proj):
        qa, _, _, qi, _, wi, qb, _, _, logf, kab, vab, kib, kbb, vbb = proj
        new = lambda a: a.reshape(dbs, dseq, -1)
        keys = lambda c, nw: _pad_keys(
            jnp.concatenate([c.reshape(dbs, past, -1).astype(bf16), new(nw)], axis=1), s_pad
        ).reshape(dbs * s_pad, -1)
        kidx2 = jnp.concatenate([cache_kidx_a[0], cache_kidx_a[0]], axis=-1)
        logf_all = _pad_keys(jnp.concatenate([cache_logf_b[0].astype(jnp.float32), new(logf)], axis=1), s_pad)
        ck = _blocked_cum(logf_all, kb_s)
        oa = _dsa(qi, wi, qa, keys(kidx2, kib), keys(cache_k_a[0], kab), keys(cache_v_a[0], vab),
                  batch=dbs, tq=dseq, kb=kb_s, q_off=past, k_sel=min(TOPK_MAX, total // 4))
        ob = _fox(qb, keys(cache_k_b[0], kbb), keys(cache_v_b[0], vbb), ck,
                  batch=dbs, tq=dseq, kb=kb_s, q_off=past)
        return oa, ob

    y_s, proj_s = layer(x_sample, past + jnp.arange(dseq, dtype=jnp.int32), dseq, attend_sample)

    def rows(proj, b, t):
        _, ka, va, _, kidx, _, _, kbv, vbv, logf = proj[:10]
        return (ka.reshape(1, b, t, N_KV_A, HEAD_DIM), va.reshape(1, b, t, N_KV_A, HEAD_DIM),
                kidx.reshape(1, b, t, IDX_DIM), kbv.reshape(1, b, t, N_HEADS_B, HEAD_DIM),
                vbv.reshape(1, b, t, N_HEADS_B, HEAD_DIM), logf.reshape(1, b, t, N_HEADS_B))

    return (y_p, y_s) + rows(proj_p, bsz, seq) + rows(proj_s, dbs, dseq)
```

```python
import functools
import math

import jax
import jax.numpy as jnp
from jax import lax
from jax.experimental import pallas as pl
from jax.experimental.pallas import tpu as pltpu

CHUNK = 64
_CHUNK_SHIFT = 6
HEAD_DIM = 64
N_HEADS_A = 8
N_KV_A = 2
N_IDX_HEADS = 8
IDX_DIM = 64
TOPK_MAX = 256
N_HEADS_B = 8
ROT_DIM = HEAD_DIM // 4
ROPE_THETA = 500000.0
LN_EPS = 1e-5
DEPTH = 1
ALPHA = (2.0 * DEPTH) ** 0.25

MXU_DTYPE = jnp.bfloat16

LANES = 128
V7X_VMEM_LIMIT_BYTES = 60 * 1024 * 1024

ROW_TILE = 512
FF_CHUNK = 256
Q_TILE = 512
KEY_BLOCK = 512
SAMPLE_KEY_BLOCK = 256
ROW_CHUNK = 64

INT_MIN = -(2 ** 31)
INT16_MIN, INT16_MAX = -(2 ** 15), 2 ** 15 - 1
NEG_INF = float("-inf")
F32_LOWEST = float(jnp.finfo(jnp.float32).min)
LOG2E = math.log2(math.e)

_C_QA, _C_KA, _C_VA, _C_QI, _C_KI, _C_WF, _C_QB, _C_KB, _C_VB, _C_END = (
    0, 512, 640, 768, 1280, 1408, 1536, 2048, 2560, 3072)
_QA_HEAD_ORDER = (0, 4, 1, 5, 2, 6, 3, 7)


def _nt_dot(a, b):
    return lax.dot_general(a, b, (((1,), (1,)), ((), ())), preferred_element_type=jnp.float32)


def _dot(a, b):
    return jnp.dot(a, b, preferred_element_type=jnp.float32)


def _layer_norm(x, g, b):
    mu = jnp.mean(x, axis=-1, keepdims=True)
    xc = x - mu
    var = jnp.mean(xc * xc, axis=-1, keepdims=True)
    return xc * lax.rsqrt(var + LN_EPS) * g + b


def _macaron_half(x, wg_ref, wu_ref, wd_ref, g, b, act_ref):
    xb = x.astype(MXU_DTYPE)
    d_ff = wg_ref.shape[1]
    for c in range(d_ff // FF_CHUNK):
        cols = slice(c * FF_CHUNK, (c + 1) * FF_CHUNK)
        gate = _dot(xb, wg_ref[:, cols])
        up = _dot(xb, wu_ref[:, cols])
        act_ref[:, cols] = (gate * jax.nn.sigmoid(gate) * up).astype(MXU_DTYPE)
    down = _dot(act_ref[...], wd_ref[...])
    return _layer_norm(ALPHA * x + 0.5 * down, g, b)


def _ffn_ln_kernel(x_ref, wg_ref, wu_ref, wd_ref, g_ref, b_ref, o_ref, act_ref):
    o_ref[...] = _macaron_half(x_ref[...], wg_ref, wu_ref, wd_ref, g_ref[...], b_ref[...], act_ref)


def _const_spec(shape):
    return pl.BlockSpec(shape, lambda *_: (0,) * len(shape), pipeline_mode=pl.Buffered(1))


def _row_tile(n):
    tm = ROW_TILE
    while n % tm:
        tm //= 2
    return tm


def _ffn_ln(x, wg, wu, wd, g, b):
    n, d = x.shape
    d_ff = wg.shape[1]
    tm = _row_tile(n)
    return pl.pallas_call(
        _ffn_ln_kernel,
        grid=(n // tm,),
        in_specs=[
            pl.BlockSpec((tm, d), lambda i: (i, 0)),
            _const_spec((d, d_ff)), _const_spec((d, d_ff)), _const_spec((d_ff, d)),
            _const_spec((1, d)), _const_spec((1, d)),
        ],
        out_specs=pl.BlockSpec((tm, d), lambda i: (i, 0)),
        out_shape=jax.ShapeDtypeStruct((n, d), jnp.float32),
        scratch_shapes=[pltpu.VMEM((tm, d_ff), MXU_DTYPE)],
        compiler_params=pltpu.CompilerParams(
            dimension_semantics=("arbitrary",), vmem_limit_bytes=V7X_VMEM_LIMIT_BYTES),
        name="ffn_ln",
    )(x, wg, wu, wd, g, b)


def _rope(x, cos, sin):
    lane = lax.broadcasted_iota(jnp.int32, x.shape, 1) & (HEAD_DIM - 1)
    partner = jnp.where(lane < ROT_DIM // 2,
                        pltpu.roll(x, LANES - ROT_DIM // 2, 1),
                        pltpu.roll(x, ROT_DIM // 2, 1))
    return x * cos + partner * sin


def _project_kernel(h_ref, w_ref, bf_ref, cos_ref, sin_ref,
                    qa_ref, ka_ref, va_ref, qi_ref, kidx_ref, wi_ref, qb_ref, kb_ref, vb_ref, logf_ref,
                    kab_ref, vab_ref, kib_ref, kbb_ref, vbb_ref):
    hb = h_ref[...].astype(MXU_DTYPE)
    cos = cos_ref[...]
    sin = sin_ref[...]
    q_scale = HEAD_DIM ** -0.5 * LOG2E

    def part(lo, hi):
        return _dot(hb, w_ref[:, lo:hi])

    def rope_groups(z):
        return jnp.concatenate(
            [_rope(z[:, g * LANES:(g + 1) * LANES], cos, sin) for g in range(z.shape[1] // LANES)], axis=1)

    qa_ref[...] = (rope_groups(part(_C_QA, _C_KA)) * q_scale).astype(MXU_DTYPE)
    ka = _rope(part(_C_KA, _C_VA), cos, sin)
    ka_ref[...] = ka
    kab_ref[...] = ka.astype(MXU_DTYPE)
    va = part(_C_VA, _C_QI)
    va_ref[...] = va
    vab_ref[...] = va.astype(MXU_DTYPE)
    qi_ref[...] = (rope_groups(part(_C_QI, _C_KI)) * (IDX_DIM ** -0.5)).astype(MXU_DTYPE)
    ki = _rope(part(_C_KI, _C_WF), cos, sin)
    kidx_ref[...] = ki[:, :IDX_DIM]
    kib_ref[...] = ki.astype(MXU_DTYPE)
    wf = part(_C_WF, _C_QB)
    wi_ref[...] = wf[:, :N_IDX_HEADS] * (N_IDX_HEADS ** -0.5)
    f = wf + bf_ref[...]
    logf = -(jnp.maximum(-f, 0.0) + jnp.log1p(jnp.exp(-jnp.abs(f))))
    logf_ref[...] = pltpu.roll(logf, LANES - N_IDX_HEADS, 1)[:, :N_HEADS_B]
    qb_ref[...] = (part(_C_QB, _C_KB) * q_scale).astype(MXU_DTYPE)
    kb = part(_C_KB, _C_VB)
    kb_ref[...] = kb
    kbb_ref[...] = kb.astype(MXU_DTYPE)
    vb = part(_C_VB, _C_END)
    vb_ref[...] = vb
    vbb_ref[...] = vb.astype(MXU_DTYPE)


def _project(h, w, bf, cos, sin, tm, tab_blocks):
    n, d = h.shape
    f32, bf16 = jnp.float32, MXU_DTYPE
    widths = [(512, bf16), (128, f32), (128, f32), (512, bf16), (IDX_DIM, f32), (N_IDX_HEADS, f32),
              (512, bf16), (512, f32), (512, f32), (N_HEADS_B, f32),
              (128, bf16), (128, bf16), (128, bf16), (512, bf16), (512, bf16)]
    row = lambda wdt: pl.BlockSpec((tm, wdt), lambda i: (i, 0))
    tab = pl.BlockSpec((tm, LANES), lambda i: (i % tab_blocks, 0))
    return pl.pallas_call(
        _project_kernel,
        grid=(n // tm,),
        in_specs=[row(d), _const_spec(w.shape), _const_spec((1, LANES)), tab, tab],
        out_specs=[row(wdt) for wdt, _ in widths],
        out_shape=[jax.ShapeDtypeStruct((n, wdt), dt) for wdt, dt in widths],
        compiler_params=pltpu.CompilerParams(
            dimension_semantics=("arbitrary",), vmem_limit_bytes=V7X_VMEM_LIMIT_BYTES),
        name="project",
    )(h, w, bf, cos, sin)


def _cumsum_kernel(x_ref, o_ref, *, seg):
    r = lax.broadcasted_iota(jnp.int32, (seg, seg), 0)
    c = lax.broadcasted_iota(jnp.int32, (seg, seg), 1)
    tri = (r <= c).astype(jnp.float32)
    carry = jnp.zeros((x_ref.shape[1], 1), jnp.float32)
    for s in range(x_ref.shape[2] // seg):
        cols = slice(s * seg, (s + 1) * seg)
        part = jnp.dot(x_ref[0, :, cols], tri, preferred_element_type=jnp.float32,
                       precision=lax.Precision.HIGHEST) + carry
        o_ref[0, :, cols] = part * LOG2E
        carry = part[:, seg - 1:seg]


def _cumsum_keys(x):
    b, h, s = x.shape
    seg = 2 * LANES if s % (2 * LANES) == 0 else LANES
    assert s % seg == 0
    return pl.pallas_call(
        functools.partial(_cumsum_kernel, seg=seg),
        grid=(b,),
        in_specs=[pl.BlockSpec((1, h, s), lambda i: (i, 0, 0))],
        out_specs=pl.BlockSpec((1, h, s), lambda i: (i, 0, 0)),
        out_shape=jax.ShapeDtypeStruct((b, h, s), jnp.float32),
        compiler_params=pltpu.CompilerParams(dimension_semantics=("arbitrary",)),
        name="cumsum_keys",
    )(x)


def _num_key_blocks(i, tq, kb, q_off):
    return (q_off + (i + 1) * tq + kb - 1) // kb


def _lane_groups(x):
    return [x[:, g * LANES:(g + 1) * LANES] for g in range(x.shape[1] // LANES)]


def _softmax_step(s, h, m_ref, l_ref, acc_ref, v_ones):
    groups = _lane_groups(s)
    smax = functools.reduce(jnp.maximum, groups)
    m_old = m_ref[h]
    m_new = jnp.maximum(m_old, jnp.max(smax, axis=-1, keepdims=True))
    m_safe = jnp.maximum(m_new, F32_LOWEST)
    corr = jnp.exp2(m_old - m_safe)
    p = jnp.concatenate([jnp.exp2(g - m_safe) for g in groups], axis=1).astype(MXU_DTYPE)
    pv = _dot(p, v_ones)
    acc_ref[h] = corr * acc_ref[h] + pv[:, :LANES]
    l_ref[h] = corr * l_ref[h] + pv[:, LANES:]
    m_ref[h] = m_new


def _store_values_and_ones(v_ref, lanes, vo_ref):
    vo_ref[:, :LANES] = v_ref[:, lanes]
    vo_ref[:, LANES:] = jnp.ones((vo_ref.shape[0], LANES), vo_ref.dtype)


def _store_half_masks(q_ref, qm_ref):
    for p in range(q_ref.shape[1] // LANES):
        q2 = q_ref[:, p * LANES:(p + 1) * LANES]
        lane = lax.broadcasted_iota(jnp.int32, q2.shape, 1)
        zero = jnp.zeros_like(q2)
        qm_ref[2 * p] = jnp.where(lane < HEAD_DIM, q2, zero)
        qm_ref[2 * p + 1] = jnp.where(lane >= HEAD_DIM, q2, zero)


def _init_state(m_ref, l_ref, acc_ref):
    m_ref[...] = jnp.full(m_ref.shape, NEG_INF, jnp.float32)
    l_ref[...] = jnp.zeros(l_ref.shape, jnp.float32)
    acc_ref[...] = jnp.zeros(acc_ref.shape, jnp.float32)


def _write_pairs(o_ref, l_ref, acc_ref):
    for p in range(o_ref.shape[1] // LANES):
        outs = [acc_ref[h] / l_ref[h] for h in (2 * p, 2 * p + 1)]
        lane = lax.broadcasted_iota(jnp.int32, outs[0].shape, 1)
        o_ref[:, p * LANES:(p + 1) * LANES] = jnp.where(lane < HEAD_DIM, outs[0], outs[1]).astype(o_ref.dtype)


def _attention_scratch(tq, n_heads):
    return [pltpu.VMEM((n_heads, tq, LANES), MXU_DTYPE),
            pltpu.VMEM((n_heads, tq, LANES), jnp.float32),
            pltpu.VMEM((n_heads, tq, LANES), jnp.float32),
            pltpu.VMEM((n_heads, tq, LANES), jnp.float32)]


def _fox_kernel(q_ref, k_ref, v_ref, ck_ref, o_ref, vo_ref, qm_ref, m_ref, l_ref, acc_ref, *, tq, kb, q_off):
    i = pl.program_id(1)
    nb = _num_key_blocks(i, tq, kb, q_off)
    assert kb % tq == 0 and q_off % tq == 0

    @pl.when(i == 0)
    def _():
        for p in range(N_HEADS_B // 2):
            _store_values_and_ones(v_ref, slice(p * LANES, (p + 1) * LANES), vo_ref.at[p])

    _store_half_masks(q_ref, qm_ref)
    _init_state(m_ref, l_ref, acc_ref)

    def block(j, masked):
        start = pl.multiple_of(j * kb, kb)
        ck = ck_ref[0, j]
        if masked:
            row = q_off + i * tq + lax.broadcasted_iota(jnp.int32, (tq, kb), 0)
            col = j * kb + lax.broadcasted_iota(jnp.int32, (tq, kb), 1)
            causal = col <= row
        for h in range(N_HEADS_B):
            lanes = slice((h // 2) * LANES, (h // 2 + 1) * LANES)
            s = _nt_dot(qm_ref[h], k_ref[pl.ds(start, kb), lanes]) - ck[h:h + 1, :]
            if masked:
                s = jnp.where(causal, s, NEG_INF)
            _softmax_step(s, h, m_ref, l_ref, acc_ref, vo_ref[h // 2, pl.ds(start, kb), :])

    def full_block(j, _):
        block(j, masked=False)
        return 0

    lax.fori_loop(0, nb - 1, full_block, 0)
    block(nb - 1, masked=True)
    _write_pairs(o_ref, l_ref, acc_ref)


def _fox(q, k, v, ck, *, batch, tq, kb, q_off):
    n, width = q.shape
    t_q = n // batch
    s_len = k.shape[0] // batch
    nq = t_q // tq
    return pl.pallas_call(
        functools.partial(_fox_kernel, tq=tq, kb=kb, q_off=q_off),
        grid=(batch, nq),
        in_specs=[
            pl.BlockSpec((tq, width), lambda b, i: (b * nq + i, 0)),
            pl.BlockSpec((s_len, width), lambda b, i: (b, 0)),
            pl.BlockSpec((s_len, width), lambda b, i: (b, 0)),
            pl.BlockSpec((1,) + ck.shape[1:], lambda b, i: (b, 0, 0, 0)),
        ],
        out_specs=pl.BlockSpec((tq, width), lambda b, i: (b * nq + i, 0)),
        out_shape=jax.ShapeDtypeStruct((n, width), MXU_DTYPE),
        scratch_shapes=[pltpu.VMEM((N_HEADS_B // 2, s_len, 2 * LANES), MXU_DTYPE)]
        + _attention_scratch(tq, N_HEADS_B),
        compiler_params=pltpu.CompilerParams(
            dimension_semantics=("arbitrary", "arbitrary"), vmem_limit_bytes=V7X_VMEM_LIMIT_BYTES),
        name="fox_attention",
    )(q, k, v, ck)


def _dsa_kernel(qi_ref, wi_ref, qa_ref, ki_ref, ka_ref, va_ref, o_ref,
                key_ref, half_ref, bias_ref, thr_ref, cand_ref, cand16_ref, lim_ref, cnt_ref, vo_ref,
                qm_ref, m_ref, l_ref, acc_ref,
                *, tq, kb, q_off, k_sel):
    i = pl.program_id(1)
    nb = _num_key_blocks(i, tq, kb, q_off)
    k_f = jnp.float32(k_sel)
    rc = min(ROW_CHUNK, tq)
    chunks = [slice(r * rc, (r + 1) * rc) for r in range(tq // rc)]
    groups = [slice(g * LANES, (g + 1) * LANES) for g in range(kb // LANES)]

    def replicate(x):
        return jnp.broadcast_to(x, (tq, LANES))

    _store_half_masks(qi_ref, qm_ref)
    wi = wi_ref[...]

    def score_block(j, _):
        start = pl.multiple_of(j * kb, kb)
        kblk = ki_ref[pl.ds(start, kb), :]
        score = jnp.zeros((tq, kb), jnp.float32)
        for h in range(N_IDX_HEADS):
            score = score + wi[:, h:h + 1] * jnp.maximum(_nt_dot(qm_ref[h], kblk), 0.0)
        bits = lax.bitcast_convert_type(score, jnp.int32)
        key = jnp.where(bits < 0, bits ^ jnp.int32(0x7FFFFFFF), bits)
        row = q_off + i * tq + lax.broadcasted_iota(jnp.int32, (tq, kb), 0)
        col = j * kb + lax.broadcasted_iota(jnp.int32, (tq, kb), 1)
        admissible = (col >> _CHUNK_SHIFT) <= (row >> _CHUNK_SHIFT)
        key = jnp.where(admissible, key, jnp.int32(INT_MIN))
        key_ref[j] = key
        half_ref[j] = (key >> 16).astype(jnp.int16)
        return 0

    lax.fori_loop(0, nb, score_block, 0)

    def count(pred):
        def body(j, acc):
            parts = []
            for rows in chunks:
                hits = [jnp.where(pred(key_ref[j, rows, g], rows, j * kb + g.start), 1.0, 0.0) for g in groups]
                parts.append(functools.reduce(jnp.add, hits))
            return acc + jnp.concatenate(parts, axis=0)
        acc = lax.fori_loop(0, nb, body, jnp.zeros((tq, LANES), jnp.float32))
        return replicate(jnp.sum(acc, axis=-1, keepdims=True))

    def count_half_at_least_cand():
        one, zero = jnp.int16(1), jnp.int16(0)

        def body(j, acc):
            parts = []
            for rows in chunks:
                cand = cand16_ref[rows, :]
                hits = [jnp.where(half_ref[j, rows, g] >= cand, one, zero) for g in groups]
                parts.append(functools.reduce(jnp.add, hits))
            return acc + jnp.concatenate(parts, axis=0)
        acc = lax.fori_loop(0, nb, body, jnp.zeros((tq, LANES), jnp.int16))
        return replicate(jnp.sum(acc.astype(jnp.float32), axis=-1, keepdims=True))

    def bisect_half():
        def test(cand):
            cand_ref[...] = cand
            cand16_ref[...] = cand.astype(jnp.int16)
            c = count_half_at_least_cand()
            take = c >= k_f
            thr_ref[...] = jnp.where(take, cand, thr_ref[...])
            cnt_ref[...] = jnp.where(take, c, cnt_ref[...])

        thr_ref[...] = jnp.full((tq, LANES), INT16_MIN, jnp.int32)
        test(jnp.zeros((tq, LANES), jnp.int32))

        def step(b, _):
            test(thr_ref[...] + (jnp.int32(1) << (jnp.int32(14) - b)))
            return 0

        lax.fori_loop(0, 15, step, 0)
        return thr_ref[...]

    cnt_ref[...] = jnp.full((tq, LANES), k_f, jnp.float32)
    lim_ref[...] = bisect_half()

    def low_halves(j, _):
        for rows in chunks:
            hi = lim_ref[rows, :].astype(jnp.int16)
            for g in groups:
                low = ((key_ref[j, rows, g] & 0xFFFF) + INT16_MIN).astype(jnp.int16)
                high = half_ref[j, rows, g]
                other = jnp.where(high > hi, jnp.int16(INT16_MAX), jnp.int16(INT16_MIN))
                half_ref[j, rows, g] = jnp.where(high == hi, low, other)
        return 0

    lax.fori_loop(0, nb, low_halves, 0)
    thr_lo = bisect_half()
    thr_ref[...] = (lim_ref[...] << 16) + (thr_lo - INT16_MIN)

    thr = thr_ref[...]
    has_thr = thr > jnp.int32(INT_MIN)
    surplus = jnp.logical_and(has_thr, cnt_ref[...] > k_f)
    big = jnp.int32(2 ** 30)

    def tie_limit():
        need = k_f - count(lambda key, rows, first: key > thr_ref[rows, :])
        lim_ref[...] = jnp.zeros((tq, LANES), jnp.int32)

        def tied_before_cand(key, rows, first):
            col = first + lax.broadcasted_iota(jnp.int32, key.shape, 1)
            return jnp.logical_and(key == thr_ref[rows, :], col < cand_ref[rows, :])

        def step(b, _):
            cand = lim_ref[...] + (jnp.int32(1) << (jnp.int32(14) - b))
            cand_ref[...] = cand
            c = count(tied_before_cand)
            lim_ref[...] = jnp.where(c < need, cand, lim_ref[...])
            return 0

        lax.fori_loop(0, 15, step, 0)
        return jnp.where(surplus, lim_ref[...] + 1, big)

    any_surplus = jnp.max(jnp.where(surplus, 1.0, 0.0)) > 0.0
    limit = lax.cond(any_surplus, tie_limit, lambda: jnp.full((tq, LANES), big, jnp.int32))
    lim_ref[...] = jnp.where(has_thr, limit, 0)

    def bias_block(j, _):
        for rows in chunks:
            for g in groups:
                key = key_ref[j, rows, g]
                col = j * kb + g.start + lax.broadcasted_iota(jnp.int32, key.shape, 1)
                tied = jnp.logical_and(key == thr_ref[rows, :], col < lim_ref[rows, :])
                sel = jnp.logical_or(key > thr_ref[rows, :], tied)
                bias_ref[j, rows, g] = jnp.where(sel, 0.0, NEG_INF)
        return 0

    lax.fori_loop(0, nb, bias_block, 0)

    @pl.when(i == 0)
    def _():
        _store_values_and_ones(va_ref, slice(0, LANES), vo_ref)

    _store_half_masks(qa_ref, qm_ref)
    _init_state(m_ref, l_ref, acc_ref)

    def attend(j, _):
        start = pl.multiple_of(j * kb, kb)
        for h in range(N_HEADS_A):
            s = _nt_dot(qm_ref[h], ka_ref[pl.ds(start, kb), :]) + bias_ref[j]
            _softmax_step(s, h, m_ref, l_ref, acc_ref, vo_ref[pl.ds(start, kb), :])
        return 0

    lax.fori_loop(0, nb, attend, 0)
    _write_pairs(o_ref, l_ref, acc_ref)


def _dsa(qi, wi, qa, ki, ka, va, *, batch, tq, kb, q_off, k_sel):
    n, width = qa.shape
    t_q = n // batch
    s_len = ka.shape[0] // batch
    nq = t_q // tq
    nblk = s_len // kb
    qspec = lambda wdt: pl.BlockSpec((tq, wdt), lambda b, i: (b * nq + i, 0))
    kspec = pl.BlockSpec((s_len, LANES), lambda b, i: (b, 0))
    return pl.pallas_call(
        functools.partial(_dsa_kernel, tq=tq, kb=kb, q_off=q_off, k_sel=k_sel),
        grid=(batch, nq),
        in_specs=[qspec(width), qspec(N_IDX_HEADS), qspec(width), kspec, kspec, kspec],
        out_specs=qspec(width),
        out_shape=jax.ShapeDtypeStruct((n, width), MXU_DTYPE),
        scratch_shapes=[pltpu.VMEM((nblk, tq, kb), jnp.int32),
                        pltpu.VMEM((nblk, tq, kb), jnp.int16),
                        pltpu.VMEM((nblk, tq, kb), jnp.float32),
                        pltpu.VMEM((tq, LANES), jnp.int32),
                        pltpu.VMEM((tq, LANES), jnp.int32),
                        pltpu.VMEM((tq, LANES), jnp.int16),
                        pltpu.VMEM((tq, LANES), jnp.int32),
                        pltpu.VMEM((tq, LANES), jnp.float32),
                        pltpu.VMEM((s_len, 2 * LANES), MXU_DTYPE),
                        ] + _attention_scratch(tq, N_HEADS_A),
        compiler_params=pltpu.CompilerParams(
            dimension_semantics=("arbitrary", "arbitrary"), vmem_limit_bytes=V7X_VMEM_LIMIT_BYTES),
        name="dsa_attention",
    )(qi, wi, qa, ki, ka, va)


def _post_kernel(h_ref, oa_ref, ob_ref, woa_ref, wob_ref, g2_ref, b2_ref,
                 wg_ref, wu_ref, wd_ref, g3_ref, b3_ref, o_ref, act_ref):
    mix = _dot(oa_ref[...], woa_ref[...]) + _dot(ob_ref[...], wob_ref[...])
    h2 = _layer_norm(ALPHA * h_ref[...] + mix, g2_ref[...], b2_ref[...])
    o_ref[...] = _macaron_half(h2, wg_ref, wu_ref, wd_ref, g3_ref[...], b3_ref[...], act_ref)


def _post(h, oa, ob, woa, wob, g2, b2, wg, wu, wd, g3, b3):
    n, d = h.shape
    d_ff = wg.shape[1]
    tm = _row_tile(n)
    row = lambda wdt: pl.BlockSpec((tm, wdt), lambda i: (i, 0))
    return pl.pallas_call(
        _post_kernel,
        grid=(n // tm,),
        in_specs=[row(d), row(oa.shape[1]), row(ob.shape[1]),
                  _const_spec(woa.shape), _const_spec(wob.shape), _const_spec((1, d)), _const_spec((1, d)),
                  _const_spec((d, d_ff)), _const_spec((d, d_ff)), _const_spec((d_ff, d)),
                  _const_spec((1, d)), _const_spec((1, d))],
        out_specs=row(d),
        out_shape=jax.ShapeDtypeStruct((n, d), jnp.float32),
        scratch_shapes=[pltpu.VMEM((tm, d_ff), MXU_DTYPE)],
        compiler_params=pltpu.CompilerParams(
            dimension_semantics=("arbitrary",), vmem_limit_bytes=V7X_VMEM_LIMIT_BYTES),
        name="post",
    )(h, oa, ob, woa, wob, g2, b2, wg, wu, wd, g3, b3)


def _prepare_w_in(w_in):
    d = w_in.shape[0]
    sizes = (N_HEADS_A * HEAD_DIM, N_KV_A * HEAD_DIM, N_KV_A * HEAD_DIM, N_IDX_HEADS * IDX_DIM, IDX_DIM,
             N_IDX_HEADS, N_HEADS_B * HEAD_DIM, N_HEADS_B * HEAD_DIM, N_HEADS_B * HEAD_DIM, N_HEADS_B)
    offs = [0]
    for s in sizes:
        offs.append(offs[-1] + s)
    qa, ka, va, qi, ki, wi, qb, kb, vb, fb = (w_in[:, offs[k]:offs[k + 1]] for k in range(10))
    qa = qa.reshape(d, N_HEADS_A, HEAD_DIM)[:, jnp.array(_QA_HEAD_ORDER)].reshape(d, -1)
    pad = jnp.zeros((d, LANES - N_IDX_HEADS - N_HEADS_B), w_in.dtype)
    w = jnp.concatenate([qa, ka, va, qi, ki, ki, wi, fb, pad, qb, kb, vb], axis=1)
    assert w.shape[1] == _C_END
    return w.astype(MXU_DTYPE)


def _rope_tables(pos):
    half = ROT_DIM // 2
    inv_freq = ROPE_THETA ** (-jnp.arange(half, dtype=jnp.float32) * 2.0 / ROT_DIM)
    ang = pos.astype(jnp.float32)[:, None] * inv_freq[None, :]
    cos, sin = jnp.cos(ang), jnp.sin(ang)
    ones = jnp.ones((pos.shape[0], HEAD_DIM - ROT_DIM), jnp.float32)
    cos64 = jnp.concatenate([cos, cos, ones], axis=1)
    sin64 = jnp.concatenate([-sin, sin, jnp.zeros_like(ones)], axis=1)
    return jnp.tile(cos64, (1, LANES // HEAD_DIM)), jnp.tile(sin64, (1, LANES // HEAD_DIM))


def _project_tokens(h, w_in_p, bf_p, pos, rows_per_seq):
    n = h.shape[0]
    tm = _row_tile(n)
    cos, sin = _rope_tables(pos)
    if tm <= rows_per_seq:
        assert rows_per_seq % tm == 0
        tab_blocks = rows_per_seq // tm
    else:
        assert tm % rows_per_seq == 0
        cos = jnp.tile(cos, (tm // rows_per_seq, 1))
        sin = jnp.tile(sin, (tm // rows_per_seq, 1))
        tab_blocks = 1
    return _project(h, w_in_p, bf_p, cos, sin, tm, tab_blocks)


def _blocked_cum(logf_keys, kb):
    b, s, h = logf_keys.shape
    cum = _cumsum_keys(jnp.transpose(logf_keys, (0, 2, 1)))
    return jnp.transpose(cum.reshape(b, h, s // kb, kb), (0, 2, 1, 3))


def _pad_keys(x, s_pad):
    return jnp.pad(x, ((0, 0), (0, s_pad - x.shape[1]), (0, 0)))


def kernel(x_prompt, x_sample, cache_k_a, cache_v_a, cache_kidx_a, cache_k_b, cache_v_b, cache_logf_b,
           w_in, b_f, w_out, ln1_g, ln1_b, ffn1_w_gate, ffn1_w_up, ffn1_w_down,
           ln2_g, ln2_b, ln3_g, ln3_b, ffn2_w_gate, ffn2_w_up, ffn2_w_down):
    assert w_in.shape[0] == DEPTH
    bsz, seq, d = x_prompt.shape
    dbs, dseq, _ = x_sample.shape
    past = cache_k_a.shape[2]
    bf16 = MXU_DTYPE

    w_in_p = _prepare_w_in(w_in[0])
    bf_p = jnp.zeros((1, LANES), jnp.float32).at[0, N_IDX_HEADS:N_IDX_HEADS + N_HEADS_B].set(b_f[0])
    order = jnp.array(_QA_HEAD_ORDER)
    w_out_a = w_out[0, :N_HEADS_A * HEAD_DIM].reshape(N_HEADS_A, HEAD_DIM, d)[order].reshape(-1, d).astype(bf16)
    w_out_b = w_out[0, N_HEADS_A * HEAD_DIM:].astype(bf16)
    ffn1 = (ffn1_w_gate[0].astype(bf16), ffn1_w_up[0].astype(bf16), ffn1_w_down[0].astype(bf16))
    ffn2 = (ffn2_w_gate[0].astype(bf16), ffn2_w_up[0].astype(bf16), ffn2_w_down[0].astype(bf16))
    vec = lambda a: a[0].reshape(1, d)

    def layer(x, pos, rows_per_seq, attend):
        n = x.shape[0] * x.shape[1]
        h = _ffn_ln(x.reshape(n, d), *ffn1, vec(ln1_g), vec(ln1_b))
        proj = _project_tokens(h, w_in_p, bf_p, pos, rows_per_seq)
        oa, ob = attend(proj)
        y = _post(h, oa, ob, w_out_a, w_out_b, vec(ln2_g), vec(ln2_b), *ffn2, vec(ln3_g), vec(ln3_b))
        return y.reshape(x.shape), proj

    tq_p = min(Q_TILE, seq)
    kb_p = min(KEY_BLOCK, seq)

    def attend_prompt(proj):
        qa, _, _, qi, _, wi, qb, _, _, logf, kab, vab, kib, kbb, vbb = proj
        ck = _blocked_cum(logf.reshape(bsz, seq, N_HEADS_B), kb_p)
        oa = _dsa(qi, wi, qa, kib, kab, vab, batch=bsz, tq=tq_p, kb=kb_p, q_off=0,
                  k_sel=min(TOPK_MAX, seq // 4))
        ob = _fox(qb, kbb, vbb, ck, batch=bsz, tq=tq_p, kb=kb_p, q_off=0)
        return oa, ob

    y_p, proj_p = layer(x_prompt, jnp.arange(seq, dtype=jnp.int32), seq, attend_prompt)

    total = past + dseq
    kb_s = min(SAMPLE_KEY_BLOCK, past)
    s_pad = -(-total // kb_s) * kb_s

    def attend_sample(proj):
        qa, _, _, qi, _, wi, qb, _, _, logf, kab, vab, kib, kbb, vbb = proj
        new = lambda a: a.reshape(dbs, dseq, -1)
        keys = lambda c, nw: _pad_keys(
            jnp.concatenate([c.reshape(dbs, past, -1).astype(bf16), new(nw)], axis=1), s_pad
        ).reshape(dbs * s_pad, -1)
        kidx2 = jnp.concatenate([cache_kidx_a[0], cache_kidx_a[0]], axis=-1)
        logf_all = _pad_keys(jnp.concatenate([cache_logf_b[0].astype(jnp.float32), new(logf)], axis=1), s_pad)
        ck = _blocked_cum(logf_all, kb_s)
        oa = _dsa(qi, wi, qa, keys(kidx2, kib), keys(cache_k_a[0], kab), keys(cache_v_a[0], vab),
                  batch=dbs, tq=dseq, kb=kb_s, q_off=past, k_sel=min(TOPK_MAX, total // 4))
        ob = _fox(qb, keys(cache_k_b[0], kbb), keys(cache_v_b[0], vbb), ck,
                  batch=dbs, tq=dseq, kb=kb_s, q_off=past)
        return oa, ob

    y_s, proj_s = layer(x_sample, past + jnp.arange(dseq, dtype=jnp.int32), dseq, attend_sample)

    def rows(proj, b, t):
        _, ka, va, _, kidx, _, _, kbv, vbv, logf = proj[:10]
        return (ka.reshape(1, b, t, N_KV_A, HEAD_DIM), va.reshape(1, b, t, N_KV_A, HEAD_DIM),
                kidx.reshape(1, b, t, IDX_DIM), kbv.reshape(1, b, t, N_HEADS_B, HEAD_DIM),
                vbv.reshape(1, b, t, N_HEADS_B, HEAD_DIM), logf.reshape(1, b, t, N_HEADS_B))

    return (y_p, y_s) + rows(proj_p, bsz, seq) + rows(proj_s, dbs, dseq)
```

```python
import functools
import math

import jax
import jax.numpy as jnp
from jax import lax
from jax.experimental import pallas as pl
from jax.experimental.pallas import tpu as pltpu

CHUNK = 64
_CHUNK_SHIFT = 6
HEAD_DIM = 64
N_HEADS_A = 8
N_KV_A = 2
N_IDX_HEADS = 8
IDX_DIM = 64
TOPK_MAX = 256
N_HEADS_B = 8
ROT_DIM = HEAD_DIM // 4
ROPE_THETA = 500000.0
LN_EPS = 1e-5
DEPTH = 1
ALPHA = (2.0 * DEPTH) ** 0.25

MXU_DTYPE = jnp.bfloat16

LANES = 128
V7X_VMEM_LIMIT_BYTES = 60 * 1024 * 1024

ROW_TILE = 512
FF_CHUNK = 256
Q_TILE = 512
KEY_BLOCK = 512
SAMPLE_KEY_BLOCK = 256
ROW_CHUNK = 64

INT_MIN = -(2 ** 31)
INT16_MIN, INT16_MAX = -(2 ** 15), 2 ** 15 - 1
NEG_INF = float("-inf")
F32_LOWEST = float(jnp.finfo(jnp.float32).min)
LOG2E = math.log2(math.e)

_C_QA, _C_KA, _C_VA, _C_QI, _C_KI, _C_WF, _C_QB, _C_KB, _C_VB, _C_END = (
    0, 512, 640, 768, 1280, 1408, 1536, 2048, 2560, 3072)
_QA_HEAD_ORDER = (0, 4, 1, 5, 2, 6, 3, 7)


def _nt_dot(a, b):
    return lax.dot_general(a, b, (((1,), (1,)), ((), ())), preferred_element_type=jnp.float32)


def _dot(a, b):
    return jnp.dot(a, b, preferred_element_type=jnp.float32)


def _layer_norm(x, g, b):
    mu = jnp.mean(x, axis=-1, keepdims=True)
    xc = x - mu
    var = jnp.mean(xc * xc, axis=-1, keepdims=True)
    return xc * lax.rsqrt(var + LN_EPS) * g + b


def _macaron_half(x, wg_ref, wu_ref, wd_ref, g, b, act_ref):
    xb = x.astype(MXU_DTYPE)
    d_ff = wg_ref.shape[1]
    for c in range(d_ff // FF_CHUNK):
        cols = slice(c * FF_CHUNK, (c + 1) * FF_CHUNK)
        gate = _dot(xb, wg_ref[:, cols])
        up = _dot(xb, wu_ref[:, cols])
        act_ref[:, cols] = (gate * jax.nn.sigmoid(gate) * up).astype(MXU_DTYPE)
    down = _dot(act_ref[...], wd_ref[...])
    return _layer_norm(ALPHA * x + 0.5 * down, g, b)


def _ffn_ln_kernel(x_ref, wg_ref, wu_ref, wd_ref, g_ref, b_ref, o_ref, act_ref):
    o_ref[...] = _macaron_half(x_ref[...], wg_ref, wu_ref, wd_ref, g_ref[...], b_ref[...], act_ref)


def _const_spec(shape):
    return pl.BlockSpec(shape, lambda *_: (0,) * len(shape), pipeline_mode=pl.Buffered(1))


def _row_tile(n):
    tm = ROW_TILE
    while n % tm:
        tm //= 2
    return tm


def _ffn_ln(x, wg, wu, wd, g, b):
    n, d = x.shape
    d_ff = wg.shape[1]
    tm = _row_tile(n)
    return pl.pallas_call(
        _ffn_ln_kernel,
        grid=(n // tm,),
        in_specs=[
            pl.BlockSpec((tm, d), lambda i: (i, 0)),
            _const_spec((d, d_ff)), _const_spec((d, d_ff)), _const_spec((d_ff, d)),
            _const_spec((1, d)), _const_spec((1, d)),
        ],
        out_specs=pl.BlockSpec((tm, d), lambda i: (i, 0)),
        out_shape=jax.ShapeDtypeStruct((n, d), jnp.float32),
        scratch_shapes=[pltpu.VMEM((tm, d_ff), MXU_DTYPE)],
        compiler_params=pltpu.CompilerParams(
            dimension_semantics=("arbitrary",), vmem_limit_bytes=V7X_VMEM_LIMIT_BYTES),
        name="ffn_ln",
    )(x, wg, wu, wd, g, b)


def _rope(x, cos, sin):
    lane = lax.broadcasted_iota(jnp.int32, x.shape, 1) & (HEAD_DIM - 1)
    partner = jnp.where(lane < ROT_DIM // 2,
                        pltpu.roll(x, LANES - ROT_DIM // 2, 1),
                        pltpu.roll(x, ROT_DIM // 2, 1))
    return x * cos + partner * sin


def _project_kernel(h_ref, w_ref, bf_ref, cos_ref, sin_ref,
                    qa_ref, ka_ref, va_ref, qi_ref, kidx_ref, wi_ref, qb_ref, kb_ref, vb_ref, logf_ref,
                    kab_ref, vab_ref, kib_ref, kbb_ref, vbb_ref):
    hb = h_ref[...].astype(MXU_DTYPE)
    cos = cos_ref[...]
    sin = sin_ref[...]
    q_scale = HEAD_DIM ** -0.5 * LOG2E

    def part(lo, hi):
        return _dot(hb, w_ref[:, lo:hi])

    def rope_groups(z):
        return jnp.concatenate(
            [_rope(z[:, g * LANES:(g + 1) * LANES], cos, sin) for g in range(z.shape[1] // LANES)], axis=1)

    qa_ref[...] = (rope_groups(part(_C_QA, _C_KA)) * q_scale).astype(MXU_DTYPE)
    ka = _rope(part(_C_KA, _C_VA), cos, sin)
    ka_ref[...] = ka
    kab_ref[...] = ka.astype(MXU_DTYPE)
    va = part(_C_VA, _C_QI)
    va_ref[...] = va
    vab_ref[...] = va.astype(MXU_DTYPE)
    qi_ref[...] = (rope_groups(part(_C_QI, _C_KI)) * (IDX_DIM ** -0.5)).astype(MXU_DTYPE)
    ki = _rope(part(_C_KI, _C_WF), cos, sin)
    kidx_ref[...] = ki[:, :IDX_DIM]
    kib_ref[...] = ki.astype(MXU_DTYPE)
    wf = part(_C_WF, _C_QB)
    wi_ref[...] = wf[:, :N_IDX_HEADS] * (N_IDX_HEADS ** -0.5)
    f = wf + bf_ref[...]
    logf = -(jnp.maximum(-f, 0.0) + jnp.log1p(jnp.exp(-jnp.abs(f))))
    logf_ref[...] = pltpu.roll(logf, LANES - N_IDX_HEADS, 1)[:, :N_HEADS_B]
    qb_ref[...] = (part(_C_QB, _C_KB) * q_scale).astype(MXU_DTYPE)
    kb = part(_C_KB, _C_VB)
    kb_ref[...] = kb
    kbb_ref[...] = kb.astype(MXU_DTYPE)
    vb = part(_C_VB, _C_END)
    vb_ref[...] = vb
    vbb_ref[...] = vb.astype(MXU_DTYPE)


def _project(h, w, bf, cos, sin, tm, tab_blocks):
    n, d = h.shape
    f32, bf16 = jnp.float32, MXU_DTYPE
    widths = [(512, bf16), (128, f32), (128, f32), (512, bf16), (IDX_DIM, f32), (N_IDX_HEADS, f32),
              (512, bf16), (512, f32), (512, f32), (N_HEADS_B, f32),
              (128, bf16), (128, bf16), (128, bf16), (512, bf16), (512, bf16)]
    row = lambda wdt: pl.BlockSpec((tm, wdt), lambda i: (i, 0))
    tab = pl.BlockSpec((tm, LANES), lambda i: (i % tab_blocks, 0))
    return pl.pallas_call(
        _project_kernel,
        grid=(n // tm,),
        in_specs=[row(d), _const_spec(w.shape), _const_spec((1, LANES)), tab, tab],
        out_specs=[row(wdt) for wdt, _ in widths],
        out_shape=[jax.ShapeDtypeStruct((n, wdt), dt) for wdt, dt in widths],
        compiler_params=pltpu.CompilerParams(
            dimension_semantics=("arbitrary",), vmem_limit_bytes=V7X_VMEM_LIMIT_BYTES),
        name="project",
    )(h, w, bf, cos, sin)


def _cumsum_kernel(x_ref, o_ref, *, seg):
    r = lax.broadcasted_iota(jnp.int32, (seg, seg), 0)
    c = lax.broadcasted_iota(jnp.int32, (seg, seg), 1)
    tri = (r <= c).astype(jnp.float32)
    carry = jnp.zeros((x_ref.shape[1], 1), jnp.float32)
    for s in range(x_ref.shape[2] // seg):
        cols = slice(s * seg, (s + 1) * seg)
        part = jnp.dot(x_ref[0, :, cols], tri, preferred_element_type=jnp.float32,
                       precision=lax.Precision.HIGHEST) + carry
        o_ref[0, :, cols] = part * LOG2E
        carry = part[:, seg - 1:seg]


def _cumsum_keys(x):
    b, h, s = x.shape
    seg = 2 * LANES if s % (2 * LANES) == 0 else LANES
    assert s % seg == 0
    return pl.pallas_call(
        functools.partial(_cumsum_kernel, seg=seg),
        grid=(b,),
        in_specs=[pl.BlockSpec((1, h, s), lambda i: (i, 0, 0))],
        out_specs=pl.BlockSpec((1, h, s), lambda i: (i, 0, 0)),
        out_shape=jax.ShapeDtypeStruct((b, h, s), jnp.float32),
        compiler_params=pltpu.CompilerParams(dimension_semantics=("arbitrary",)),
        name="cumsum_keys",
    )(x)


def _num_key_blocks(i, tq, kb, q_off):
    return (q_off + (i + 1) * tq + kb - 1) // kb


def _lane_groups(x):
    return [x[:, g * LANES:(g + 1) * LANES] for g in range(x.shape[1] // LANES)]


def _lane_tile(x, width):
    return jnp.concatenate([x] * (width // LANES), axis=1)


def _softmax_step(s, h, m_ref, l_ref, acc_ref, v_ones):
    groups = _lane_groups(s)
    smax = functools.reduce(jnp.maximum, groups)
    m_old = m_ref[h]
    m_new = jnp.maximum(m_old, jnp.max(smax, axis=-1, keepdims=True))
    m_safe = jnp.maximum(m_new, F32_LOWEST)
    corr = jnp.exp2(m_old - m_safe)
    p = jnp.concatenate([jnp.exp2(g - m_safe) for g in groups], axis=1).astype(MXU_DTYPE)
    pv = _dot(p, v_ones)
    acc_ref[h] = corr * acc_ref[h] + pv[:, :LANES]
    l_ref[h] = corr * l_ref[h] + pv[:, LANES:]
    m_ref[h] = m_new


def _store_values_and_ones(v_ref, lanes, vo_ref):
    vo_ref[:, :LANES] = v_ref[:, lanes]
    vo_ref[:, LANES:] = jnp.ones((vo_ref.shape[0], LANES), vo_ref.dtype)


def _store_half_masks(q_ref, qm_ref):
    for p in range(q_ref.shape[1] // LANES):
        q2 = q_ref[:, p * LANES:(p + 1) * LANES]
        lane = lax.broadcasted_iota(jnp.int32, q2.shape, 1)
        zero = jnp.zeros_like(q2)
        qm_ref[2 * p] = jnp.where(lane < HEAD_DIM, q2, zero)
        qm_ref[2 * p + 1] = jnp.where(lane >= HEAD_DIM, q2, zero)


def _init_state(m_ref, l_ref, acc_ref):
    m_ref[...] = jnp.full(m_ref.shape, NEG_INF, jnp.float32)
    l_ref[...] = jnp.zeros(l_ref.shape, jnp.float32)
    acc_ref[...] = jnp.zeros(acc_ref.shape, jnp.float32)


def _write_pairs(o_ref, l_ref, acc_ref):
    for p in range(o_ref.shape[1] // LANES):
        outs = [acc_ref[h] / l_ref[h] for h in (2 * p, 2 * p + 1)]
        lane = lax.broadcasted_iota(jnp.int32, outs[0].shape, 1)
        o_ref[:, p * LANES:(p + 1) * LANES] = jnp.where(lane < HEAD_DIM, outs[0], outs[1]).astype(o_ref.dtype)


def _attention_scratch(tq, n_heads):
    return [pltpu.VMEM((n_heads, tq, LANES), MXU_DTYPE),
            pltpu.VMEM((n_heads, tq, LANES), jnp.float32),
            pltpu.VMEM((n_heads, tq, LANES), jnp.float32),
            pltpu.VMEM((n_heads, tq, LANES), jnp.float32)]


def _fox_kernel(q_ref, k_ref, v_ref, ck_ref, o_ref, vo_ref, qm_ref, m_ref, l_ref, acc_ref, *, tq, kb, q_off):
    i = pl.program_id(1)
    nb = _num_key_blocks(i, tq, kb, q_off)
    assert kb % tq == 0 and q_off % tq == 0

    @pl.when(i == 0)
    def _():
        for p in range(N_HEADS_B // 2):
            _store_values_and_ones(v_ref, slice(p * LANES, (p + 1) * LANES), vo_ref.at[p])

    _store_half_masks(q_ref, qm_ref)
    _init_state(m_ref, l_ref, acc_ref)

    def block(j, masked):
        start = pl.multiple_of(j * kb, kb)
        ck = ck_ref[0, j]
        if masked:
            row = q_off + i * tq + lax.broadcasted_iota(jnp.int32, (tq, kb), 0)
            col = j * kb + lax.broadcasted_iota(jnp.int32, (tq, kb), 1)
            causal = col <= row
        for h in range(N_HEADS_B):
            lanes = slice((h // 2) * LANES, (h // 2 + 1) * LANES)
            s = _nt_dot(qm_ref[h], k_ref[pl.ds(start, kb), lanes]) - ck[h:h + 1, :]
            if masked:
                s = jnp.where(causal, s, NEG_INF)
            _softmax_step(s, h, m_ref, l_ref, acc_ref, vo_ref[h // 2, pl.ds(start, kb), :])

    def full_block(j, _):
        block(j, masked=False)
        return 0

    lax.fori_loop(0, nb - 1, full_block, 0)
    block(nb - 1, masked=True)
    _write_pairs(o_ref, l_ref, acc_ref)


def _fox(q, k, v, ck, *, batch, tq, kb, q_off):
    n, width = q.shape
    t_q = n // batch
    s_len = k.shape[0] // batch
    nq = t_q // tq
    return pl.pallas_call(
        functools.partial(_fox_kernel, tq=tq, kb=kb, q_off=q_off),
        grid=(batch, nq),
        in_specs=[
            pl.BlockSpec((tq, width), lambda b, i: (b * nq + i, 0)),
            pl.BlockSpec((s_len, width), lambda b, i: (b, 0)),
            pl.BlockSpec((s_len, width), lambda b, i: (b, 0)),
            pl.BlockSpec((1,) + ck.shape[1:], lambda b, i: (b, 0, 0, 0)),
        ],
        out_specs=pl.BlockSpec((tq, width), lambda b, i: (b * nq + i, 0)),
        out_shape=jax.ShapeDtypeStruct((n, width), MXU_DTYPE),
        scratch_shapes=[pltpu.VMEM((N_HEADS_B // 2, s_len, 2 * LANES), MXU_DTYPE)]
        + _attention_scratch(tq, N_HEADS_B),
        compiler_params=pltpu.CompilerParams(
            dimension_semantics=("arbitrary", "arbitrary"), vmem_limit_bytes=V7X_VMEM_LIMIT_BYTES),
        name="fox_attention",
    )(q, k, v, ck)


def _dsa_kernel(qi_ref, wi_ref, qa_ref, ki_ref, ka_ref, va_ref, o_ref,
                key_ref, half_ref, bias_ref, thr_ref, cand_ref, cand16_ref, lim_ref, cnt_ref, vo_ref,
                qm_ref, m_ref, l_ref, acc_ref,
                *, tq, kb, q_off, k_sel):
    i = pl.program_id(1)
    nb = _num_key_blocks(i, tq, kb, q_off)
    k_f = jnp.float32(k_sel)
    rc = min(ROW_CHUNK, tq)
    chunks = [slice(r * rc, (r + 1) * rc) for r in range(tq // rc)]
    groups = [slice(g * LANES, (g + 1) * LANES) for g in range(kb // LANES)]

    def replicate(x):
        return jnp.broadcast_to(x, (tq, LANES))

    _store_half_masks(qi_ref, qm_ref)
    wi = wi_ref[...]

    def score_block(j, _):
        start = pl.multiple_of(j * kb, kb)
        kblk = ki_ref[pl.ds(start, kb), :]
        score = jnp.zeros((tq, kb), jnp.float32)
        for h in range(N_IDX_HEADS):
            score = score + wi[:, h:h + 1] * jnp.maximum(_nt_dot(qm_ref[h], kblk), 0.0)
        bits = lax.bitcast_convert_type(score, jnp.int32)
        key = jnp.where(bits < 0, bits ^ jnp.int32(0x7FFFFFFF), bits)
        row = q_off + i * tq + lax.broadcasted_iota(jnp.int32, (tq, kb), 0)
        col = j * kb + lax.broadcasted_iota(jnp.int32, (tq, kb), 1)
        admissible = (col >> _CHUNK_SHIFT) <= (row >> _CHUNK_SHIFT)
        key = jnp.where(admissible, key, jnp.int32(INT_MIN))
        key_ref[j] = key
        half_ref[j] = (key >> 16).astype(jnp.int16)
        return 0

    lax.fori_loop(0, nb, score_block, 0)

    def count(pred):
        def body(j, acc):
            parts = []
            for rows in chunks:
                hits = [jnp.where(pred(key_ref[j, rows, g], rows, j * kb + g.start), 1.0, 0.0) for g in groups]
                parts.append(functools.reduce(jnp.add, hits))
            return acc + jnp.concatenate(parts, axis=0)
        acc = lax.fori_loop(0, nb, body, jnp.zeros((tq, LANES), jnp.float32))
        return replicate(jnp.sum(acc, axis=-1, keepdims=True))

    def count_half_at_least_cand():
        one, zero = jnp.int16(1), jnp.int16(0)

        def body(j, acc):
            parts = []
            for rows in chunks:
                cand = cand16_ref[rows, :]
                hits = [jnp.where(half_ref[j, rows, g] >= cand, one, zero) for g in groups]
                parts.append(functools.reduce(jnp.add, hits))
            return acc + jnp.concatenate(parts, axis=0)
        acc = lax.fori_loop(0, nb, body, jnp.zeros((tq, LANES), jnp.int16))
        return replicate(jnp.sum(acc.astype(jnp.float32), axis=-1, keepdims=True))

    def bisect_half():
        def test(cand):
            cand_ref[...] = cand
            cand16_ref[...] = cand.astype(jnp.int16)
            c = count_half_at_least_cand()
            take = c >= k_f
            thr_ref[...] = jnp.where(take, cand, thr_ref[...])
            cnt_ref[...] = jnp.where(take, c, cnt_ref[...])

        thr_ref[...] = jnp.full((tq, LANES), INT16_MIN, jnp.int32)
        test(jnp.zeros((tq, LANES), jnp.int32))

        def step(b, _):
            test(thr_ref[...] + (jnp.int32(1) << (jnp.int32(14) - b)))
            return 0

        lax.fori_loop(0, 15, step, 0)
        return thr_ref[...]

    cnt_ref[...] = jnp.full((tq, LANES), k_f, jnp.float32)
    lim_ref[...] = bisect_half()

    def low_halves(j, _):
        for rows in chunks:
            hi = lim_ref[rows, :].astype(jnp.int16)
            for g in groups:
                low = ((key_ref[j, rows, g] & 0xFFFF) + INT16_MIN).astype(jnp.int16)
                high = half_ref[j, rows, g]
                other = jnp.where(high > hi, jnp.int16(INT16_MAX), jnp.int16(INT16_MIN))
                half_ref[j, rows, g] = jnp.where(high == hi, low, other)
        return 0

    lax.fori_loop(0, nb, low_halves, 0)
    thr_lo = bisect_half()
    thr_ref[...] = (lim_ref[...] << 16) + (thr_lo - INT16_MIN)

    thr = thr_ref[...]
    has_thr = thr > jnp.int32(INT_MIN)
    surplus = jnp.logical_and(has_thr, cnt_ref[...] > k_f)

    def bias_without_surplus():
        cand_ref[...] = jnp.maximum(thr_ref[...], jnp.int32(INT_MIN + 1))

        def block(j, _):
            for rows in chunks:
                for g in groups:
                    bias_ref[j, rows, g] = jnp.where(key_ref[j, rows, g] >= cand_ref[rows, :], 0.0, NEG_INF)
            return 0

        lax.fori_loop(0, nb, block, 0)

    def bias_with_surplus():
        above = count(lambda key, rows, first: key > thr_ref[rows, :])
        cnt_ref[...] = jnp.where(has_thr, k_f - above, 0.0)
        r = lax.broadcasted_iota(jnp.int32, (kb, kb), 0)
        c = lax.broadcasted_iota(jnp.int32, (kb, kb), 1)
        tri = (r <= c).astype(MXU_DTYPE)

        def block(j, before):
            key = key_ref[j]
            tied = key == _lane_tile(thr_ref[...], kb)
            rank = _dot(jnp.where(tied, 1.0, 0.0).astype(MXU_DTYPE), tri)
            take = jnp.logical_and(tied, rank + _lane_tile(before, kb) <= _lane_tile(cnt_ref[...], kb))
            sel = jnp.logical_or(key > _lane_tile(thr_ref[...], kb), take)
            bias_ref[j] = jnp.where(sel, 0.0, NEG_INF)
            return before + replicate(jnp.max(rank, axis=-1, keepdims=True))

        lax.fori_loop(0, nb, block, jnp.zeros((tq, LANES), jnp.float32))

    any_surplus = jnp.max(jnp.where(surplus, 1.0, 0.0)) > 0.0
    lax.cond(any_surplus, bias_with_surplus, bias_without_surplus)

    @pl.when(i == 0)
    def _():
        _store_values_and_ones(va_ref, slice(0, LANES), vo_ref)

    _store_half_masks(qa_ref, qm_ref)
    _init_state(m_ref, l_ref, acc_ref)

    def attend(j, _):
        start = pl.multiple_of(j * kb, kb)
        for h in range(N_HEADS_A):
            s = _nt_dot(qm_ref[h], ka_ref[pl.ds(start, kb), :]) + bias_ref[j]
            _softmax_step(s, h, m_ref, l_ref, acc_ref, vo_ref[pl.ds(start, kb), :])
        return 0

    lax.fori_loop(0, nb, attend, 0)
    _write_pairs(o_ref, l_ref, acc_ref)


def _dsa(qi, wi, qa, ki, ka, va, *, batch, tq, kb, q_off, k_sel):
    n, width = qa.shape
    t_q = n // batch
    s_len = ka.shape[0] // batch
    nq = t_q // tq
    nblk = s_len // kb
    qspec = lambda wdt: pl.BlockSpec((tq, wdt), lambda b, i: (b * nq + i, 0))
    kspec = pl.BlockSpec((s_len, LANES), lambda b, i: (b, 0))
    return pl.pallas_call(
        functools.partial(_dsa_kernel, tq=tq, kb=kb, q_off=q_off, k_sel=k_sel),
        grid=(batch, nq),
        in_specs=[qspec(width), qspec(N_IDX_HEADS), qspec(width), kspec, kspec, kspec],
        out_specs=qspec(width),
        out_shape=jax.ShapeDtypeStruct((n, width), MXU_DTYPE),
        scratch_shapes=[pltpu.VMEM((nblk, tq, kb), jnp.int32),
                        pltpu.VMEM((nblk, tq, kb), jnp.int16),
                        pltpu.VMEM((nblk, tq, kb), jnp.float32),
                        pltpu.VMEM((tq, LANES), jnp.int32),
                        pltpu.VMEM((tq, LANES), jnp.int32),
                        pltpu.VMEM((tq, LANES), jnp.int16),
                        pltpu.VMEM((tq, LANES), jnp.int32),
                        pltpu.VMEM((tq, LANES), jnp.float32),
                        pltpu.VMEM((s_len, 2 * LANES), MXU_DTYPE),
                        ] + _attention_scratch(tq, N_HEADS_A),
        compiler_params=pltpu.CompilerParams(
            dimension_semantics=("arbitrary", "arbitrary"), vmem_limit_bytes=V7X_VMEM_LIMIT_BYTES),
        name="dsa_attention",
    )(qi, wi, qa, ki, ka, va)


def _post_kernel(h_ref, oa_ref, ob_ref, woa_ref, wob_ref, g2_ref, b2_ref,
                 wg_ref, wu_ref, wd_ref, g3_ref, b3_ref, o_ref, act_ref):
    mix = _dot(oa_ref[...], woa_ref[...]) + _dot(ob_ref[...], wob_ref[...])
    h2 = _layer_norm(ALPHA * h_ref[...] + mix, g2_ref[...], b2_ref[...])
    o_ref[...] = _macaron_half(h2, wg_ref, wu_ref, wd_ref, g3_ref[...], b3_ref[...], act_ref)


def _post(h, oa, ob, woa, wob, g2, b2, wg, wu, wd, g3, b3):
    n, d = h.shape
    d_ff = wg.shape[1]
    tm = _row_tile(n)
    row = lambda wdt: pl.BlockSpec((tm, wdt), lambda i: (i, 0))
    return pl.pallas_call(
        _post_kernel,
        grid=(n // tm,),
        in_specs=[row(d), row(oa.shape[1]), row(ob.shape[1]),
                  _const_spec(woa.shape), _const_spec(wob.shape), _const_spec((1, d)), _const_spec((1, d)),
                  _const_spec((d, d_ff)), _const_spec((d, d_ff)), _const_spec((d_ff, d)),
                  _const_spec((1, d)), _const_spec((1, d))],
        out_specs=row(d),
        out_shape=jax.ShapeDtypeStruct((n, d), jnp.float32),
        scratch_shapes=[pltpu.VMEM((tm, d_ff), MXU_DTYPE)],
        compiler_params=pltpu.CompilerParams(
            dimension_semantics=("arbitrary",), vmem_limit_bytes=V7X_VMEM_LIMIT_BYTES),
        name="post",
    )(h, oa, ob, woa, wob, g2, b2, wg, wu, wd, g3, b3)


def _prepare_w_in(w_in):
    d = w_in.shape[0]
    sizes = (N_HEADS_A * HEAD_DIM, N_KV_A * HEAD_DIM, N_KV_A * HEAD_DIM, N_IDX_HEADS * IDX_DIM, IDX_DIM,
             N_IDX_HEADS, N_HEADS_B * HEAD_DIM, N_HEADS_B * HEAD_DIM, N_HEADS_B * HEAD_DIM, N_HEADS_B)
    offs = [0]
    for s in sizes:
        offs.append(offs[-1] + s)
    qa, ka, va, qi, ki, wi, qb, kb, vb, fb = (w_in[:, offs[k]:offs[k + 1]] for k in range(10))
    qa = qa.reshape(d, N_HEADS_A, HEAD_DIM)[:, jnp.array(_QA_HEAD_ORDER)].reshape(d, -1)
    pad = jnp.zeros((d, LANES - N_IDX_HEADS - N_HEADS_B), w_in.dtype)
    w = jnp.concatenate([qa, ka, va, qi, ki, ki, wi, fb, pad, qb, kb, vb], axis=1)
    assert w.shape[1] == _C_END
    return w.astype(MXU_DTYPE)


def _rope_tables(pos):
    half = ROT_DIM // 2
    inv_freq = ROPE_THETA ** (-jnp.arange(half, dtype=jnp.float32) * 2.0 / ROT_DIM)
    ang = pos.astype(jnp.float32)[:, None] * inv_freq[None, :]
    cos, sin = jnp.cos(ang), jnp.sin(ang)
    ones = jnp.ones((pos.shape[0], HEAD_DIM - ROT_DIM), jnp.float32)
    cos64 = jnp.concatenate([cos, cos, ones], axis=1)
    sin64 = jnp.concatenate([-sin, sin, jnp.zeros_like(ones)], axis=1)
    return jnp.tile(cos64, (1, LANES // HEAD_DIM)), jnp.tile(sin64, (1, LANES // HEAD_DIM))


def _project_tokens(h, w_in_p, bf_p, pos, rows_per_seq):
    n = h.shape[0]
    tm = _row_tile(n)
    cos, sin = _rope_tables(pos)
    if tm <= rows_per_seq:
        assert rows_per_seq % tm == 0
        tab_blocks = rows_per_seq // tm
    else:
        assert tm % rows_per_seq == 0
        cos = jnp.tile(cos, (tm // rows_per_seq, 1))
        sin = jnp.tile(sin, (tm // rows_per_seq, 1))
        tab_blocks = 1
    return _project(h, w_in_p, bf_p, cos, sin, tm, tab_blocks)


def _blocked_cum(logf_keys, kb):
    b, s, h = logf_keys.shape
    cum = _cumsum_keys(jnp.transpose(logf_keys, (0, 2, 1)))
    return jnp.transpose(cum.reshape(b, h, s // kb, kb), (0, 2, 1, 3))


def _pad_keys(x, s_pad):
    return jnp.pad(x, ((0, 0), (0, s_pad - x.shape[1]), (0, 0)))


def kernel(x_prompt, x_sample, cache_k_a, cache_v_a, cache_kidx_a, cache_k_b, cache_v_b, cache_logf_b,
           w_in, b_f, w_out, ln1_g, ln1_b, ffn1_w_gate, ffn1_w_up, ffn1_w_down,
           ln2_g, ln2_b, ln3_g, ln3_b, ffn2_w_gate, ffn2_w_up, ffn2_w_down):
    assert w_in.shape[0] == DEPTH
    bsz, seq, d = x_prompt.shape
    dbs, dseq, _ = x_sample.shape
    past = cache_k_a.shape[2]
    bf16 = MXU_DTYPE

    w_in_p = _prepare_w_in(w_in[0])
    bf_p = jnp.zeros((1, LANES), jnp.float32).at[0, N_IDX_HEADS:N_IDX_HEADS + N_HEADS_B].set(b_f[0])
    order = jnp.array(_QA_HEAD_ORDER)
    w_out_a = w_out[0, :N_HEADS_A * HEAD_DIM].reshape(N_HEADS_A, HEAD_DIM, d)[order].reshape(-1, d).astype(bf16)
    w_out_b = w_out[0, N_HEADS_A * HEAD_DIM:].astype(bf16)
    ffn1 = (ffn1_w_gate[0].astype(bf16), ffn1_w_up[0].astype(bf16), ffn1_w_down[0].astype(bf16))
    ffn2 = (ffn2_w_gate[0].astype(bf16), ffn2_w_up[0].astype(bf16), ffn2_w_down[0].astype(bf16))
    vec = lambda a: a[0].reshape(1, d)

    def layer(x, pos, rows_per_seq, attend):
        n = x.shape[0] * x.shape[1]
        h = _ffn_ln(x.reshape(n, d), *ffn1, vec(ln1_g), vec(ln1_b))
        proj = _project_tokens(h, w_in_p, bf_p, pos, rows_per_seq)
        oa, ob = attend(proj)
        y = _post(h, oa, ob, w_out_a, w_out_b, vec(ln2_g), vec(ln2_b), *ffn2, vec(ln3_g), vec(ln3_b))
        return y.reshape(x.shape), proj

    tq_p = min(Q_TILE, seq)
    kb_p = min(KEY_BLOCK, seq)

    def attend_prompt(proj):
        qa, _, _, qi, _, wi, qb, _, _, logf, kab, vab, kib, kbb, vbb = proj
        ck = _blocked_cum(logf.reshape(bsz, seq, N_HEADS_B), kb_p)
        oa = _dsa(qi, wi, qa, kib, kab, vab, batch=bsz, tq=tq_p, kb=kb_p, q_off=0,
                  k_sel=min(TOPK_MAX, seq // 4))
        ob = _fox(qb, kbb, vbb, ck, batch=bsz, tq=tq_p, kb=kb_p, q_off=0)
        return oa, ob

    y_p, proj_p = layer(x_prompt, jnp.arange(seq, dtype=jnp.int32), seq, attend_prompt)

    total = past + dseq
    kb_s = min(SAMPLE_KEY_BLOCK, past)
    s_pad = -(-total // kb_s) * kb_s

    def attend_sample(proj):
        qa, _, _, qi, _, wi, qb, _, _, logf, kab, vab, kib, kbb, vbb = proj
        new = lambda a: a.reshape(dbs, dseq, -1)
        keys = lambda c, nw: _pad_keys(
            jnp.concatenate([c.reshape(dbs, past, -1).astype(bf16), new(nw)], axis=1), s_pad
        ).reshape(dbs * s_pad, -1)
        kidx2 = jnp.concatenate([cache_kidx_a[0], cache_kidx_a[0]], axis=-1)
        logf_all = _pad_keys(jnp.concatenate([cache_logf_b[0].astype(jnp.float32), new(logf)], axis=1), s_pad)
        ck = _blocked_cum(logf_all, kb_s)
        oa = _dsa(qi, wi, qa, keys(kidx2, kib), keys(cache_k_a[0], kab), keys(cache_v_a[0], vab),
                  batch=dbs, tq=dseq, kb=kb_s, q_off=past, k_sel=min(TOPK_MAX, total // 4))
        ob = _fox(qb, keys(cache_k_b[0], kbb), keys(cache_v_b[0], vbb), ck,
                  batch=dbs, tq=dseq, kb=kb_s, q_off=past)
        return oa, ob

    y_s, proj_s = layer(x_sample, past + jnp.arange(dseq, dtype=jnp.int32), dseq, attend_sample)

    def rows(proj, b, t):
        _, ka, va, _, kidx, _, _, kbv, vbv, logf = proj[:10]
        return (ka.reshape(1, b, t, N_KV_A, HEAD_DIM), va.reshape(1, b, t, N_KV_A, HEAD_DIM),
                kidx.reshape(1, b, t, IDX_DIM), kbv.reshape(1, b, t, N_HEADS_B, HEAD_DIM),
                vbv.reshape(1, b, t, N_HEADS_B, HEAD_DIM), logf.reshape(1, b, t, N_HEADS_B))

    return (y_p, y_s) + rows(proj_p, bsz, seq) + rows(proj_s, dbs, dseq)
```

```python
import functools
import math

import jax
import jax.numpy as jnp
from jax import lax
from jax.experimental import pallas as pl
from jax.experimental.pallas import tpu as pltpu

CHUNK = 64
_CHUNK_SHIFT = 6
HEAD_DIM = 64
N_HEADS_A = 8
N_KV_A = 2
N_IDX_HEADS = 8
IDX_DIM = 64
TOPK_MAX = 256
N_HEADS_B = 8
ROT_DIM = HEAD_DIM // 4
ROPE_THETA = 500000.0
LN_EPS = 1e-5
DEPTH = 1
ALPHA = (2.0 * DEPTH) ** 0.25

MXU_DTYPE = jnp.bfloat16

LANES = 128
V7X_VMEM_LIMIT_BYTES = 60 * 1024 * 1024

ROW_TILE = 512
FF_CHUNK = 256
Q_TILE = 512
KEY_BLOCK = 512
SAMPLE_KEY_BLOCK = 512
ROW_CHUNK = 64

INT_MIN = -(2 ** 31)
INT16_MIN, INT16_MAX = -(2 ** 15), 2 ** 15 - 1
NEG_INF = float("-inf")
F32_LOWEST = float(jnp.finfo(jnp.float32).min)
LOG2E = math.log2(math.e)

_C_QA, _C_KA, _C_VA, _C_QI, _C_KI, _C_WF, _C_QB, _C_KB, _C_VB, _C_END = (
    0, 512, 640, 768, 1280, 1408, 1536, 2048, 2560, 3072)
_QA_HEAD_ORDER = (0, 4, 1, 5, 2, 6, 3, 7)


def _nt_dot(a, b):
    return lax.dot_general(a, b, (((1,), (1,)), ((), ())), preferred_element_type=jnp.float32)


def _dot(a, b):
    return jnp.dot(a, b, preferred_element_type=jnp.float32)


def _layer_norm(x, g, b):
    mu = jnp.mean(x, axis=-1, keepdims=True)
    xc = x - mu
    var = jnp.mean(xc * xc, axis=-1, keepdims=True)
    return xc * lax.rsqrt(var + LN_EPS) * g + b


def _macaron_half(x, wg_ref, wu_ref, wd_ref, g, b, act_ref):
    xb = x.astype(MXU_DTYPE)
    d_ff = wg_ref.shape[1]
    for c in range(d_ff // FF_CHUNK):
        cols = slice(c * FF_CHUNK, (c + 1) * FF_CHUNK)
        gate = _dot(xb, wg_ref[:, cols])
        up = _dot(xb, wu_ref[:, cols])
        act_ref[:, cols] = (gate * jax.nn.sigmoid(gate) * up).astype(MXU_DTYPE)
    down = _dot(act_ref[...], wd_ref[...])
    return _layer_norm(ALPHA * x + 0.5 * down, g, b)


def _ffn_ln_kernel(x_ref, wg_ref, wu_ref, wd_ref, g_ref, b_ref, o_ref, act_ref):
    o_ref[...] = _macaron_half(x_ref[...], wg_ref, wu_ref, wd_ref, g_ref[...], b_ref[...], act_ref)


def _const_spec(shape):
    return pl.BlockSpec(shape, lambda *_: (0,) * len(shape), pipeline_mode=pl.Buffered(1))


def _row_tile(n):
    tm = ROW_TILE
    while n % tm:
        tm //= 2
    return tm


def _ffn_ln(x, wg, wu, wd, g, b):
    n, d = x.shape
    d_ff = wg.shape[1]
    tm = _row_tile(n)
    return pl.pallas_call(
        _ffn_ln_kernel,
        grid=(n // tm,),
        in_specs=[
            pl.BlockSpec((tm, d), lambda i: (i, 0)),
            _const_spec((d, d_ff)), _const_spec((d, d_ff)), _const_spec((d_ff, d)),
            _const_spec((1, d)), _const_spec((1, d)),
        ],
        out_specs=pl.BlockSpec((tm, d), lambda i: (i, 0)),
        out_shape=jax.ShapeDtypeStruct((n, d), jnp.float32),
        scratch_shapes=[pltpu.VMEM((tm, d_ff), MXU_DTYPE)],
        compiler_params=pltpu.CompilerParams(
            dimension_semantics=("arbitrary",), vmem_limit_bytes=V7X_VMEM_LIMIT_BYTES),
        name="ffn_ln",
    )(x, wg, wu, wd, g, b)


def _rope(x, cos, sin):
    lane = lax.broadcasted_iota(jnp.int32, x.shape, 1) & (HEAD_DIM - 1)
    partner = jnp.where(lane < ROT_DIM // 2,
                        pltpu.roll(x, LANES - ROT_DIM // 2, 1),
                        pltpu.roll(x, ROT_DIM // 2, 1))
    return x * cos + partner * sin


def _project_kernel(h_ref, w_ref, bf_ref, cos_ref, sin_ref,
                    qa_ref, ka_ref, va_ref, qi_ref, kidx_ref, wi_ref, qb_ref, kb_ref, vb_ref, logf_ref,
                    kab_ref, vab_ref, kib_ref, kbb_ref, vbb_ref):
    hb = h_ref[...].astype(MXU_DTYPE)
    cos = cos_ref[...]
    sin = sin_ref[...]
    q_scale = HEAD_DIM ** -0.5 * LOG2E

    def part(lo, hi):
        return _dot(hb, w_ref[:, lo:hi])

    def rope_groups(z):
        return jnp.concatenate(
            [_rope(z[:, g * LANES:(g + 1) * LANES], cos, sin) for g in range(z.shape[1] // LANES)], axis=1)

    qa_ref[...] = (rope_groups(part(_C_QA, _C_KA)) * q_scale).astype(MXU_DTYPE)
    def heads(z):
        return z.reshape(z.shape[0], z.shape[1] // HEAD_DIM, HEAD_DIM)

    ka = _rope(part(_C_KA, _C_VA), cos, sin)
    ka_ref[...] = heads(ka)
    kab_ref[...] = ka.astype(MXU_DTYPE)
    va = part(_C_VA, _C_QI)
    va_ref[...] = heads(va)
    vab_ref[...] = va.astype(MXU_DTYPE)
    qi_ref[...] = (rope_groups(part(_C_QI, _C_KI)) * (IDX_DIM ** -0.5)).astype(MXU_DTYPE)
    ki = _rope(part(_C_KI, _C_WF), cos, sin)
    kidx_ref[...] = ki[:, :IDX_DIM]
    kib_ref[...] = ki.astype(MXU_DTYPE)
    wf = part(_C_WF, _C_QB)
    wi_ref[...] = wf[:, :N_IDX_HEADS] * (N_IDX_HEADS ** -0.5)
    f = wf + bf_ref[...]
    logf = -(jnp.maximum(-f, 0.0) + jnp.log1p(jnp.exp(-jnp.abs(f))))
    logf_ref[...] = pltpu.roll(logf, LANES - N_IDX_HEADS, 1)[:, :N_HEADS_B]
    qb_ref[...] = (part(_C_QB, _C_KB) * q_scale).astype(MXU_DTYPE)
    kb = part(_C_KB, _C_VB)
    kb_ref[...] = heads(kb)
    kbb_ref[...] = kb.astype(MXU_DTYPE)
    vb = part(_C_VB, _C_END)
    vb_ref[...] = heads(vb)
    vbb_ref[...] = vb.astype(MXU_DTYPE)


def _project(h, w, bf, cos, sin, tm, tab_blocks):
    n, d = h.shape
    f32, bf16 = jnp.float32, MXU_DTYPE
    kv_a, kv_b = (N_KV_A, HEAD_DIM), (N_HEADS_B, HEAD_DIM)
    tails = [((512,), bf16), (kv_a, f32), (kv_a, f32), ((512,), bf16), ((IDX_DIM,), f32), ((N_IDX_HEADS,), f32),
             ((512,), bf16), (kv_b, f32), (kv_b, f32), ((N_HEADS_B,), f32),
             ((128,), bf16), ((128,), bf16), ((128,), bf16), ((512,), bf16), ((512,), bf16)]
    row = lambda *tail: pl.BlockSpec((tm,) + tail, lambda i: (i,) + (0,) * len(tail))
    tab = pl.BlockSpec((tm, LANES), lambda i: (i % tab_blocks, 0))
    return pl.pallas_call(
        _project_kernel,
        grid=(n // tm,),
        in_specs=[row(d), _const_spec(w.shape), _const_spec((1, LANES)), tab, tab],
        out_specs=[row(*tail) for tail, _ in tails],
        out_shape=[jax.ShapeDtypeStruct((n,) + tail, dt) for tail, dt in tails],
        compiler_params=pltpu.CompilerParams(
            dimension_semantics=("arbitrary",), vmem_limit_bytes=V7X_VMEM_LIMIT_BYTES),
        name="project",
    )(h, w, bf, cos, sin)


def _cumsum_kernel(x_ref, o_ref, *, seg):
    r = lax.broadcasted_iota(jnp.int32, (seg, seg), 0)
    c = lax.broadcasted_iota(jnp.int32, (seg, seg), 1)
    tri = (r <= c).astype(jnp.float32)
    carry = jnp.zeros((x_ref.shape[1], 1), jnp.float32)
    for s in range(x_ref.shape[2] // seg):
        cols = slice(s * seg, (s + 1) * seg)
        part = jnp.dot(x_ref[0, :, cols], tri, preferred_element_type=jnp.float32,
                       precision=lax.Precision.HIGHEST) + carry
        o_ref[0, :, cols] = part * LOG2E
        carry = part[:, seg - 1:seg]


def _cumsum_keys(x):
    b, h, s = x.shape
    seg = 2 * LANES if s % (2 * LANES) == 0 else LANES
    assert s % seg == 0
    return pl.pallas_call(
        functools.partial(_cumsum_kernel, seg=seg),
        grid=(b,),
        in_specs=[pl.BlockSpec((1, h, s), lambda i: (i, 0, 0))],
        out_specs=pl.BlockSpec((1, h, s), lambda i: (i, 0, 0)),
        out_shape=jax.ShapeDtypeStruct((b, h, s), jnp.float32),
        compiler_params=pltpu.CompilerParams(dimension_semantics=("arbitrary",)),
        name="cumsum_keys",
    )(x)


def _num_key_blocks(i, tq, kb, q_off):
    return (q_off + (i + 1) * tq + kb - 1) // kb


def _lane_groups(x):
    return [x[:, g * LANES:(g + 1) * LANES] for g in range(x.shape[1] // LANES)]


def _lane_tile(x, width):
    return jnp.concatenate([x] * (width // LANES), axis=1)


def _softmax_step(s, h, m_ref, l_ref, acc_ref, v_ones):
    groups = _lane_groups(s)
    smax = functools.reduce(jnp.maximum, groups)
    m_old = m_ref[h]
    m_new = jnp.maximum(m_old, jnp.max(smax, axis=-1, keepdims=True))
    m_safe = jnp.maximum(m_new, F32_LOWEST)
    corr = jnp.exp2(m_old - m_safe)
    p = jnp.concatenate([jnp.exp2(g - m_safe) for g in groups], axis=1).astype(MXU_DTYPE)
    pv = _dot(p, v_ones)
    acc_ref[h] = corr * acc_ref[h] + pv[:, :LANES]
    l_ref[h] = corr * l_ref[h] + pv[:, LANES:]
    m_ref[h] = m_new


def _store_values_and_ones(v_ref, lanes, vo_ref):
    vo_ref[:, :LANES] = v_ref[:, lanes]
    vo_ref[:, LANES:] = jnp.ones((vo_ref.shape[0], LANES), vo_ref.dtype)


def _store_half_masks(q_ref, qm_ref):
    for p in range(q_ref.shape[1] // LANES):
        q2 = q_ref[:, p * LANES:(p + 1) * LANES]
        lane = lax.broadcasted_iota(jnp.int32, q2.shape, 1)
        zero = jnp.zeros_like(q2)
        qm_ref[2 * p] = jnp.where(lane < HEAD_DIM, q2, zero)
        qm_ref[2 * p + 1] = jnp.where(lane >= HEAD_DIM, q2, zero)


def _init_state(m_ref, l_ref, acc_ref):
    m_ref[...] = jnp.full(m_ref.shape, NEG_INF, jnp.float32)
    l_ref[...] = jnp.zeros(l_ref.shape, jnp.float32)
    acc_ref[...] = jnp.zeros(acc_ref.shape, jnp.float32)


def _write_pairs(o_ref, l_ref, acc_ref):
    for p in range(o_ref.shape[1] // LANES):
        outs = [acc_ref[h] / l_ref[h] for h in (2 * p, 2 * p + 1)]
        lane = lax.broadcasted_iota(jnp.int32, outs[0].shape, 1)
        o_ref[:, p * LANES:(p + 1) * LANES] = jnp.where(lane < HEAD_DIM, outs[0], outs[1]).astype(o_ref.dtype)


def _attention_scratch(tq, n_heads):
    return [pltpu.VMEM((n_heads, tq, LANES), MXU_DTYPE),
            pltpu.VMEM((n_heads, tq, LANES), jnp.float32),
            pltpu.VMEM((n_heads, tq, LANES), jnp.float32),
            pltpu.VMEM((n_heads, tq, LANES), jnp.float32)]


def _fox_kernel(q_ref, k_ref, v_ref, ck_ref, o_ref, vo_ref, qm_ref, m_ref, l_ref, acc_ref, *, tq, kb, q_off):
    i = pl.program_id(1)
    nb = _num_key_blocks(i, tq, kb, q_off)
    assert kb % tq == 0 and q_off % tq == 0

    @pl.when(i == 0)
    def _():
        for p in range(N_HEADS_B // 2):
            _store_values_and_ones(v_ref, slice(p * LANES, (p + 1) * LANES), vo_ref.at[p])

    _store_half_masks(q_ref, qm_ref)
    _init_state(m_ref, l_ref, acc_ref)

    def block(j, masked):
        start = pl.multiple_of(j * kb, kb)
        ck = ck_ref[0, j]
        if masked:
            row = q_off + i * tq + lax.broadcasted_iota(jnp.int32, (tq, kb), 0)
            col = j * kb + lax.broadcasted_iota(jnp.int32, (tq, kb), 1)
            causal = col <= row
        for h in range(N_HEADS_B):
            lanes = slice((h // 2) * LANES, (h // 2 + 1) * LANES)
            s = _nt_dot(qm_ref[h], k_ref[pl.ds(start, kb), lanes]) - ck[h:h + 1, :]
            if masked:
                s = jnp.where(causal, s, NEG_INF)
            _softmax_step(s, h, m_ref, l_ref, acc_ref, vo_ref[h // 2, pl.ds(start, kb), :])

    def full_block(j, _):
        block(j, masked=False)
        return 0

    lax.fori_loop(0, nb - 1, full_block, 0)
    block(nb - 1, masked=True)
    _write_pairs(o_ref, l_ref, acc_ref)


def _fox(q, k, v, ck, *, batch, tq, kb, q_off):
    n, width = q.shape
    t_q = n // batch
    s_len = k.shape[0] // batch
    nq = t_q // tq
    return pl.pallas_call(
        functools.partial(_fox_kernel, tq=tq, kb=kb, q_off=q_off),
        grid=(batch, nq),
        in_specs=[
            pl.BlockSpec((tq, width), lambda b, i: (b * nq + i, 0)),
            pl.BlockSpec((s_len, width), lambda b, i: (b, 0)),
            pl.BlockSpec((s_len, width), lambda b, i: (b, 0)),
            pl.BlockSpec((1,) + ck.shape[1:], lambda b, i: (b, 0, 0, 0)),
        ],
        out_specs=pl.BlockSpec((tq, width), lambda b, i: (b * nq + i, 0)),
        out_shape=jax.ShapeDtypeStruct((n, width), MXU_DTYPE),
        scratch_shapes=[pltpu.VMEM((N_HEADS_B // 2, s_len, 2 * LANES), MXU_DTYPE)]
        + _attention_scratch(tq, N_HEADS_B),
        compiler_params=pltpu.CompilerParams(
            dimension_semantics=("arbitrary", "arbitrary"), vmem_limit_bytes=V7X_VMEM_LIMIT_BYTES),
        name="fox_attention",
    )(q, k, v, ck)


def _dsa_kernel(qi_ref, wi_ref, qa_ref, ki_ref, ka_ref, va_ref, o_ref,
                key_ref, half_ref, bias_ref, thr_ref, cand_ref, cand16_ref, lim_ref, cnt_ref, vo_ref,
                qm_ref, m_ref, l_ref, acc_ref,
                *, tq, kb, q_off, k_sel):
    i = pl.program_id(1)
    nb = _num_key_blocks(i, tq, kb, q_off)
    k_f = jnp.float32(k_sel)
    rc = min(ROW_CHUNK, tq)
    chunks = [slice(r * rc, (r + 1) * rc) for r in range(tq // rc)]
    groups = [slice(g * LANES, (g + 1) * LANES) for g in range(kb // LANES)]

    def replicate(x):
        return jnp.broadcast_to(x, (tq, LANES))

    _store_half_masks(qi_ref, qm_ref)
    wi = wi_ref[...]

    def score_block(j, _):
        start = pl.multiple_of(j * kb, kb)
        kblk = ki_ref[pl.ds(start, kb), :]
        score = jnp.zeros((tq, kb), jnp.float32)
        for h in range(N_IDX_HEADS):
            score = score + wi[:, h:h + 1] * jnp.maximum(_nt_dot(qm_ref[h], kblk), 0.0)
        bits = lax.bitcast_convert_type(score, jnp.int32)
        key = jnp.where(bits < 0, bits ^ jnp.int32(0x7FFFFFFF), bits)
        row = q_off + i * tq + lax.broadcasted_iota(jnp.int32, (tq, kb), 0)
        col = j * kb + lax.broadcasted_iota(jnp.int32, (tq, kb), 1)
        admissible = (col >> _CHUNK_SHIFT) <= (row >> _CHUNK_SHIFT)
        key = jnp.where(admissible, key, jnp.int32(INT_MIN))
        key_ref[j] = key
        half_ref[j] = (key >> 16).astype(jnp.int16)
        return 0

    lax.fori_loop(0, nb, score_block, 0)

    def count(pred):
        def body(j, acc):
            parts = []
            for rows in chunks:
                hits = [jnp.where(pred(key_ref[j, rows, g], rows, j * kb + g.start), 1.0, 0.0) for g in groups]
                parts.append(functools.reduce(jnp.add, hits))
            return acc + jnp.concatenate(parts, axis=0)
        acc = lax.fori_loop(0, nb, body, jnp.zeros((tq, LANES), jnp.float32))
        return replicate(jnp.sum(acc, axis=-1, keepdims=True))

    def count_half_at_least_cand():
        one, zero = jnp.int16(1), jnp.int16(0)

        def body(j, acc):
            parts = []
            for rows in chunks:
                cand = cand16_ref[rows, :]
                hits = [jnp.where(half_ref[j, rows, g] >= cand, one, zero) for g in groups]
                parts.append(functools.reduce(jnp.add, hits))
            return acc + jnp.concatenate(parts, axis=0)
        acc = lax.fori_loop(0, nb, body, jnp.zeros((tq, LANES), jnp.int16))
        return replicate(jnp.sum(acc.astype(jnp.float32), axis=-1, keepdims=True))

    def bisect_half():
        def test(cand):
            cand_ref[...] = cand
            cand16_ref[...] = cand.astype(jnp.int16)
            c = count_half_at_least_cand()
            take = c >= k_f
            thr_ref[...] = jnp.where(take, cand, thr_ref[...])
            cnt_ref[...] = jnp.where(take, c, cnt_ref[...])

        thr_ref[...] = jnp.full((tq, LANES), INT16_MIN, jnp.int32)
        test(jnp.zeros((tq, LANES), jnp.int32))

        def step(b, _):
            test(thr_ref[...] + (jnp.int32(1) << (jnp.int32(14) - b)))
            return 0

        lax.fori_loop(0, 15, step, 0)
        return thr_ref[...]

    cnt_ref[...] = jnp.full((tq, LANES), k_f, jnp.float32)
    lim_ref[...] = bisect_half()

    def low_halves(j, _):
        for rows in chunks:
            hi = lim_ref[rows, :].astype(jnp.int16)
            for g in groups:
                low = ((key_ref[j, rows, g] & 0xFFFF) + INT16_MIN).astype(jnp.int16)
                high = half_ref[j, rows, g]
                other = jnp.where(high > hi, jnp.int16(INT16_MAX), jnp.int16(INT16_MIN))
                half_ref[j, rows, g] = jnp.where(high == hi, low, other)
        return 0

    lax.fori_loop(0, nb, low_halves, 0)
    thr_lo = bisect_half()
    thr_ref[...] = (lim_ref[...] << 16) + (thr_lo - INT16_MIN)

    thr = thr_ref[...]
    has_thr = thr > jnp.int32(INT_MIN)
    surplus = jnp.logical_and(has_thr, cnt_ref[...] > k_f)

    def bias_without_surplus():
        cand_ref[...] = jnp.maximum(thr_ref[...], jnp.int32(INT_MIN + 1))

        def block(j, _):
            for rows in chunks:
                for g in groups:
                    bias_ref[j, rows, g] = jnp.where(key_ref[j, rows, g] >= cand_ref[rows, :], 0.0, NEG_INF)
            return 0

        lax.fori_loop(0, nb, block, 0)

    def bias_with_surplus():
        above = count(lambda key, rows, first: key > thr_ref[rows, :])
        cnt_ref[...] = jnp.where(has_thr, k_f - above, 0.0)
        r = lax.broadcasted_iota(jnp.int32, (kb, kb), 0)
        c = lax.broadcasted_iota(jnp.int32, (kb, kb), 1)
        tri = (r <= c).astype(MXU_DTYPE)

        def block(j, before):
            key = key_ref[j]
            tied = key == _lane_tile(thr_ref[...], kb)
            rank = _dot(jnp.where(tied, 1.0, 0.0).astype(MXU_DTYPE), tri)
            take = jnp.logical_and(tied, rank + _lane_tile(before, kb) <= _lane_tile(cnt_ref[...], kb))
            sel = jnp.logical_or(key > _lane_tile(thr_ref[...], kb), take)
            bias_ref[j] = jnp.where(sel, 0.0, NEG_INF)
            return before + replicate(jnp.max(rank, axis=-1, keepdims=True))

        lax.fori_loop(0, nb, block, jnp.zeros((tq, LANES), jnp.float32))

    any_surplus = jnp.max(jnp.where(surplus, 1.0, 0.0)) > 0.0
    lax.cond(any_surplus, bias_with_surplus, bias_without_surplus)

    @pl.when(i == 0)
    def _():
        _store_values_and_ones(va_ref, slice(0, LANES), vo_ref)

    _store_half_masks(qa_ref, qm_ref)
    _init_state(m_ref, l_ref, acc_ref)

    def attend(j, _):
        start = pl.multiple_of(j * kb, kb)
        for h in range(N_HEADS_A):
            s = _nt_dot(qm_ref[h], ka_ref[pl.ds(start, kb), :]) + bias_ref[j]
            _softmax_step(s, h, m_ref, l_ref, acc_ref, vo_ref[pl.ds(start, kb), :])
        return 0

    lax.fori_loop(0, nb, attend, 0)
    _write_pairs(o_ref, l_ref, acc_ref)


def _dsa(qi, wi, qa, ki, ka, va, *, batch, tq, kb, q_off, k_sel):
    n, width = qa.shape
    t_q = n // batch
    s_len = ka.shape[0] // batch
    nq = t_q // tq
    nblk = s_len // kb
    qspec = lambda wdt: pl.BlockSpec((tq, wdt), lambda b, i: (b * nq + i, 0))
    kspec = pl.BlockSpec((s_len, LANES), lambda b, i: (b, 0))
    return pl.pallas_call(
        functools.partial(_dsa_kernel, tq=tq, kb=kb, q_off=q_off, k_sel=k_sel),
        grid=(batch, nq),
        in_specs=[qspec(width), qspec(N_IDX_HEADS), qspec(width), kspec, kspec, kspec],
        out_specs=qspec(width),
        out_shape=jax.ShapeDtypeStruct((n, width), MXU_DTYPE),
        scratch_shapes=[pltpu.VMEM((nblk, tq, kb), jnp.int32),
                        pltpu.VMEM((nblk, tq, kb), jnp.int16),
                        pltpu.VMEM((nblk, tq, kb), jnp.float32),
                        pltpu.VMEM((tq, LANES), jnp.int32),
                        pltpu.VMEM((tq, LANES), jnp.int32),
                        pltpu.VMEM((tq, LANES), jnp.int16),
                        pltpu.VMEM((tq, LANES), jnp.int32),
                        pltpu.VMEM((tq, LANES), jnp.float32),
                        pltpu.VMEM((s_len, 2 * LANES), MXU_DTYPE),
                        ] + _attention_scratch(tq, N_HEADS_A),
        compiler_params=pltpu.CompilerParams(
            dimension_semantics=("arbitrary", "arbitrary"), vmem_limit_bytes=V7X_VMEM_LIMIT_BYTES),
        name="dsa_attention",
    )(qi, wi, qa, ki, ka, va)


def _post_kernel(h_ref, oa_ref, ob_ref, woa_ref, wob_ref, g2_ref, b2_ref,
                 wg_ref, wu_ref, wd_ref, g3_ref, b3_ref, o_ref, act_ref):
    mix = _dot(oa_ref[...], woa_ref[...]) + _dot(ob_ref[...], wob_ref[...])
    h2 = _layer_norm(ALPHA * h_ref[...] + mix, g2_ref[...], b2_ref[...])
    o_ref[...] = _macaron_half(h2, wg_ref, wu_ref, wd_ref, g3_ref[...], b3_ref[...], act_ref)


def _post(h, oa, ob, woa, wob, g2, b2, wg, wu, wd, g3, b3):
    n, d = h.shape
    d_ff = wg.shape[1]
    tm = _row_tile(n)
    row = lambda wdt: pl.BlockSpec((tm, wdt), lambda i: (i, 0))
    return pl.pallas_call(
        _post_kernel,
        grid=(n // tm,),
        in_specs=[row(d), row(oa.shape[1]), row(ob.shape[1]),
                  _const_spec(woa.shape), _const_spec(wob.shape), _const_spec((1, d)), _const_spec((1, d)),
                  _const_spec((d, d_ff)), _const_spec((d, d_ff)), _const_spec((d_ff, d)),
                  _const_spec((1, d)), _const_spec((1, d))],
        out_specs=row(d),
        out_shape=jax.ShapeDtypeStruct((n, d), jnp.float32),
        scratch_shapes=[pltpu.VMEM((tm, d_ff), MXU_DTYPE)],
        compiler_params=pltpu.CompilerParams(
            dimension_semantics=("arbitrary",), vmem_limit_bytes=V7X_VMEM_LIMIT_BYTES),
        name="post",
    )(h, oa, ob, woa, wob, g2, b2, wg, wu, wd, g3, b3)


def _prepare_w_in(w_in):
    d = w_in.shape[0]
    sizes = (N_HEADS_A * HEAD_DIM, N_KV_A * HEAD_DIM, N_KV_A * HEAD_DIM, N_IDX_HEADS * IDX_DIM, IDX_DIM,
             N_IDX_HEADS, N_HEADS_B * HEAD_DIM, N_HEADS_B * HEAD_DIM, N_HEADS_B * HEAD_DIM, N_HEADS_B)
    offs = [0]
    for s in sizes:
        offs.append(offs[-1] + s)
    qa, ka, va, qi, ki, wi, qb, kb, vb, fb = (w_in[:, offs[k]:offs[k + 1]] for k in range(10))
    qa = qa.reshape(d, N_HEADS_A, HEAD_DIM)[:, jnp.array(_QA_HEAD_ORDER)].reshape(d, -1)
    pad = jnp.zeros((d, LANES - N_IDX_HEADS - N_HEADS_B), w_in.dtype)
    w = jnp.concatenate([qa, ka, va, qi, ki, ki, wi, fb, pad, qb, kb, vb], axis=1)
    assert w.shape[1] == _C_END
    return w.astype(MXU_DTYPE)


def _rope_tables(pos):
    half = ROT_DIM // 2
    inv_freq = ROPE_THETA ** (-jnp.arange(half, dtype=jnp.float32) * 2.0 / ROT_DIM)
    ang = pos.astype(jnp.float32)[:, None] * inv_freq[None, :]
    cos, sin = jnp.cos(ang), jnp.sin(ang)
    ones = jnp.ones((pos.shape[0], HEAD_DIM - ROT_DIM), jnp.float32)
    cos64 = jnp.concatenate([cos, cos, ones], axis=1)
    sin64 = jnp.concatenate([-sin, sin, jnp.zeros_like(ones)], axis=1)
    return jnp.tile(cos64, (1, LANES // HEAD_DIM)), jnp.tile(sin64, (1, LANES // HEAD_DIM))


def _project_tokens(h, w_in_p, bf_p, pos, rows_per_seq):
    n = h.shape[0]
    tm = _row_tile(n)
    cos, sin = _rope_tables(pos)
    if tm <= rows_per_seq:
        assert rows_per_seq % tm == 0
        tab_blocks = rows_per_seq // tm
    else:
        assert tm % rows_per_seq == 0
        cos = jnp.tile(cos, (tm // rows_per_seq, 1))
        sin = jnp.tile(sin, (tm // rows_per_seq, 1))
        tab_blocks = 1
    return _project(h, w_in_p, bf_p, cos, sin, tm, tab_blocks)


def _blocked_cum(logf_keys, kb):
    b, s, h = logf_keys.shape
    cum = _cumsum_keys(jnp.transpose(logf_keys, (0, 2, 1)))
    return jnp.transpose(cum.reshape(b, h, s // kb, kb), (0, 2, 1, 3))


def _pad_keys(x, s_pad):
    return jnp.pad(x, ((0, 0), (0, s_pad - x.shape[1]), (0, 0)))


def kernel(x_prompt, x_sample, cache_k_a, cache_v_a, cache_kidx_a, cache_k_b, cache_v_b, cache_logf_b,
           w_in, b_f, w_out, ln1_g, ln1_b, ffn1_w_gate, ffn1_w_up, ffn1_w_down,
           ln2_g, ln2_b, ln3_g, ln3_b, ffn2_w_gate, ffn2_w_up, ffn2_w_down):
    assert w_in.shape[0] == DEPTH
    bsz, seq, d = x_prompt.shape
    dbs, dseq, _ = x_sample.shape
    past = cache_k_a.shape[2]
    bf16 = MXU_DTYPE

    w_in_p = _prepare_w_in(w_in[0])
    bf_p = jnp.zeros((1, LANES), jnp.float32).at[0, N_IDX_HEADS:N_IDX_HEADS + N_HEADS_B].set(b_f[0])
    order = jnp.array(_QA_HEAD_ORDER)
    w_out_a = w_out[0, :N_HEADS_A * HEAD_DIM].reshape(N_HEADS_A, HEAD_DIM, d)[order].reshape(-1, d).astype(bf16)
    w_out_b = w_out[0, N_HEADS_A * HEAD_DIM:].astype(bf16)
    ffn1 = (ffn1_w_gate[0].astype(bf16), ffn1_w_up[0].astype(bf16), ffn1_w_down[0].astype(bf16))
    ffn2 = (ffn2_w_gate[0].astype(bf16), ffn2_w_up[0].astype(bf16), ffn2_w_down[0].astype(bf16))
    vec = lambda a: a[0].reshape(1, d)

    def layer(x, pos, rows_per_seq, attend):
        n = x.shape[0] * x.shape[1]
        h = _ffn_ln(x.reshape(n, d), *ffn1, vec(ln1_g), vec(ln1_b))
        proj = _project_tokens(h, w_in_p, bf_p, pos, rows_per_seq)
        oa, ob = attend(proj)
        y = _post(h, oa, ob, w_out_a, w_out_b, vec(ln2_g), vec(ln2_b), *ffn2, vec(ln3_g), vec(ln3_b))
        return y.reshape(x.shape), proj

    tq_p = min(Q_TILE, seq)
    kb_p = min(KEY_BLOCK, seq)

    def attend_prompt(proj):
        qa, _, _, qi, _, wi, qb, _, _, logf, kab, vab, kib, kbb, vbb = proj
        ck = _blocked_cum(logf.reshape(bsz, seq, N_HEADS_B), kb_p)
        oa = _dsa(qi, wi, qa, kib, kab, vab, batch=bsz, tq=tq_p, kb=kb_p, q_off=0,
                  k_sel=min(TOPK_MAX, seq // 4))
        ob = _fox(qb, kbb, vbb, ck, batch=bsz, tq=tq_p, kb=kb_p, q_off=0)
        return oa, ob

    y_p, proj_p = layer(x_prompt, jnp.arange(seq, dtype=jnp.int32), seq, attend_prompt)

    total = past + dseq
    kb_s = min(SAMPLE_KEY_BLOCK, past)
    s_pad = -(-total // kb_s) * kb_s

    def attend_sample(proj):
        qa, _, _, qi, _, wi, qb, _, _, logf, kab, vab, kib, kbb, vbb = proj
        new = lambda a: a.reshape(dbs, dseq, -1)
        keys = lambda c, nw: lax.dynamic_update_slice(
            _pad_keys(c.reshape(dbs, past, -1).astype(bf16), s_pad), new(nw), (0, past, 0)
        ).reshape(dbs * s_pad, -1)
        kidx2 = jnp.concatenate([cache_kidx_a[0], cache_kidx_a[0]], axis=-1)
        logf_all = _pad_keys(jnp.concatenate([cache_logf_b[0].astype(jnp.float32), new(logf)], axis=1), s_pad)
        ck = _blocked_cum(logf_all, kb_s)
        oa = _dsa(qi, wi, qa, keys(kidx2, kib), keys(cache_k_a[0], kab), keys(cache_v_a[0], vab),
                  batch=dbs, tq=dseq, kb=kb_s, q_off=past, k_sel=min(TOPK_MAX, total // 4))
        ob = _fox(qb, keys(cache_k_b[0], kbb), keys(cache_v_b[0], vbb), ck,
                  batch=dbs, tq=dseq, kb=kb_s, q_off=past)
        return oa, ob

    y_s, proj_s = layer(x_sample, past + jnp.arange(dseq, dtype=jnp.int32), dseq, attend_sample)

    def rows(proj, b, t):
        _, ka, va, _, kidx, _, _, kbv, vbv, logf = proj[:10]
        return (ka.reshape(1, b, t, N_KV_A, HEAD_DIM), va.reshape(1, b, t, N_KV_A, HEAD_DIM),
                kidx.reshape(1, b, t, IDX_DIM), kbv.reshape(1, b, t, N_HEADS_B, HEAD_DIM),
                vbv.reshape(1, b, t, N_HEADS_B, HEAD_DIM), logf.reshape(1, b, t, N_HEADS_B))

    return (y_p, y_s) + rows(proj_p, bsz, seq) + rows(proj_s, dbs, dseq)
```

```python
import functools
import math

import jax
import jax.numpy as jnp
from jax import lax
from jax.experimental import pallas as pl
from jax.experimental.pallas import tpu as pltpu

CHUNK = 64
_CHUNK_SHIFT = 6
HEAD_DIM = 64
N_HEADS_A = 8
N_KV_A = 2
N_IDX_HEADS = 8
IDX_DIM = 64
TOPK_MAX = 256
N_HEADS_B = 8
ROT_DIM = HEAD_DIM // 4
ROPE_THETA = 500000.0
LN_EPS = 1e-5
DEPTH = 1
ALPHA = (2.0 * DEPTH) ** 0.25

MXU_DTYPE = jnp.bfloat16

LANES = 128
V7X_VMEM_LIMIT_BYTES = 60 * 1024 * 1024

ROW_TILE = 512
FF_CHUNK = 256
Q_TILE = 512
KEY_BLOCK = 512
SAMPLE_KEY_BLOCK = 512
ROW_CHUNK = 64

INT_MIN = -(2 ** 31)
INT16_MIN, INT16_MAX = -(2 ** 15), 2 ** 15 - 1
NEG_INF = float("-inf")
F32_LOWEST = float(jnp.finfo(jnp.float32).min)
LOG2E = math.log2(math.e)

_C_QA, _C_KA, _C_VA, _C_QI, _C_KI, _C_WF, _C_QB, _C_KB, _C_VB, _C_END = (
    0, 512, 640, 768, 1280, 1408, 1536, 2048, 2560, 3072)
_QA_HEAD_ORDER = (0, 4, 1, 5, 2, 6, 3, 7)


def _nt_dot(a, b):
    return lax.dot_general(a, b, (((1,), (1,)), ((), ())), preferred_element_type=jnp.float32)


def _dot(a, b):
    return jnp.dot(a, b, preferred_element_type=jnp.float32)


def _layer_norm(x, g, b):
    mu = jnp.mean(x, axis=-1, keepdims=True)
    xc = x - mu
    var = jnp.mean(xc * xc, axis=-1, keepdims=True)
    return xc * lax.rsqrt(var + LN_EPS) * g + b


def _macaron_half(x, wg_ref, wu_ref, wd_ref, g, b, act_ref):
    xb = x.astype(MXU_DTYPE)
    d_ff = wg_ref.shape[1]
    for c in range(d_ff // FF_CHUNK):
        cols = slice(c * FF_CHUNK, (c + 1) * FF_CHUNK)
        gate = _dot(xb, wg_ref[:, cols])
        up = _dot(xb, wu_ref[:, cols])
        act_ref[:, cols] = (gate * jax.nn.sigmoid(gate) * up).astype(MXU_DTYPE)
    down = _dot(act_ref[...], wd_ref[...])
    return _layer_norm(ALPHA * x + 0.5 * down, g, b)


def _ffn_ln_kernel(x_ref, wg_ref, wu_ref, wd_ref, g_ref, b_ref, o_ref, act_ref):
    o_ref[...] = _macaron_half(x_ref[...], wg_ref, wu_ref, wd_ref, g_ref[...], b_ref[...], act_ref)


def _const_spec(shape):
    return pl.BlockSpec(shape, lambda *_: (0,) * len(shape), pipeline_mode=pl.Buffered(1))


def _row_tile(n):
    tm = ROW_TILE
    while n % tm:
        tm //= 2
    return tm


def _ffn_ln(x, wg, wu, wd, g, b):
    n, d = x.shape
    d_ff = wg.shape[1]
    tm = _row_tile(n)
    return pl.pallas_call(
        _ffn_ln_kernel,
        grid=(n // tm,),
        in_specs=[
            pl.BlockSpec((tm, d), lambda i: (i, 0)),
            _const_spec((d, d_ff)), _const_spec((d, d_ff)), _const_spec((d_ff, d)),
            _const_spec((1, d)), _const_spec((1, d)),
        ],
        out_specs=pl.BlockSpec((tm, d), lambda i: (i, 0)),
        out_shape=jax.ShapeDtypeStruct((n, d), jnp.float32),
        scratch_shapes=[pltpu.VMEM((tm, d_ff), MXU_DTYPE)],
        compiler_params=pltpu.CompilerParams(
            dimension_semantics=("arbitrary",), vmem_limit_bytes=V7X_VMEM_LIMIT_BYTES),
        name="ffn_ln",
    )(x, wg, wu, wd, g, b)


def _rope(x, cos, sin):
    lane = lax.broadcasted_iota(jnp.int32, x.shape, 1) & (HEAD_DIM - 1)
    partner = jnp.where(lane < ROT_DIM // 2,
                        pltpu.roll(x, LANES - ROT_DIM // 2, 1),
                        pltpu.roll(x, ROT_DIM // 2, 1))
    return x * cos + partner * sin


def _project_kernel(h_ref, w_ref, bf_ref, cos_ref, sin_ref,
                    qa_ref, ka_ref, va_ref, qi_ref, kidx_ref, wi_ref, qb_ref, kb_ref, vb_ref, logf_ref,
                    kab_ref, vab_ref, kib_ref, kbb_ref, vbb_ref):
    hb = h_ref[...].astype(MXU_DTYPE)
    cos = cos_ref[...]
    sin = sin_ref[...]
    q_scale = HEAD_DIM ** -0.5 * LOG2E

    def part(lo, hi):
        return _dot(hb, w_ref[:, lo:hi])

    def rope_groups(z):
        return jnp.concatenate(
            [_rope(z[:, g * LANES:(g + 1) * LANES], cos, sin) for g in range(z.shape[1] // LANES)], axis=1)

    qa_ref[...] = (rope_groups(part(_C_QA, _C_KA)) * q_scale).astype(MXU_DTYPE)
    def heads(z):
        return z.reshape(z.shape[0], z.shape[1] // HEAD_DIM, HEAD_DIM)

    ka = _rope(part(_C_KA, _C_VA), cos, sin)
    ka_ref[...] = heads(ka)
    kab_ref[...] = ka.astype(MXU_DTYPE)
    va = part(_C_VA, _C_QI)
    va_ref[...] = heads(va)
    vab_ref[...] = va.astype(MXU_DTYPE)
    qi_ref[...] = (rope_groups(part(_C_QI, _C_KI)) * (IDX_DIM ** -0.5)).astype(MXU_DTYPE)
    ki = _rope(part(_C_KI, _C_WF), cos, sin)
    kidx_ref[...] = ki[:, :IDX_DIM]
    kib_ref[...] = ki.astype(MXU_DTYPE)
    wf = part(_C_WF, _C_QB)
    wi_ref[...] = wf[:, :N_IDX_HEADS] * (N_IDX_HEADS ** -0.5)
    f = wf + bf_ref[...]
    logf = -(jnp.maximum(-f, 0.0) + jnp.log1p(jnp.exp(-jnp.abs(f))))
    logf_ref[...] = pltpu.roll(logf, LANES - N_IDX_HEADS, 1)[:, :N_HEADS_B]
    qb_ref[...] = (part(_C_QB, _C_KB) * q_scale).astype(MXU_DTYPE)
    kb = part(_C_KB, _C_VB)
    kb_ref[...] = heads(kb)
    kbb_ref[...] = kb.astype(MXU_DTYPE)
    vb = part(_C_VB, _C_END)
    vb_ref[...] = heads(vb)
    vbb_ref[...] = vb.astype(MXU_DTYPE)


def _project(h, w, bf, cos, sin, tm, tab_blocks):
    n, d = h.shape
    f32, bf16 = jnp.float32, MXU_DTYPE
    kv_a, kv_b = (N_KV_A, HEAD_DIM), (N_HEADS_B, HEAD_DIM)
    tails = [((512,), bf16), (kv_a, f32), (kv_a, f32), ((512,), bf16), ((IDX_DIM,), f32), ((N_IDX_HEADS,), f32),
             ((512,), bf16), (kv_b, f32), (kv_b, f32), ((N_HEADS_B,), f32),
             ((128,), bf16), ((128,), bf16), ((128,), bf16), ((512,), bf16), ((512,), bf16)]
    row = lambda *tail: pl.BlockSpec((tm,) + tail, lambda i: (i,) + (0,) * len(tail))
    tab = pl.BlockSpec((tm, LANES), lambda i: (i % tab_blocks, 0))
    return pl.pallas_call(
        _project_kernel,
        grid=(n // tm,),
        in_specs=[row(d), _const_spec(w.shape), _const_spec((1, LANES)), tab, tab],
        out_specs=[row(*tail) for tail, _ in tails],
        out_shape=[jax.ShapeDtypeStruct((n,) + tail, dt) for tail, dt in tails],
        compiler_params=pltpu.CompilerParams(
            dimension_semantics=("arbitrary",), vmem_limit_bytes=V7X_VMEM_LIMIT_BYTES),
        name="project",
    )(h, w, bf, cos, sin)


def _cumsum_kernel(x_ref, o_ref, *, seg):
    r = lax.broadcasted_iota(jnp.int32, (seg, seg), 0)
    c = lax.broadcasted_iota(jnp.int32, (seg, seg), 1)
    tri = (r <= c).astype(jnp.float32)
    carry = jnp.zeros((x_ref.shape[1], 1), jnp.float32)
    for s in range(x_ref.shape[2] // seg):
        cols = slice(s * seg, (s + 1) * seg)
        part = jnp.dot(x_ref[0, :, cols], tri, preferred_element_type=jnp.float32,
                       precision=lax.Precision.HIGHEST) + carry
        o_ref[0, :, cols] = part * LOG2E
        carry = part[:, seg - 1:seg]


def _cumsum_keys(x):
    b, h, s = x.shape
    seg = 2 * LANES if s % (2 * LANES) == 0 else LANES
    assert s % seg == 0
    return pl.pallas_call(
        functools.partial(_cumsum_kernel, seg=seg),
        grid=(b,),
        in_specs=[pl.BlockSpec((1, h, s), lambda i: (i, 0, 0))],
        out_specs=pl.BlockSpec((1, h, s), lambda i: (i, 0, 0)),
        out_shape=jax.ShapeDtypeStruct((b, h, s), jnp.float32),
        compiler_params=pltpu.CompilerParams(dimension_semantics=("arbitrary",)),
        name="cumsum_keys",
    )(x)


def _num_key_blocks(i, tq, kb, q_off):
    return (q_off + (i + 1) * tq + kb - 1) // kb


def _lane_groups(x):
    return [x[:, g * LANES:(g + 1) * LANES] for g in range(x.shape[1] // LANES)]


def _lane_tile(x, width):
    return jnp.concatenate([x] * (width // LANES), axis=1)


def _softmax_step(s, h, m_ref, l_ref, acc_ref, v_ones):
    groups = _lane_groups(s)
    smax = functools.reduce(jnp.maximum, groups)
    m_old = m_ref[h]
    m_new = jnp.maximum(m_old, jnp.max(smax, axis=-1, keepdims=True))
    m_safe = jnp.maximum(m_new, F32_LOWEST)
    corr = jnp.exp2(m_old - m_safe)
    p = jnp.concatenate([jnp.exp2(g - m_safe) for g in groups], axis=1).astype(MXU_DTYPE)
    pv = _dot(p, v_ones)
    acc_ref[h] = corr * acc_ref[h] + pv[:, :LANES]
    l_ref[h] = corr * l_ref[h] + pv[:, LANES:]
    m_ref[h] = m_new


def _store_values_and_ones(v_ref, lanes, vo_ref):
    vo_ref[:, :LANES] = v_ref[:, lanes]
    vo_ref[:, LANES:] = jnp.ones((vo_ref.shape[0], LANES), vo_ref.dtype)


def _store_half_masks(q_ref, qm_ref):
    for p in range(q_ref.shape[1] // LANES):
        q2 = q_ref[:, p * LANES:(p + 1) * LANES]
        lane = lax.broadcasted_iota(jnp.int32, q2.shape, 1)
        zero = jnp.zeros_like(q2)
        qm_ref[2 * p] = jnp.where(lane < HEAD_DIM, q2, zero)
        qm_ref[2 * p + 1] = jnp.where(lane >= HEAD_DIM, q2, zero)


def _init_state(m_ref, l_ref, acc_ref):
    m_ref[...] = jnp.full(m_ref.shape, NEG_INF, jnp.float32)
    l_ref[...] = jnp.zeros(l_ref.shape, jnp.float32)
    acc_ref[...] = jnp.zeros(acc_ref.shape, jnp.float32)


def _write_pairs(o_ref, l_ref, acc_ref):
    for p in range(o_ref.shape[1] // LANES):
        outs = [acc_ref[h] / l_ref[h] for h in (2 * p, 2 * p + 1)]
        lane = lax.broadcasted_iota(jnp.int32, outs[0].shape, 1)
        o_ref[:, p * LANES:(p + 1) * LANES] = jnp.where(lane < HEAD_DIM, outs[0], outs[1]).astype(o_ref.dtype)


def _attention_scratch(tq, n_heads):
    return [pltpu.VMEM((n_heads, tq, LANES), MXU_DTYPE),
            pltpu.VMEM((n_heads, tq, LANES), jnp.float32),
            pltpu.VMEM((n_heads, tq, LANES), jnp.float32),
            pltpu.VMEM((n_heads, tq, LANES), jnp.float32)]


def _assemble_keys(cache_ref, new_ref, dst_ref, repeat=1):
    past, n_new, s_len = cache_ref.shape[0], new_ref.shape[0], dst_ref.shape[0]
    step = ROW_TILE
    for r0 in range(0, past, step):
        rows = slice(r0, min(r0 + step, past))
        x = cache_ref[rows, :].astype(dst_ref.dtype)
        dst_ref[rows, :] = x if repeat == 1 else jnp.concatenate([x] * repeat, axis=1)
    dst_ref[past:past + n_new, :] = new_ref[...]
    dst_ref[past + n_new:, :] = jnp.zeros((s_len - past - n_new, dst_ref.shape[1]), dst_ref.dtype)


def _fox_kernel(*refs, tq, kb, q_off, cached):
    if cached:
        q_ref, kc_ref, vc_ref, kn_ref, vn_ref, ck_ref, o_ref, k_ref, v_ref, vo_ref, qm_ref, m_ref, l_ref, acc_ref = refs
    else:
        q_ref, k_ref, v_ref, ck_ref, o_ref, vo_ref, qm_ref, m_ref, l_ref, acc_ref = refs
    i = pl.program_id(1)
    nb = _num_key_blocks(i, tq, kb, q_off)
    assert kb % tq == 0 and q_off % tq == 0

    @pl.when(i == 0)
    def _():
        if cached:
            _assemble_keys(kc_ref, kn_ref, k_ref)
            _assemble_keys(vc_ref, vn_ref, v_ref)
        for p in range(N_HEADS_B // 2):
            _store_values_and_ones(v_ref, slice(p * LANES, (p + 1) * LANES), vo_ref.at[p])

    _store_half_masks(q_ref, qm_ref)
    _init_state(m_ref, l_ref, acc_ref)

    def block(j, masked):
        start = pl.multiple_of(j * kb, kb)
        ck = ck_ref[0, j]
        if masked:
            row = q_off + i * tq + lax.broadcasted_iota(jnp.int32, (tq, kb), 0)
            col = j * kb + lax.broadcasted_iota(jnp.int32, (tq, kb), 1)
            causal = col <= row
        for h in range(N_HEADS_B):
            lanes = slice((h // 2) * LANES, (h // 2 + 1) * LANES)
            s = _nt_dot(qm_ref[h], k_ref[pl.ds(start, kb), lanes]) - ck[h:h + 1, :]
            if masked:
                s = jnp.where(causal, s, NEG_INF)
            _softmax_step(s, h, m_ref, l_ref, acc_ref, vo_ref[h // 2, pl.ds(start, kb), :])

    def full_block(j, _):
        block(j, masked=False)
        return 0

    lax.fori_loop(0, nb - 1, full_block, 0)
    block(nb - 1, masked=True)
    _write_pairs(o_ref, l_ref, acc_ref)


def _key_specs(keys, new_keys, batch):
    per_batch = lambda a: pl.BlockSpec((a.shape[0] // batch, a.shape[1]), lambda b, i: (b, 0))
    return [per_batch(a) for a in keys] + [per_batch(a) for a in new_keys]


def _fox(q, k, v, ck, *, batch, tq, kb, q_off, new_kv=()):
    n, width = q.shape
    t_q = n // batch
    s_len = ck.shape[1] * kb
    nq = t_q // tq
    cached = bool(new_kv)
    joined = [pltpu.VMEM((s_len, width), MXU_DTYPE)] * 2 if cached else []
    return pl.pallas_call(
        functools.partial(_fox_kernel, tq=tq, kb=kb, q_off=q_off, cached=cached),
        grid=(batch, nq),
        in_specs=[pl.BlockSpec((tq, width), lambda b, i: (b * nq + i, 0))]
        + _key_specs((k, v), new_kv, batch)
        + [pl.BlockSpec((1,) + ck.shape[1:], lambda b, i: (b, 0, 0, 0))],
        out_specs=pl.BlockSpec((tq, width), lambda b, i: (b * nq + i, 0)),
        out_shape=jax.ShapeDtypeStruct((n, width), MXU_DTYPE),
        scratch_shapes=joined
        + [pltpu.VMEM((N_HEADS_B // 2, s_len, 2 * LANES), MXU_DTYPE)]
        + _attention_scratch(tq, N_HEADS_B),
        compiler_params=pltpu.CompilerParams(
            dimension_semantics=("arbitrary", "arbitrary"), vmem_limit_bytes=V7X_VMEM_LIMIT_BYTES),
        name="fox_attention",
    )(q, k, v, *new_kv, ck)


def _dsa_kernel(*refs, tq, kb, q_off, k_sel, cached):
    qi_ref, wi_ref, qa_ref = refs[:3]
    if cached:
        kic_ref, kac_ref, vac_ref, kin_ref, kan_ref, van_ref, o_ref, ki_ref, ka_ref, va_ref = refs[3:13]
        scratch = refs[13:]
    else:
        ki_ref, ka_ref, va_ref, o_ref = refs[3:7]
        scratch = refs[7:]
    (key_ref, half_ref, bias_ref, thr_ref, cand_ref, cand16_ref, lim_ref, cnt_ref, vo_ref,
     qm_ref, m_ref, l_ref, acc_ref) = scratch
    i = pl.program_id(1)

    @pl.when(i == 0)
    def _():
        if cached:
            _assemble_keys(kic_ref, kin_ref, ki_ref, repeat=LANES // IDX_DIM)
            _assemble_keys(kac_ref, kan_ref, ka_ref)
            _assemble_keys(vac_ref, van_ref, va_ref)
        _store_values_and_ones(va_ref, slice(0, LANES), vo_ref)

    nb = _num_key_blocks(i, tq, kb, q_off)
    k_f = jnp.float32(k_sel)
    rc = min(ROW_CHUNK, tq)
    chunks = [slice(r * rc, (r + 1) * rc) for r in range(tq // rc)]
    groups = [slice(g * LANES, (g + 1) * LANES) for g in range(kb // LANES)]

    def replicate(x):
        return jnp.broadcast_to(x, (tq, LANES))

    _store_half_masks(qi_ref, qm_ref)
    wi = wi_ref[...]

    def score_block(j, _):
        start = pl.multiple_of(j * kb, kb)
        kblk = ki_ref[pl.ds(start, kb), :]
        score = jnp.zeros((tq, kb), jnp.float32)
        for h in range(N_IDX_HEADS):
            score = score + wi[:, h:h + 1] * jnp.maximum(_nt_dot(qm_ref[h], kblk), 0.0)
        bits = lax.bitcast_convert_type(score, jnp.int32)
        key = jnp.where(bits < 0, bits ^ jnp.int32(0x7FFFFFFF), bits)
        row = q_off + i * tq + lax.broadcasted_iota(jnp.int32, (tq, kb), 0)
        col = j * kb + lax.broadcasted_iota(jnp.int32, (tq, kb), 1)
        admissible = (col >> _CHUNK_SHIFT) <= (row >> _CHUNK_SHIFT)
        key = jnp.where(admissible, key, jnp.int32(INT_MIN))
        key_ref[j] = key
        half_ref[j] = (key >> 16).astype(jnp.int16)
        return 0

    lax.fori_loop(0, nb, score_block, 0)

    def count(pred):
        def body(j, acc):
            parts = []
            for rows in chunks:
                hits = [jnp.where(pred(key_ref[j, rows, g], rows, j * kb + g.start), 1.0, 0.0) for g in groups]
                parts.append(functools.reduce(jnp.add, hits))
            return acc + jnp.concatenate(parts, axis=0)
        acc = lax.fori_loop(0, nb, body, jnp.zeros((tq, LANES), jnp.float32))
        return replicate(jnp.sum(acc, axis=-1, keepdims=True))

    def count_half_at_least_cand():
        one, zero = jnp.int16(1), jnp.int16(0)

        def body(j, acc):
            parts = []
            for rows in chunks:
                cand = cand16_ref[rows, :]
                hits = [jnp.where(half_ref[j, rows, g] >= cand, one, zero) for g in groups]
                parts.append(functools.reduce(jnp.add, hits))
            return acc + jnp.concatenate(parts, axis=0)
        acc = lax.fori_loop(0, nb, body, jnp.zeros((tq, LANES), jnp.int16))
        return replicate(jnp.sum(acc.astype(jnp.float32), axis=-1, keepdims=True))

    def bisect_half():
        def test(cand):
            cand_ref[...] = cand
            cand16_ref[...] = cand.astype(jnp.int16)
            c = count_half_at_least_cand()
            take = c >= k_f
            thr_ref[...] = jnp.where(take, cand, thr_ref[...])
            cnt_ref[...] = jnp.where(take, c, cnt_ref[...])

        thr_ref[...] = jnp.full((tq, LANES), INT16_MIN, jnp.int32)
        test(jnp.zeros((tq, LANES), jnp.int32))

        def step(b, _):
            test(thr_ref[...] + (jnp.int32(1) << (jnp.int32(14) - b)))
            return 0

        lax.fori_loop(0, 15, step, 0)
        return thr_ref[...]

    cnt_ref[...] = jnp.full((tq, LANES), k_f, jnp.float32)
    lim_ref[...] = bisect_half()

    def low_halves(j, _):
        for rows in chunks:
            hi = lim_ref[rows, :].astype(jnp.int16)
            for g in groups:
                low = ((key_ref[j, rows, g] & 0xFFFF) + INT16_MIN).astype(jnp.int16)
                high = half_ref[j, rows, g]
                other = jnp.where(high > hi, jnp.int16(INT16_MAX), jnp.int16(INT16_MIN))
                half_ref[j, rows, g] = jnp.where(high == hi, low, other)
        return 0

    lax.fori_loop(0, nb, low_halves, 0)
    thr_lo = bisect_half()
    thr_ref[...] = (lim_ref[...] << 16) + (thr_lo - INT16_MIN)

    thr = thr_ref[...]
    has_thr = thr > jnp.int32(INT_MIN)
    surplus = jnp.logical_and(has_thr, cnt_ref[...] > k_f)

    def bias_without_surplus():
        cand_ref[...] = jnp.maximum(thr_ref[...], jnp.int32(INT_MIN + 1))

        def block(j, _):
            for rows in chunks:
                for g in groups:
                    bias_ref[j, rows, g] = jnp.where(key_ref[j, rows, g] >= cand_ref[rows, :], 0.0, NEG_INF)
            return 0

        lax.fori_loop(0, nb, block, 0)

    def bias_with_surplus():
        above = count(lambda key, rows, first: key > thr_ref[rows, :])
        cnt_ref[...] = jnp.where(has_thr, k_f - above, 0.0)
        r = lax.broadcasted_iota(jnp.int32, (kb, kb), 0)
        c = lax.broadcasted_iota(jnp.int32, (kb, kb), 1)
        tri = (r <= c).astype(MXU_DTYPE)

        def block(j, before):
            key = key_ref[j]
            tied = key == _lane_tile(thr_ref[...], kb)
            rank = _dot(jnp.where(tied, 1.0, 0.0).astype(MXU_DTYPE), tri)
            take = jnp.logical_and(tied, rank + _lane_tile(before, kb) <= _lane_tile(cnt_ref[...], kb))
            sel = jnp.logical_or(key > _lane_tile(thr_ref[...], kb), take)
            bias_ref[j] = jnp.where(sel, 0.0, NEG_INF)
            return before + replicate(jnp.max(rank, axis=-1, keepdims=True))

        lax.fori_loop(0, nb, block, jnp.zeros((tq, LANES), jnp.float32))

    any_surplus = jnp.max(jnp.where(surplus, 1.0, 0.0)) > 0.0
    lax.cond(any_surplus, bias_with_surplus, bias_without_surplus)

    _store_half_masks(qa_ref, qm_ref)
    _init_state(m_ref, l_ref, acc_ref)

    def attend(j, _):
        start = pl.multiple_of(j * kb, kb)
        for h in range(N_HEADS_A):
            s = _nt_dot(qm_ref[h], ka_ref[pl.ds(start, kb), :]) + bias_ref[j]
            _softmax_step(s, h, m_ref, l_ref, acc_ref, vo_ref[pl.ds(start, kb), :])
        return 0

    lax.fori_loop(0, nb, attend, 0)
    _write_pairs(o_ref, l_ref, acc_ref)


def _dsa(qi, wi, qa, ki, ka, va, *, batch, tq, kb, q_off, k_sel, s_len, new_keys=()):
    n, width = qa.shape
    t_q = n // batch
    nq = t_q // tq
    nblk = s_len // kb
    cached = bool(new_keys)
    qspec = lambda wdt: pl.BlockSpec((tq, wdt), lambda b, i: (b * nq + i, 0))
    joined = [pltpu.VMEM((s_len, LANES), MXU_DTYPE)] * 3 if cached else []
    return pl.pallas_call(
        functools.partial(_dsa_kernel, tq=tq, kb=kb, q_off=q_off, k_sel=k_sel, cached=cached),
        grid=(batch, nq),
        in_specs=[qspec(width), qspec(N_IDX_HEADS), qspec(width)] + _key_specs((ki, ka, va), new_keys, batch),
        out_specs=qspec(width),
        out_shape=jax.ShapeDtypeStruct((n, width), MXU_DTYPE),
        scratch_shapes=joined + [
                        pltpu.VMEM((nblk, tq, kb), jnp.int32),
                        pltpu.VMEM((nblk, tq, kb), jnp.int16),
                        pltpu.VMEM((nblk, tq, kb), jnp.float32),
                        pltpu.VMEM((tq, LANES), jnp.int32),
                        pltpu.VMEM((tq, LANES), jnp.int32),
                        pltpu.VMEM((tq, LANES), jnp.int16),
                        pltpu.VMEM((tq, LANES), jnp.int32),
                        pltpu.VMEM((tq, LANES), jnp.float32),
                        pltpu.VMEM((s_len, 2 * LANES), MXU_DTYPE),
                        ] + _attention_scratch(tq, N_HEADS_A),
        compiler_params=pltpu.CompilerParams(
            dimension_semantics=("arbitrary", "arbitrary"), vmem_limit_bytes=V7X_VMEM_LIMIT_BYTES),
        name="dsa_attention",
    )(qi, wi, qa, ki, ka, va, *new_keys)


def _post_kernel(h_ref, oa_ref, ob_ref, woa_ref, wob_ref, g2_ref, b2_ref,
                 wg_ref, wu_ref, wd_ref, g3_ref, b3_ref, o_ref, act_ref):
    mix = _dot(oa_ref[...], woa_ref[...]) + _dot(ob_ref[...], wob_ref[...])
    h2 = _layer_norm(ALPHA * h_ref[...] + mix, g2_ref[...], b2_ref[...])
    o_ref[...] = _macaron_half(h2, wg_ref, wu_ref, wd_ref, g3_ref[...], b3_ref[...], act_ref)


def _post(h, oa, ob, woa, wob, g2, b2, wg, wu, wd, g3, b3):
    n, d = h.shape
    d_ff = wg.shape[1]
    tm = _row_tile(n)
    row = lambda wdt: pl.BlockSpec((tm, wdt), lambda i: (i, 0))
    return pl.pallas_call(
        _post_kernel,
        grid=(n // tm,),
        in_specs=[row(d), row(oa.shape[1]), row(ob.shape[1]),
                  _const_spec(woa.shape), _const_spec(wob.shape), _const_spec((1, d)), _const_spec((1, d)),
                  _const_spec((d, d_ff)), _const_spec((d, d_ff)), _const_spec((d_ff, d)),
                  _const_spec((1, d)), _const_spec((1, d))],
        out_specs=row(d),
        out_shape=jax.ShapeDtypeStruct((n, d), jnp.float32),
        scratch_shapes=[pltpu.VMEM((tm, d_ff), MXU_DTYPE)],
        compiler_params=pltpu.CompilerParams(
            dimension_semantics=("arbitrary",), vmem_limit_bytes=V7X_VMEM_LIMIT_BYTES),
        name="post",
    )(h, oa, ob, woa, wob, g2, b2, wg, wu, wd, g3, b3)


def _prepare_w_in(w_in):
    d = w_in.shape[0]
    sizes = (N_HEADS_A * HEAD_DIM, N_KV_A * HEAD_DIM, N_KV_A * HEAD_DIM, N_IDX_HEADS * IDX_DIM, IDX_DIM,
             N_IDX_HEADS, N_HEADS_B * HEAD_DIM, N_HEADS_B * HEAD_DIM, N_HEADS_B * HEAD_DIM, N_HEADS_B)
    offs = [0]
    for s in sizes:
        offs.append(offs[-1] + s)
    qa, ka, va, qi, ki, wi, qb, kb, vb, fb = (w_in[:, offs[k]:offs[k + 1]] for k in range(10))
    qa = qa.reshape(d, N_HEADS_A, HEAD_DIM)[:, jnp.array(_QA_HEAD_ORDER)].reshape(d, -1)
    pad = jnp.zeros((d, LANES - N_IDX_HEADS - N_HEADS_B), w_in.dtype)
    w = jnp.concatenate([qa, ka, va, qi, ki, ki, wi, fb, pad, qb, kb, vb], axis=1)
    assert w.shape[1] == _C_END
    return w.astype(MXU_DTYPE)


def _rope_tables(pos):
    half = ROT_DIM // 2
    inv_freq = ROPE_THETA ** (-jnp.arange(half, dtype=jnp.float32) * 2.0 / ROT_DIM)
    ang = pos.astype(jnp.float32)[:, None] * inv_freq[None, :]
    cos, sin = jnp.cos(ang), jnp.sin(ang)
    ones = jnp.ones((pos.shape[0], HEAD_DIM - ROT_DIM), jnp.float32)
    cos64 = jnp.concatenate([cos, cos, ones], axis=1)
    sin64 = jnp.concatenate([-sin, sin, jnp.zeros_like(ones)], axis=1)
    return jnp.tile(cos64, (1, LANES // HEAD_DIM)), jnp.tile(sin64, (1, LANES // HEAD_DIM))


def _project_tokens(h, w_in_p, bf_p, pos, rows_per_seq):
    n = h.shape[0]
    tm = _row_tile(n)
    cos, sin = _rope_tables(pos)
    if tm <= rows_per_seq:
        assert rows_per_seq % tm == 0
        tab_blocks = rows_per_seq // tm
    else:
        assert tm % rows_per_seq == 0
        cos = jnp.tile(cos, (tm // rows_per_seq, 1))
        sin = jnp.tile(sin, (tm // rows_per_seq, 1))
        tab_blocks = 1
    return _project(h, w_in_p, bf_p, cos, sin, tm, tab_blocks)


def _blocked_cum(logf_keys, kb):
    b, s, h = logf_keys.shape
    cum = _cumsum_keys(jnp.transpose(logf_keys, (0, 2, 1)))
    return jnp.transpose(cum.reshape(b, h, s // kb, kb), (0, 2, 1, 3))


def _pad_keys(x, s_pad):
    return jnp.pad(x, ((0, 0), (0, s_pad - x.shape[1]), (0, 0)))


def kernel(x_prompt, x_sample, cache_k_a, cache_v_a, cache_kidx_a, cache_k_b, cache_v_b, cache_logf_b,
           w_in, b_f, w_out, ln1_g, ln1_b, ffn1_w_gate, ffn1_w_up, ffn1_w_down,
           ln2_g, ln2_b, ln3_g, ln3_b, ffn2_w_gate, ffn2_w_up, ffn2_w_down):
    assert w_in.shape[0] == DEPTH
    bsz, seq, d = x_prompt.shape
    dbs, dseq, _ = x_sample.shape
    past = cache_k_a.shape[2]
    bf16 = MXU_DTYPE

    w_in_p = _prepare_w_in(w_in[0])
    bf_p = jnp.zeros((1, LANES), jnp.float32).at[0, N_IDX_HEADS:N_IDX_HEADS + N_HEADS_B].set(b_f[0])
    order = jnp.array(_QA_HEAD_ORDER)
    w_out_a = w_out[0, :N_HEADS_A * HEAD_DIM].reshape(N_HEADS_A, HEAD_DIM, d)[order].reshape(-1, d).astype(bf16)
    w_out_b = w_out[0, N_HEADS_A * HEAD_DIM:].astype(bf16)
    ffn1 = (ffn1_w_gate[0].astype(bf16), ffn1_w_up[0].astype(bf16), ffn1_w_down[0].astype(bf16))
    ffn2 = (ffn2_w_gate[0].astype(bf16), ffn2_w_up[0].astype(bf16), ffn2_w_down[0].astype(bf16))
    vec = lambda a: a[0].reshape(1, d)

    def layer(x, pos, rows_per_seq, attend):
        n = x.shape[0] * x.shape[1]
        h = _ffn_ln(x.reshape(n, d), *ffn1, vec(ln1_g), vec(ln1_b))
        proj = _project_tokens(h, w_in_p, bf_p, pos, rows_per_seq)
        oa, ob = attend(proj)
        y = _post(h, oa, ob, w_out_a, w_out_b, vec(ln2_g), vec(ln2_b), *ffn2, vec(ln3_g), vec(ln3_b))
        return y.reshape(x.shape), proj

    tq_p = min(Q_TILE, seq)
    kb_p = min(KEY_BLOCK, seq)

    def attend_prompt(proj):
        qa, _, _, qi, _, wi, qb, _, _, logf, kab, vab, kib, kbb, vbb = proj
        ck = _blocked_cum(logf.reshape(bsz, seq, N_HEADS_B), kb_p)
        oa = _dsa(qi, wi, qa, kib, kab, vab, batch=bsz, tq=tq_p, kb=kb_p, q_off=0,
                  k_sel=min(TOPK_MAX, seq // 4), s_len=seq)
        ob = _fox(qb, kbb, vbb, ck, batch=bsz, tq=tq_p, kb=kb_p, q_off=0)
        return oa, ob

    y_p, proj_p = layer(x_prompt, jnp.arange(seq, dtype=jnp.int32), seq, attend_prompt)

    total = past + dseq
    kb_s = min(SAMPLE_KEY_BLOCK, past)
    s_pad = -(-total // kb_s) * kb_s

    def attend_sample(proj):
        qa, _, _, qi, _, wi, qb, _, _, logf, kab, vab, kib, kbb, vbb = proj
        cache = lambda c: c[0].reshape(dbs * past, -1)
        logf_all = _pad_keys(jnp.concatenate(
            [cache_logf_b[0].astype(jnp.float32), logf.reshape(dbs, dseq, -1)], axis=1), s_pad)
        ck = _blocked_cum(logf_all, kb_s)
        oa = _dsa(qi, wi, qa, cache(cache_kidx_a), cache(cache_k_a), cache(cache_v_a),
                  batch=dbs, tq=dseq, kb=kb_s, q_off=past, k_sel=min(TOPK_MAX, total // 4), s_len=s_pad,
                  new_keys=(kib, kab, vab))
        ob = _fox(qb, cache(cache_k_b), cache(cache_v_b), ck, batch=dbs, tq=dseq, kb=kb_s, q_off=past,
                  new_kv=(kbb, vbb))
        return oa, ob

    y_s, proj_s = layer(x_sample, past + jnp.arange(dseq, dtype=jnp.int32), dseq, attend_sample)

    def rows(proj, b, t):
        _, ka, va, _, kidx, _, _, kbv, vbv, logf = proj[:10]
        return (ka.reshape(1, b, t, N_KV_A, HEAD_DIM), va.reshape(1, b, t, N_KV_A, HEAD_DIM),
                kidx.reshape(1, b, t, IDX_DIM), kbv.reshape(1, b, t, N_HEADS_B, HEAD_DIM),
                vbv.reshape(1, b, t, N_HEADS_B, HEAD_DIM), logf.reshape(1, b, t, N_HEADS_B))

    return (y_p, y_s) + rows(proj_p, bsz, seq) + rows(proj_s, dbs, dseq)
```

```python
import functools
import math

import jax
import jax.numpy as jnp
from jax import lax
from jax.experimental import pallas as pl
from jax.experimental.pallas import tpu as pltpu

CHUNK = 64
_CHUNK_SHIFT = 6
HEAD_DIM = 64
N_HEADS_A = 8
N_KV_A = 2
N_IDX_HEADS = 8
IDX_DIM = 64
TOPK_MAX = 256
N_HEADS_B = 8
ROT_DIM = HEAD_DIM // 4
ROPE_THETA = 500000.0
LN_EPS = 1e-5
DEPTH = 1
ALPHA = (2.0 * DEPTH) ** 0.25

MXU_DTYPE = jnp.bfloat16

LANES = 128
V7X_VMEM_LIMIT_BYTES = 60 * 1024 * 1024

ROW_TILE = 512
FF_CHUNK = 256
Q_TILE = 512
KEY_BLOCK = 512
SAMPLE_KEY_BLOCK = 512
ROW_CHUNK = 64

INT_MIN = -(2 ** 31)
FIELD_BITS = 15
FIELD_GUARD = 1 << FIELD_BITS
FIELD_FLAGS = 0x00010001
NEG_INF = float("-inf")
F32_LOWEST = float(jnp.finfo(jnp.float32).min)
LOG2E = math.log2(math.e)

_C_QA, _C_KA, _C_VA, _C_QI, _C_KI, _C_WF, _C_QB, _C_KB, _C_VB, _C_END = (
    0, 512, 640, 768, 1280, 1408, 1536, 2048, 2560, 3072)
_QA_HEAD_ORDER = (0, 4, 1, 5, 2, 6, 3, 7)


def _nt_dot(a, b):
    return lax.dot_general(a, b, (((1,), (1,)), ((), ())), preferred_element_type=jnp.float32)


def _dot(a, b):
    return jnp.dot(a, b, preferred_element_type=jnp.float32)


def _layer_norm(x, g, b):
    mu = jnp.mean(x, axis=-1, keepdims=True)
    xc = x - mu
    var = jnp.mean(xc * xc, axis=-1, keepdims=True)
    return xc * lax.rsqrt(var + LN_EPS) * g + b


def _macaron_half(x, wg_ref, wu_ref, wd_ref, g, b, act_ref):
    xb = x.astype(MXU_DTYPE)
    d_ff = wg_ref.shape[1]
    for c in range(d_ff // FF_CHUNK):
        cols = slice(c * FF_CHUNK, (c + 1) * FF_CHUNK)
        gate = _dot(xb, wg_ref[:, cols])
        up = _dot(xb, wu_ref[:, cols])
        act_ref[:, cols] = (gate * jax.nn.sigmoid(gate) * up).astype(MXU_DTYPE)
    down = _dot(act_ref[...], wd_ref[...])
    return _layer_norm(ALPHA * x + 0.5 * down, g, b)


def _ffn_ln_kernel(x_ref, wg_ref, wu_ref, wd_ref, g_ref, b_ref, o_ref, act_ref):
    o_ref[...] = _macaron_half(x_ref[...], wg_ref, wu_ref, wd_ref, g_ref[...], b_ref[...], act_ref)


def _const_spec(shape):
    return pl.BlockSpec(shape, lambda *_: (0,) * len(shape), pipeline_mode=pl.Buffered(1))


def _row_tile(n):
    tm = ROW_TILE
    while n % tm:
        tm //= 2
    return tm


def _ffn_ln(x, wg, wu, wd, g, b):
    n, d = x.shape
    d_ff = wg.shape[1]
    tm = _row_tile(n)
    return pl.pallas_call(
        _ffn_ln_kernel,
        grid=(n // tm,),
        in_specs=[
            pl.BlockSpec((tm, d), lambda i: (i, 0)),
            _const_spec((d, d_ff)), _const_spec((d, d_ff)), _const_spec((d_ff, d)),
            _const_spec((1, d)), _const_spec((1, d)),
        ],
        out_specs=pl.BlockSpec((tm, d), lambda i: (i, 0)),
        out_shape=jax.ShapeDtypeStruct((n, d), jnp.float32),
        scratch_shapes=[pltpu.VMEM((tm, d_ff), MXU_DTYPE)],
        compiler_params=pltpu.CompilerParams(
            dimension_semantics=("arbitrary",), vmem_limit_bytes=V7X_VMEM_LIMIT_BYTES),
        name="ffn_ln",
    )(x, wg, wu, wd, g, b)


def _rope(x, cos, sin):
    lane = lax.broadcasted_iota(jnp.int32, x.shape, 1) & (HEAD_DIM - 1)
    partner = jnp.where(lane < ROT_DIM // 2,
                        pltpu.roll(x, LANES - ROT_DIM // 2, 1),
                        pltpu.roll(x, ROT_DIM // 2, 1))
    return x * cos + partner * sin


def _project_kernel(h_ref, w_ref, bf_ref, cos_ref, sin_ref,
                    qa_ref, ka_ref, va_ref, qi_ref, kidx_ref, wi_ref, qb_ref, kb_ref, vb_ref, logf_ref,
                    kab_ref, vab_ref, kib_ref, kbb_ref, vbb_ref):
    hb = h_ref[...].astype(MXU_DTYPE)
    cos = cos_ref[...]
    sin = sin_ref[...]
    q_scale = HEAD_DIM ** -0.5 * LOG2E

    def part(lo, hi):
        return _dot(hb, w_ref[:, lo:hi])

    def rope_groups(z):
        return jnp.concatenate(
            [_rope(z[:, g * LANES:(g + 1) * LANES], cos, sin) for g in range(z.shape[1] // LANES)], axis=1)

    qa_ref[...] = (rope_groups(part(_C_QA, _C_KA)) * q_scale).astype(MXU_DTYPE)
    def heads(z):
        return z.reshape(z.shape[0], z.shape[1] // HEAD_DIM, HEAD_DIM)

    ka = _rope(part(_C_KA, _C_VA), cos, sin)
    ka_ref[...] = heads(ka)
    kab_ref[...] = ka.astype(MXU_DTYPE)
    va = part(_C_VA, _C_QI)
    va_ref[...] = heads(va)
    vab_ref[...] = va.astype(MXU_DTYPE)
    qi_ref[...] = (rope_groups(part(_C_QI, _C_KI)) * (IDX_DIM ** -0.5)).astype(MXU_DTYPE)
    ki = _rope(part(_C_KI, _C_WF), cos, sin)
    kidx_ref[...] = ki[:, :IDX_DIM]
    kib_ref[...] = ki.astype(MXU_DTYPE)
    wf = part(_C_WF, _C_QB)
    wi_ref[...] = wf[:, :N_IDX_HEADS] * (N_IDX_HEADS ** -0.5)
    f = wf + bf_ref[...]
    logf = -(jnp.maximum(-f, 0.0) + jnp.log1p(jnp.exp(-jnp.abs(f))))
    logf_ref[...] = pltpu.roll(logf, LANES - N_IDX_HEADS, 1)[:, :N_HEADS_B]
    qb_ref[...] = (part(_C_QB, _C_KB) * q_scale).astype(MXU_DTYPE)
    kb = part(_C_KB, _C_VB)
    kb_ref[...] = heads(kb)
    kbb_ref[...] = kb.astype(MXU_DTYPE)
    vb = part(_C_VB, _C_END)
    vb_ref[...] = heads(vb)
    vbb_ref[...] = vb.astype(MXU_DTYPE)


def _project(h, w, bf, cos, sin, tm, tab_blocks):
    n, d = h.shape
    f32, bf16 = jnp.float32, MXU_DTYPE
    kv_a, kv_b = (N_KV_A, HEAD_DIM), (N_HEADS_B, HEAD_DIM)
    tails = [((512,), bf16), (kv_a, f32), (kv_a, f32), ((512,), bf16), ((IDX_DIM,), f32), ((N_IDX_HEADS,), f32),
             ((512,), bf16), (kv_b, f32), (kv_b, f32), ((N_HEADS_B,), f32),
             ((128,), bf16), ((128,), bf16), ((128,), bf16), ((512,), bf16), ((512,), bf16)]
    row = lambda *tail: pl.BlockSpec((tm,) + tail, lambda i: (i,) + (0,) * len(tail))
    tab = pl.BlockSpec((tm, LANES), lambda i: (i % tab_blocks, 0))
    return pl.pallas_call(
        _project_kernel,
        grid=(n // tm,),
        in_specs=[row(d), _const_spec(w.shape), _const_spec((1, LANES)), tab, tab],
        out_specs=[row(*tail) for tail, _ in tails],
        out_shape=[jax.ShapeDtypeStruct((n,) + tail, dt) for tail, dt in tails],
        compiler_params=pltpu.CompilerParams(
            dimension_semantics=("arbitrary",), vmem_limit_bytes=V7X_VMEM_LIMIT_BYTES),
        name="project",
    )(h, w, bf, cos, sin)


def _cumsum_kernel(x_ref, o_ref, *, seg):
    r = lax.broadcasted_iota(jnp.int32, (seg, seg), 0)
    c = lax.broadcasted_iota(jnp.int32, (seg, seg), 1)
    tri = (r <= c).astype(jnp.float32)
    carry = jnp.zeros((x_ref.shape[1], 1), jnp.float32)
    for s in range(x_ref.shape[2] // seg):
        cols = slice(s * seg, (s + 1) * seg)
        part = jnp.dot(x_ref[0, :, cols], tri, preferred_element_type=jnp.float32,
                       precision=lax.Precision.HIGHEST) + carry
        o_ref[0, :, cols] = part * LOG2E
        carry = part[:, seg - 1:seg]


def _cumsum_keys(x):
    b, h, s = x.shape
    seg = 2 * LANES if s % (2 * LANES) == 0 else LANES
    assert s % seg == 0
    return pl.pallas_call(
        functools.partial(_cumsum_kernel, seg=seg),
        grid=(b,),
        in_specs=[pl.BlockSpec((1, h, s), lambda i: (i, 0, 0))],
        out_specs=pl.BlockSpec((1, h, s), lambda i: (i, 0, 0)),
        out_shape=jax.ShapeDtypeStruct((b, h, s), jnp.float32),
        compiler_params=pltpu.CompilerParams(dimension_semantics=("arbitrary",)),
        name="cumsum_keys",
    )(x)


def _num_key_blocks(i, tq, kb, q_off):
    return (q_off + (i + 1) * tq + kb - 1) // kb


def _lane_groups(x):
    return [x[:, g * LANES:(g + 1) * LANES] for g in range(x.shape[1] // LANES)]


def _lane_tile(x, width):
    return jnp.concatenate([x] * (width // LANES), axis=1)


def _softmax_step(s, h, m_ref, l_ref, acc_ref, v_ones):
    groups = _lane_groups(s)
    smax = functools.reduce(jnp.maximum, groups)
    m_old = m_ref[h]
    m_new = jnp.maximum(m_old, jnp.max(smax, axis=-1, keepdims=True))
    m_safe = jnp.maximum(m_new, F32_LOWEST)
    corr = jnp.exp2(m_old - m_safe)
    p = jnp.concatenate([jnp.exp2(g - m_safe) for g in groups], axis=1).astype(MXU_DTYPE)
    pv = _dot(p, v_ones)
    acc_ref[h] = corr * acc_ref[h] + pv[:, :LANES]
    l_ref[h] = corr * l_ref[h] + pv[:, LANES:]
    m_ref[h] = m_new


def _store_values_and_ones(v_ref, lanes, vo_ref):
    vo_ref[:, :LANES] = v_ref[:, lanes]
    vo_ref[:, LANES:] = jnp.ones((vo_ref.shape[0], LANES), vo_ref.dtype)


def _store_half_masks(q_ref, qm_ref):
    for p in range(q_ref.shape[1] // LANES):
        q2 = q_ref[:, p * LANES:(p + 1) * LANES]
        lane = lax.broadcasted_iota(jnp.int32, q2.shape, 1)
        zero = jnp.zeros_like(q2)
        qm_ref[2 * p] = jnp.where(lane < HEAD_DIM, q2, zero)
        qm_ref[2 * p + 1] = jnp.where(lane >= HEAD_DIM, q2, zero)


def _init_state(m_ref, l_ref, acc_ref):
    m_ref[...] = jnp.full(m_ref.shape, NEG_INF, jnp.float32)
    l_ref[...] = jnp.zeros(l_ref.shape, jnp.float32)
    acc_ref[...] = jnp.zeros(acc_ref.shape, jnp.float32)


def _write_pairs(o_ref, l_ref, acc_ref):
    for p in range(o_ref.shape[1] // LANES):
        outs = [acc_ref[h] / l_ref[h] for h in (2 * p, 2 * p + 1)]
        lane = lax.broadcasted_iota(jnp.int32, outs[0].shape, 1)
        o_ref[:, p * LANES:(p + 1) * LANES] = jnp.where(lane < HEAD_DIM, outs[0], outs[1]).astype(o_ref.dtype)


def _attention_scratch(tq, n_heads):
    return [pltpu.VMEM((n_heads, tq, LANES), MXU_DTYPE),
            pltpu.VMEM((n_heads, tq, LANES), jnp.float32),
            pltpu.VMEM((n_heads, tq, LANES), jnp.float32),
            pltpu.VMEM((n_heads, tq, LANES), jnp.float32)]


def _fox_kernel(q_ref, k_ref, v_ref, ck_ref, o_ref, vo_ref, qm_ref, m_ref, l_ref, acc_ref, *, tq, kb, q_off):
    i = pl.program_id(1)
    nb = _num_key_blocks(i, tq, kb, q_off)
    assert kb % tq == 0 and q_off % tq == 0

    @pl.when(i == 0)
    def _():
        for p in range(N_HEADS_B // 2):
            _store_values_and_ones(v_ref, slice(p * LANES, (p + 1) * LANES), vo_ref.at[p])

    _store_half_masks(q_ref, qm_ref)
    _init_state(m_ref, l_ref, acc_ref)

    def block(j, masked):
        start = pl.multiple_of(j * kb, kb)
        ck = ck_ref[0, j]
        if masked:
            row = q_off + i * tq + lax.broadcasted_iota(jnp.int32, (tq, kb), 0)
            col = j * kb + lax.broadcasted_iota(jnp.int32, (tq, kb), 1)
            causal = col <= row
        for h in range(N_HEADS_B):
            lanes = slice((h // 2) * LANES, (h // 2 + 1) * LANES)
            s = _nt_dot(qm_ref[h], k_ref[pl.ds(start, kb), lanes]) - ck[h:h + 1, :]
            if masked:
                s = jnp.where(causal, s, NEG_INF)
            _softmax_step(s, h, m_ref, l_ref, acc_ref, vo_ref[h // 2, pl.ds(start, kb), :])

    def full_block(j, _):
        block(j, masked=False)
        return 0

    lax.fori_loop(0, nb - 1, full_block, 0)
    block(nb - 1, masked=True)
    _write_pairs(o_ref, l_ref, acc_ref)


def _fox(q, k, v, ck, *, batch, tq, kb, q_off):
    n, width = q.shape
    t_q = n // batch
    s_len = k.shape[0] // batch
    nq = t_q // tq
    return pl.pallas_call(
        functools.partial(_fox_kernel, tq=tq, kb=kb, q_off=q_off),
        grid=(batch, nq),
        in_specs=[
            pl.BlockSpec((tq, width), lambda b, i: (b * nq + i, 0)),
            pl.BlockSpec((s_len, width), lambda b, i: (b, 0)),
            pl.BlockSpec((s_len, width), lambda b, i: (b, 0)),
            pl.BlockSpec((1,) + ck.shape[1:], lambda b, i: (b, 0, 0, 0)),
        ],
        out_specs=pl.BlockSpec((tq, width), lambda b, i: (b * nq + i, 0)),
        out_shape=jax.ShapeDtypeStruct((n, width), MXU_DTYPE),
        scratch_shapes=[pltpu.VMEM((N_HEADS_B // 2, s_len, 2 * LANES), MXU_DTYPE)]
        + _attention_scratch(tq, N_HEADS_B),
        compiler_params=pltpu.CompilerParams(
            dimension_semantics=("arbitrary", "arbitrary"), vmem_limit_bytes=V7X_VMEM_LIMIT_BYTES),
        name="fox_attention",
    )(q, k, v, ck)


def _dsa_kernel(qi_ref, wi_ref, qa_ref, ki_ref, ka_ref, va_ref, o_ref,
                key_ref, word_ref, bias_ref, thr_ref, cand_ref, lim_ref, cnt_ref, vo_ref,
                qm_ref, m_ref, l_ref, acc_ref,
                *, tq, kb, q_off, k_sel):
    i = pl.program_id(1)
    nb = _num_key_blocks(i, tq, kb, q_off)
    k_f = jnp.float32(k_sel)
    rc = min(ROW_CHUNK, tq)
    chunks = [slice(r * rc, (r + 1) * rc) for r in range(tq // rc)]
    groups = [slice(g * LANES, (g + 1) * LANES) for g in range(kb // LANES)]

    def replicate(x):
        return jnp.broadcast_to(x, (tq, LANES))

    def pack_fields(f):
        return (f[:, :kb // 2] << 16) | f[:, kb // 2:]

    assert kb % (2 * LANES) == 0
    _store_half_masks(qi_ref, qm_ref)
    wi = wi_ref[...]

    def score_block(j, _):
        start = pl.multiple_of(j * kb, kb)
        kblk = ki_ref[pl.ds(start, kb), :]
        score = jnp.zeros((tq, kb), jnp.float32)
        for h in range(N_IDX_HEADS):
            score = score + wi[:, h:h + 1] * jnp.maximum(_nt_dot(qm_ref[h], kblk), 0.0)
        bits = lax.bitcast_convert_type(score, jnp.int32)
        key = jnp.where(bits < 0, bits ^ jnp.int32(0x7FFFFFFF), bits)
        row = q_off + i * tq + lax.broadcasted_iota(jnp.int32, (tq, kb), 0)
        col = j * kb + lax.broadcasted_iota(jnp.int32, (tq, kb), 1)
        admissible = (col >> _CHUNK_SHIFT) <= (row >> _CHUNK_SHIFT)
        key = jnp.where(admissible, key, jnp.int32(INT_MIN))
        key_ref[j] = key
        word_ref[j] = jnp.transpose(pack_fields((key >> (32 - FIELD_BITS)) + (FIELD_GUARD + FIELD_GUARD // 2)))
        return 0

    lax.fori_loop(0, nb, score_block, 0)

    def count(pred):
        def body(j, acc):
            parts = []
            for rows in chunks:
                hits = [jnp.where(pred(key_ref[j, rows, g], rows, j * kb + g.start), 1.0, 0.0) for g in groups]
                parts.append(functools.reduce(jnp.add, hits))
            return acc + jnp.concatenate(parts, axis=0)
        acc = lax.fori_loop(0, nb, body, jnp.zeros((tq, LANES), jnp.float32))
        return replicate(jnp.sum(acc, axis=-1, keepdims=True))

    def count_fields_at_least(cand):
        both = (cand << 16) | cand

        def body(j, acc):
            for r0 in range(0, kb // 2, ROW_CHUNK):
                words = word_ref[j, r0:r0 + ROW_CHUNK, :]
                flags = lax.shift_right_logical(words - both, FIELD_BITS) & FIELD_FLAGS
                acc = acc + jnp.sum(flags.reshape(ROW_CHUNK // 8, 8, tq), axis=0)
            return acc
        acc = lax.fori_loop(0, nb, body, jnp.zeros((8, tq), jnp.int32))
        return jnp.sum((acc & 0xFFFF) + lax.shift_right_logical(acc, 16), axis=0, keepdims=True)

    def bisect_fields(cnt):
        def step(b, carry):
            thr, cnt = carry
            cand = thr | (jnp.int32(1) << (jnp.int32(FIELD_BITS - 1) - b))
            c = count_fields_at_least(cand)
            take = c >= k_sel
            return jnp.where(take, cand, thr), jnp.where(take, c, cnt)

        return lax.fori_loop(0, FIELD_BITS, step, (jnp.zeros((1, tq), jnp.int32), cnt))

    def per_row(row):
        return jnp.transpose(jnp.broadcast_to(row, (LANES, tq)))

    top, cnt = bisect_fields(jnp.full((1, tq), k_sel, jnp.int32))
    lim_ref[...] = per_row(top << FIELD_BITS)

    def middle_fields(j, _):
        words = []
        for rows in chunks:
            base = lim_ref[rows, :] - (FIELD_GUARD + (1 << 29))
            mid = [jnp.clip((key_ref[j, rows, g] >> 2) - base, FIELD_GUARD, 2 * FIELD_GUARD - 1) for g in groups]
            words.append(pack_fields(jnp.concatenate(mid, axis=1)))
        word_ref[j] = jnp.transpose(jnp.concatenate(words, axis=0))
        return 0

    lax.fori_loop(0, nb, middle_fields, 0)
    middle, cnt = bisect_fields(cnt)
    prefix = (top << FIELD_BITS) | middle
    thr_ref[...] = per_row((prefix << 2) ^ jnp.int32(INT_MIN))
    cnt_ref[...] = per_row(cnt).astype(jnp.float32)

    def low_bit(b, _):
        cand = thr_ref[...] | (jnp.int32(2) >> b)
        cand_ref[...] = cand
        c = count(lambda key, rows, first: key >= cand_ref[rows, :])
        take = c >= k_f
        thr_ref[...] = jnp.where(take, cand, thr_ref[...])
        cnt_ref[...] = jnp.where(take, c, cnt_ref[...])
        return 0

    lax.fori_loop(0, 2, low_bit, 0)

    thr = thr_ref[...]
    has_thr = thr > jnp.int32(INT_MIN)
    surplus = jnp.logical_and(has_thr, cnt_ref[...] > k_f)

    def bias_without_surplus():
        cand_ref[...] = jnp.maximum(thr_ref[...], jnp.int32(INT_MIN + 1))

        def block(j, _):
            for rows in chunks:
                for g in groups:
                    bias_ref[j, rows, g] = jnp.where(key_ref[j, rows, g] >= cand_ref[rows, :], 0.0, NEG_INF)
            return 0

        lax.fori_loop(0, nb, block, 0)

    def bias_with_surplus():
        above = count(lambda key, rows, first: key > thr_ref[rows, :])
        cnt_ref[...] = jnp.where(has_thr, k_f - above, 0.0)
        r = lax.broadcasted_iota(jnp.int32, (kb, kb), 0)
        c = lax.broadcasted_iota(jnp.int32, (kb, kb), 1)
        tri = (r <= c).astype(MXU_DTYPE)

        def block(j, before):
            key = key_ref[j]
            tied = key == _lane_tile(thr_ref[...], kb)
            rank = _dot(jnp.where(tied, 1.0, 0.0).astype(MXU_DTYPE), tri)
            take = jnp.logical_and(tied, rank + _lane_tile(before, kb) <= _lane_tile(cnt_ref[...], kb))
            sel = jnp.logical_or(key > _lane_tile(thr_ref[...], kb), take)
            bias_ref[j] = jnp.where(sel, 0.0, NEG_INF)
            return before + replicate(jnp.max(rank, axis=-1, keepdims=True))

        lax.fori_loop(0, nb, block, jnp.zeros((tq, LANES), jnp.float32))

    any_surplus = jnp.max(jnp.where(surplus, 1.0, 0.0)) > 0.0
    lax.cond(any_surplus, bias_with_surplus, bias_without_surplus)

    @pl.when(i == 0)
    def _():
        _store_values_and_ones(va_ref, slice(0, LANES), vo_ref)

    _store_half_masks(qa_ref, qm_ref)
    _init_state(m_ref, l_ref, acc_ref)

    def attend(j, _):
        start = pl.multiple_of(j * kb, kb)
        for h in range(N_HEADS_A):
            s = _nt_dot(qm_ref[h], ka_ref[pl.ds(start, kb), :]) + bias_ref[j]
            _softmax_step(s, h, m_ref, l_ref, acc_ref, vo_ref[pl.ds(start, kb), :])
        return 0

    lax.fori_loop(0, nb, attend, 0)
    _write_pairs(o_ref, l_ref, acc_ref)


def _dsa(qi, wi, qa, ki, ka, va, *, batch, tq, kb, q_off, k_sel):
    n, width = qa.shape
    t_q = n // batch
    s_len = ka.shape[0] // batch
    nq = t_q // tq
    nblk = s_len // kb
    qspec = lambda wdt: pl.BlockSpec((tq, wdt), lambda b, i: (b * nq + i, 0))
    kspec = pl.BlockSpec((s_len, LANES), lambda b, i: (b, 0))
    return pl.pallas_call(
        functools.partial(_dsa_kernel, tq=tq, kb=kb, q_off=q_off, k_sel=k_sel),
        grid=(batch, nq),
        in_specs=[qspec(width), qspec(N_IDX_HEADS), qspec(width), kspec, kspec, kspec],
        out_specs=qspec(width),
        out_shape=jax.ShapeDtypeStruct((n, width), MXU_DTYPE),
        scratch_shapes=[pltpu.VMEM((nblk, tq, kb), jnp.int32),
                        pltpu.VMEM((nblk, kb // 2, tq), jnp.int32),
                        pltpu.VMEM((nblk, tq, kb), jnp.float32),
                        pltpu.VMEM((tq, LANES), jnp.int32),
                        pltpu.VMEM((tq, LANES), jnp.int32),
                        pltpu.VMEM((tq, LANES), jnp.int32),
                        pltpu.VMEM((tq, LANES), jnp.float32),
                        pltpu.VMEM((s_len, 2 * LANES), MXU_DTYPE),
                        ] + _attention_scratch(tq, N_HEADS_A),
        compiler_params=pltpu.CompilerParams(
            dimension_semantics=("arbitrary", "arbitrary"), vmem_limit_bytes=V7X_VMEM_LIMIT_BYTES),
        name="dsa_attention",
    )(qi, wi, qa, ki, ka, va)


def _post_kernel(h_ref, oa_ref, ob_ref, woa_ref, wob_ref, g2_ref, b2_ref,
                 wg_ref, wu_ref, wd_ref, g3_ref, b3_ref, o_ref, act_ref):
    mix = _dot(oa_ref[...], woa_ref[...]) + _dot(ob_ref[...], wob_ref[...])
    h2 = _layer_norm(ALPHA * h_ref[...] + mix, g2_ref[...], b2_ref[...])
    o_ref[...] = _macaron_half(h2, wg_ref, wu_ref, wd_ref, g3_ref[...], b3_ref[...], act_ref)


def _post(h, oa, ob, woa, wob, g2, b2, wg, wu, wd, g3, b3):
    n, d = h.shape
    d_ff = wg.shape[1]
    tm = _row_tile(n)
    row = lambda wdt: pl.BlockSpec((tm, wdt), lambda i: (i, 0))
    return pl.pallas_call(
        _post_kernel,
        grid=(n // tm,),
        in_specs=[row(d), row(oa.shape[1]), row(ob.shape[1]),
                  _const_spec(woa.shape), _const_spec(wob.shape), _const_spec((1, d)), _const_spec((1, d)),
                  _const_spec((d, d_ff)), _const_spec((d, d_ff)), _const_spec((d_ff, d)),
                  _const_spec((1, d)), _const_spec((1, d))],
        out_specs=row(d),
        out_shape=jax.ShapeDtypeStruct((n, d), jnp.float32),
        scratch_shapes=[pltpu.VMEM((tm, d_ff), MXU_DTYPE)],
        compiler_params=pltpu.CompilerParams(
            dimension_semantics=("arbitrary",), vmem_limit_bytes=V7X_VMEM_LIMIT_BYTES),
        name="post",
    )(h, oa, ob, woa, wob, g2, b2, wg, wu, wd, g3, b3)


def _prepare_w_in(w_in):
    d = w_in.shape[0]
    sizes = (N_HEADS_A * HEAD_DIM, N_KV_A * HEAD_DIM, N_KV_A * HEAD_DIM, N_IDX_HEADS * IDX_DIM, IDX_DIM,
             N_IDX_HEADS, N_HEADS_B * HEAD_DIM, N_HEADS_B * HEAD_DIM, N_HEADS_B * HEAD_DIM, N_HEADS_B)
    offs = [0]
    for s in sizes:
        offs.append(offs[-1] + s)
    qa, ka, va, qi, ki, wi, qb, kb, vb, fb = (w_in[:, offs[k]:offs[k + 1]] for k in range(10))
    qa = qa.reshape(d, N_HEADS_A, HEAD_DIM)[:, jnp.array(_QA_HEAD_ORDER)].reshape(d, -1)
    pad = jnp.zeros((d, LANES - N_IDX_HEADS - N_HEADS_B), w_in.dtype)
    w = jnp.concatenate([qa, ka, va, qi, ki, ki, wi, fb, pad, qb, kb, vb], axis=1)
    assert w.shape[1] == _C_END
    return w.astype(MXU_DTYPE)


def _rope_tables(pos):
    half = ROT_DIM // 2
    inv_freq = ROPE_THETA ** (-jnp.arange(half, dtype=jnp.float32) * 2.0 / ROT_DIM)
    ang = pos.astype(jnp.float32)[:, None] * inv_freq[None, :]
    cos, sin = jnp.cos(ang), jnp.sin(ang)
    ones = jnp.ones((pos.shape[0], HEAD_DIM - ROT_DIM), jnp.float32)
    cos64 = jnp.concatenate([cos, cos, ones], axis=1)
    sin64 = jnp.concatenate([-sin, sin, jnp.zeros_like(ones)], axis=1)
    return jnp.tile(cos64, (1, LANES // HEAD_DIM)), jnp.tile(sin64, (1, LANES // HEAD_DIM))


def _project_tokens(h, w_in_p, bf_p, pos, rows_per_seq):
    n = h.shape[0]
    tm = _row_tile(n)
    cos, sin = _rope_tables(pos)
    if tm <= rows_per_seq:
        assert rows_per_seq % tm == 0
        tab_blocks = rows_per_seq // tm
    else:
        assert tm % rows_per_seq == 0
        cos = jnp.tile(cos, (tm // rows_per_seq, 1))
        sin = jnp.tile(sin, (tm // rows_per_seq, 1))
        tab_blocks = 1
    return _project(h, w_in_p, bf_p, cos, sin, tm, tab_blocks)


def _blocked_cum(logf_keys, kb):
    b, s, h = logf_keys.shape
    cum = _cumsum_keys(jnp.transpose(logf_keys, (0, 2, 1)))
    return jnp.transpose(cum.reshape(b, h, s // kb, kb), (0, 2, 1, 3))


def _pad_keys(x, s_pad):
    return jnp.pad(x, ((0, 0), (0, s_pad - x.shape[1]), (0, 0)))


def kernel(x_prompt, x_sample, cache_k_a, cache_v_a, cache_kidx_a, cache_k_b, cache_v_b, cache_logf_b,
           w_in, b_f, w_out, ln1_g, ln1_b, ffn1_w_gate, ffn1_w_up, ffn1_w_down,
           ln2_g, ln2_b, ln3_g, ln3_b, ffn2_w_gate, ffn2_w_up, ffn2_w_down):
    assert w_in.shape[0] == DEPTH
    bsz, seq, d = x_prompt.shape
    dbs, dseq, _ = x_sample.shape
    past = cache_k_a.shape[2]
    bf16 = MXU_DTYPE

    w_in_p = _prepare_w_in(w_in[0])
    bf_p = jnp.zeros((1, LANES), jnp.float32).at[0, N_IDX_HEADS:N_IDX_HEADS + N_HEADS_B].set(b_f[0])
    order = jnp.array(_QA_HEAD_ORDER)
    w_out_a = w_out[0, :N_HEADS_A * HEAD_DIM].reshape(N_HEADS_A, HEAD_DIM, d)[order].reshape(-1, d).astype(bf16)
    w_out_b = w_out[0, N_HEADS_A * HEAD_DIM:].astype(bf16)
    ffn1 = (ffn1_w_gate[0].astype(bf16), ffn1_w_up[0].astype(bf16), ffn1_w_down[0].astype(bf16))
    ffn2 = (ffn2_w_gate[0].astype(bf16), ffn2_w_up[0].astype(bf16), ffn2_w_down[0].astype(bf16))
    vec = lambda a: a[0].reshape(1, d)

    def layer(x, pos, rows_per_seq, attend):
        n = x.shape[0] * x.shape[1]
        h = _ffn_ln(x.reshape(n, d), *ffn1, vec(ln1_g), vec(ln1_b))
        proj = _project_tokens(h, w_in_p, bf_p, pos, rows_per_seq)
        oa, ob = attend(proj)
        y = _post(h, oa, ob, w_out_a, w_out_b, vec(ln2_g), vec(ln2_b), *ffn2, vec(ln3_g), vec(ln3_b))
        return y.reshape(x.shape), proj

    tq_p = min(Q_TILE, seq)
    kb_p = min(KEY_BLOCK, seq)

    def attend_prompt(proj):
        qa, _, _, qi, _, wi, qb, _, _, logf, kab, vab, kib, kbb, vbb = proj
        ck = _blocked_cum(logf.reshape(bsz, seq, N_HEADS_B), kb_p)
        oa = _dsa(qi, wi, qa, kib, kab, vab, batch=bsz, tq=tq_p, kb=kb_p, q_off=0,
                  k_sel=min(TOPK_MAX, seq // 4))
        ob = _fox(qb, kbb, vbb, ck, batch=bsz, tq=tq_p, kb=kb_p, q_off=0)
        return oa, ob

    y_p, proj_p = layer(x_prompt, jnp.arange(seq, dtype=jnp.int32), seq, attend_prompt)

    total = past + dseq
    kb_s = min(SAMPLE_KEY_BLOCK, past)
    s_pad = -(-total // kb_s) * kb_s

    def attend_sample(proj):
        qa, _, _, qi, _, wi, qb, _, _, logf, kab, vab, kib, kbb, vbb = proj
        new = lambda a: a.reshape(dbs, dseq, -1)
        keys = lambda c, nw: lax.dynamic_update_slice(
            _pad_keys(c.reshape(dbs, past, -1).astype(bf16), s_pad), new(nw), (0, past, 0)
        ).reshape(dbs * s_pad, -1)
        kidx2 = jnp.concatenate([cache_kidx_a[0], cache_kidx_a[0]], axis=-1)
        logf_all = _pad_keys(jnp.concatenate([cache_logf_b[0].astype(jnp.float32), new(logf)], axis=1), s_pad)
        ck = _blocked_cum(logf_all, kb_s)
        oa = _dsa(qi, wi, qa, keys(kidx2, kib), keys(cache_k_a[0], kab), keys(cache_v_a[0], vab),
                  batch=dbs, tq=dseq, kb=kb_s, q_off=past, k_sel=min(TOPK_MAX, total // 4))
        ob = _fox(qb, keys(cache_k_b[0], kbb), keys(cache_v_b[0], vbb), ck,
                  batch=dbs, tq=dseq, kb=kb_s, q_off=past)
        return oa, ob

    y_s, proj_s = layer(x_sample, past + jnp.arange(dseq, dtype=jnp.int32), dseq, attend_sample)

    def rows(proj, b, t):
        _, ka, va, _, kidx, _, _, kbv, vbv, logf = proj[:10]
        return (ka.reshape(1, b, t, N_KV_A, HEAD_DIM), va.reshape(1, b, t, N_KV_A, HEAD_DIM),
                kidx.reshape(1, b, t, IDX_DIM), kbv.reshape(1, b, t, N_HEADS_B, HEAD_DIM),
                vbv.reshape(1, b, t, N_HEADS_B, HEAD_DIM), logf.reshape(1, b, t, N_HEADS_B))

    return (y_p, y_s) + rows(proj_p, bsz, seq) + rows(proj_s, dbs, dseq)
```

```python
import functools
import math

import jax
import jax.numpy as jnp
from jax import lax
from jax.experimental import pallas as pl
from jax.experimental.pallas import tpu as pltpu

CHUNK = 64
_CHUNK_SHIFT = 6
HEAD_DIM = 64
N_HEADS_A = 8
N_KV_A = 2
N_IDX_HEADS = 8
IDX_DIM = 64
TOPK_MAX = 256
N_HEADS_B = 8
ROT_DIM = HEAD_DIM // 4
ROPE_THETA = 500000.0
LN_EPS = 1e-5
DEPTH = 1
ALPHA = (2.0 * DEPTH) ** 0.25

MXU_DTYPE = jnp.bfloat16

LANES = 128
V7X_VMEM_LIMIT_BYTES = 60 * 1024 * 1024

ROW_TILE = 512
FF_CHUNK = 256
Q_TILE = 512
KEY_BLOCK = 512
SAMPLE_KEY_BLOCK = 512
ROW_CHUNK = 64

INT_MIN = -(2 ** 31)
FIELD_BITS = 15
FIELD_GUARD = 1 << FIELD_BITS
FIELD_FLAGS = 0x00010001
NEG_INF = float("-inf")
F32_LOWEST = float(jnp.finfo(jnp.float32).min)
LOG2E = math.log2(math.e)

_C_QA, _C_KA, _C_VA, _C_QI, _C_KI, _C_WF, _C_QB, _C_KB, _C_VB, _C_END = (
    0, 512, 640, 768, 1280, 1408, 1536, 2048, 2560, 3072)
_QA_HEAD_ORDER = (0, 4, 1, 5, 2, 6, 3, 7)


def _nt_dot(a, b):
    return lax.dot_general(a, b, (((1,), (1,)), ((), ())), preferred_element_type=jnp.float32)


def _dot(a, b):
    return jnp.dot(a, b, preferred_element_type=jnp.float32)


def _layer_norm(x, g, b):
    mu = jnp.mean(x, axis=-1, keepdims=True)
    xc = x - mu
    var = jnp.mean(xc * xc, axis=-1, keepdims=True)
    return xc * lax.rsqrt(var + LN_EPS) * g + b


def _macaron_half(x, wg_ref, wu_ref, wd_ref, g, b, act_ref):
    xb = x.astype(MXU_DTYPE)
    d_ff = wg_ref.shape[1]
    for c in range(d_ff // FF_CHUNK):
        cols = slice(c * FF_CHUNK, (c + 1) * FF_CHUNK)
        gate = _dot(xb, wg_ref[:, cols])
        up = _dot(xb, wu_ref[:, cols])
        act_ref[:, cols] = (gate * jax.nn.sigmoid(gate) * up).astype(MXU_DTYPE)
    down = _dot(act_ref[...], wd_ref[...])
    return _layer_norm(ALPHA * x + 0.5 * down, g, b)


def _ffn_ln_kernel(x_ref, wg_ref, wu_ref, wd_ref, g_ref, b_ref, o_ref, act_ref):
    o_ref[...] = _macaron_half(x_ref[...], wg_ref, wu_ref, wd_ref, g_ref[...], b_ref[...], act_ref)


def _const_spec(shape):
    return pl.BlockSpec(shape, lambda *_: (0,) * len(shape), pipeline_mode=pl.Buffered(1))


def _row_tile(n):
    tm = ROW_TILE
    while n % tm:
        tm //= 2
    return tm


def _ffn_ln(x, wg, wu, wd, g, b):
    n, d = x.shape
    d_ff = wg.shape[1]
    tm = _row_tile(n)
    return pl.pallas_call(
        _ffn_ln_kernel,
        grid=(n // tm,),
        in_specs=[
            pl.BlockSpec((tm, d), lambda i: (i, 0)),
            _const_spec((d, d_ff)), _const_spec((d, d_ff)), _const_spec((d_ff, d)),
            _const_spec((1, d)), _const_spec((1, d)),
        ],
        out_specs=pl.BlockSpec((tm, d), lambda i: (i, 0)),
        out_shape=jax.ShapeDtypeStruct((n, d), jnp.float32),
        scratch_shapes=[pltpu.VMEM((tm, d_ff), MXU_DTYPE)],
        compiler_params=pltpu.CompilerParams(
            dimension_semantics=("arbitrary",), vmem_limit_bytes=V7X_VMEM_LIMIT_BYTES),
        name="ffn_ln",
    )(x, wg, wu, wd, g, b)


def _rope(x, cos, sin):
    lane = lax.broadcasted_iota(jnp.int32, x.shape, 1) & (HEAD_DIM - 1)
    partner = jnp.where(lane < ROT_DIM // 2,
                        pltpu.roll(x, LANES - ROT_DIM // 2, 1),
                        pltpu.roll(x, ROT_DIM // 2, 1))
    return x * cos + partner * sin


def _project_kernel(h_ref, w_ref, bf_ref, cos_ref, sin_ref,
                    qa_ref, ka_ref, va_ref, qi_ref, kidx_ref, wi_ref, qb_ref, kb_ref, vb_ref, logf_ref,
                    kab_ref, vab_ref, kib_ref, kbb_ref, vbb_ref):
    hb = h_ref[...].astype(MXU_DTYPE)
    cos = cos_ref[...]
    sin = sin_ref[...]
    q_scale = HEAD_DIM ** -0.5 * LOG2E

    def part(lo, hi):
        return _dot(hb, w_ref[:, lo:hi])

    def rope_groups(z):
        return jnp.concatenate(
            [_rope(z[:, g * LANES:(g + 1) * LANES], cos, sin) for g in range(z.shape[1] // LANES)], axis=1)

    qa_ref[...] = (rope_groups(part(_C_QA, _C_KA)) * q_scale).astype(MXU_DTYPE)
    def heads(z):
        return z.reshape(z.shape[0], z.shape[1] // HEAD_DIM, HEAD_DIM)

    ka = _rope(part(_C_KA, _C_VA), cos, sin)
    ka_ref[...] = heads(ka)
    kab_ref[...] = ka.astype(MXU_DTYPE)
    va = part(_C_VA, _C_QI)
    va_ref[...] = heads(va)
    vab_ref[...] = va.astype(MXU_DTYPE)
    qi_ref[...] = (rope_groups(part(_C_QI, _C_KI)) * (IDX_DIM ** -0.5)).astype(MXU_DTYPE)
    ki = _rope(part(_C_KI, _C_WF), cos, sin)
    kidx_ref[...] = ki[:, :IDX_DIM]
    kib_ref[...] = ki.astype(MXU_DTYPE)
    wf = part(_C_WF, _C_QB)
    wi_ref[...] = wf[:, :N_IDX_HEADS] * (N_IDX_HEADS ** -0.5)
    f = wf + bf_ref[...]
    logf = -(jnp.maximum(-f, 0.0) + jnp.log1p(jnp.exp(-jnp.abs(f))))
    logf_ref[...] = pltpu.roll(logf, LANES - N_IDX_HEADS, 1)[:, :N_HEADS_B]
    qb_ref[...] = (part(_C_QB, _C_KB) * q_scale).astype(MXU_DTYPE)
    kb = part(_C_KB, _C_VB)
    kb_ref[...] = heads(kb)
    kbb_ref[...] = kb.astype(MXU_DTYPE)
    vb = part(_C_VB, _C_END)
    vb_ref[...] = heads(vb)
    vbb_ref[...] = vb.astype(MXU_DTYPE)


def _project(h, w, bf, cos, sin, tm, tab_blocks):
    n, d = h.shape
    f32, bf16 = jnp.float32, MXU_DTYPE
    kv_a, kv_b = (N_KV_A, HEAD_DIM), (N_HEADS_B, HEAD_DIM)
    tails = [((512,), bf16), (kv_a, f32), (kv_a, f32), ((512,), bf16), ((IDX_DIM,), f32), ((N_IDX_HEADS,), f32),
             ((512,), bf16), (kv_b, f32), (kv_b, f32), ((N_HEADS_B,), f32),
             ((128,), bf16), ((128,), bf16), ((128,), bf16), ((512,), bf16), ((512,), bf16)]
    row = lambda *tail: pl.BlockSpec((tm,) + tail, lambda i: (i,) + (0,) * len(tail))
    tab = pl.BlockSpec((tm, LANES), lambda i: (i % tab_blocks, 0))
    return pl.pallas_call(
        _project_kernel,
        grid=(n // tm,),
        in_specs=[row(d), _const_spec(w.shape), _const_spec((1, LANES)), tab, tab],
        out_specs=[row(*tail) for tail, _ in tails],
        out_shape=[jax.ShapeDtypeStruct((n,) + tail, dt) for tail, dt in tails],
        compiler_params=pltpu.CompilerParams(
            dimension_semantics=("arbitrary",), vmem_limit_bytes=V7X_VMEM_LIMIT_BYTES),
        name="project",
    )(h, w, bf, cos, sin)


def _cumsum_kernel(x_ref, o_ref, *, seg):
    r = lax.broadcasted_iota(jnp.int32, (seg, seg), 0)
    c = lax.broadcasted_iota(jnp.int32, (seg, seg), 1)
    tri = (r <= c).astype(jnp.float32)
    carry = jnp.zeros((x_ref.shape[1], 1), jnp.float32)
    for s in range(x_ref.shape[2] // seg):
        cols = slice(s * seg, (s + 1) * seg)
        part = jnp.dot(x_ref[0, :, cols], tri, preferred_element_type=jnp.float32,
                       precision=lax.Precision.HIGHEST) + carry
        o_ref[0, :, cols] = part * LOG2E
        carry = part[:, seg - 1:seg]


def _cumsum_keys(x):
    b, h, s = x.shape
    seg = 2 * LANES if s % (2 * LANES) == 0 else LANES
    assert s % seg == 0
    return pl.pallas_call(
        functools.partial(_cumsum_kernel, seg=seg),
        grid=(b,),
        in_specs=[pl.BlockSpec((1, h, s), lambda i: (i, 0, 0))],
        out_specs=pl.BlockSpec((1, h, s), lambda i: (i, 0, 0)),
        out_shape=jax.ShapeDtypeStruct((b, h, s), jnp.float32),
        compiler_params=pltpu.CompilerParams(dimension_semantics=("arbitrary",)),
        name="cumsum_keys",
    )(x)


def _num_key_blocks(i, tq, kb, q_off):
    return (q_off + (i + 1) * tq + kb - 1) // kb


def _lane_groups(x):
    return [x[:, g * LANES:(g + 1) * LANES] for g in range(x.shape[1] // LANES)]


def _lane_tile(x, width):
    return jnp.concatenate([x] * (width // LANES), axis=1)


def _softmax_step(s, h, m_ref, l_ref, acc_ref, v_ones):
    groups = _lane_groups(s)
    smax = functools.reduce(jnp.maximum, groups)
    m_old = m_ref[h]
    m_new = jnp.maximum(m_old, jnp.max(smax, axis=-1, keepdims=True))
    m_safe = jnp.maximum(m_new, F32_LOWEST)
    corr = jnp.exp2(m_old - m_safe)
    p = jnp.concatenate([jnp.exp2(g - m_safe) for g in groups], axis=1).astype(MXU_DTYPE)
    pv = _dot(p, v_ones)
    acc_ref[h] = corr * acc_ref[h] + pv[:, :LANES]
    l_ref[h] = corr * l_ref[h] + pv[:, LANES:]
    m_ref[h] = m_new


def _store_values_and_ones(v_ref, lanes, vo_ref):
    vo_ref[:, :LANES] = v_ref[:, lanes]
    vo_ref[:, LANES:] = jnp.ones((vo_ref.shape[0], LANES), vo_ref.dtype)


def _store_half_masks(q_ref, qm_ref):
    for p in range(q_ref.shape[1] // LANES):
        q2 = q_ref[:, p * LANES:(p + 1) * LANES]
        lane = lax.broadcasted_iota(jnp.int32, q2.shape, 1)
        zero = jnp.zeros_like(q2)
        qm_ref[2 * p] = jnp.where(lane < HEAD_DIM, q2, zero)
        qm_ref[2 * p + 1] = jnp.where(lane >= HEAD_DIM, q2, zero)


def _init_state(m_ref, l_ref, acc_ref):
    m_ref[...] = jnp.full(m_ref.shape, NEG_INF, jnp.float32)
    l_ref[...] = jnp.zeros(l_ref.shape, jnp.float32)
    acc_ref[...] = jnp.zeros(acc_ref.shape, jnp.float32)


def _write_pairs(o_ref, l_ref, acc_ref):
    for p in range(o_ref.shape[1] // LANES):
        outs = [acc_ref[h] / l_ref[h] for h in (2 * p, 2 * p + 1)]
        lane = lax.broadcasted_iota(jnp.int32, outs[0].shape, 1)
        o_ref[:, p * LANES:(p + 1) * LANES] = jnp.where(lane < HEAD_DIM, outs[0], outs[1]).astype(o_ref.dtype)


def _attention_scratch(tq, n_heads):
    return [pltpu.VMEM((n_heads, tq, LANES), MXU_DTYPE),
            pltpu.VMEM((n_heads, tq, LANES), jnp.float32),
            pltpu.VMEM((n_heads, tq, LANES), jnp.float32),
            pltpu.VMEM((n_heads, tq, LANES), jnp.float32)]


def _join_keys(cache_ref, new_ref, dst_ref, repeat=1):
    past, n_new, s_len = cache_ref.shape[0], new_ref.shape[0], dst_ref.shape[0]
    for r0 in range(0, past, ROW_TILE):
        rows = slice(r0, min(r0 + ROW_TILE, past))
        x = cache_ref[rows, :]
        dst_ref[rows, :] = x if repeat == 1 else jnp.concatenate([x] * repeat, axis=1)
    dst_ref[past:past + n_new, :] = new_ref[...]
    dst_ref[past + n_new:, :] = jnp.zeros((s_len - past - n_new, dst_ref.shape[1]), dst_ref.dtype)


def _fox_kernel(*refs, tq, kb, q_off, cached):
    if cached:
        q_ref, kc_ref, vc_ref, kn_ref, vn_ref, ck_ref, o_ref, k_ref, v_ref, vo_ref, qm_ref, m_ref, l_ref, acc_ref = refs
    else:
        q_ref, k_ref, v_ref, ck_ref, o_ref, vo_ref, qm_ref, m_ref, l_ref, acc_ref = refs
    i = pl.program_id(1)
    nb = _num_key_blocks(i, tq, kb, q_off)
    assert kb % tq == 0 and q_off % tq == 0

    @pl.when(i == 0)
    def _():
        if cached:
            _join_keys(kc_ref, kn_ref, k_ref)
            _join_keys(vc_ref, vn_ref, v_ref)
        for p in range(N_HEADS_B // 2):
            _store_values_and_ones(v_ref, slice(p * LANES, (p + 1) * LANES), vo_ref.at[p])

    _store_half_masks(q_ref, qm_ref)
    _init_state(m_ref, l_ref, acc_ref)

    def block(j, masked):
        start = pl.multiple_of(j * kb, kb)
        ck = ck_ref[0, j]
        if masked:
            row = q_off + i * tq + lax.broadcasted_iota(jnp.int32, (tq, kb), 0)
            col = j * kb + lax.broadcasted_iota(jnp.int32, (tq, kb), 1)
            causal = col <= row
        for h in range(N_HEADS_B):
            lanes = slice((h // 2) * LANES, (h // 2 + 1) * LANES)
            s = _nt_dot(qm_ref[h], k_ref[pl.ds(start, kb), lanes]) - ck[h:h + 1, :]
            if masked:
                s = jnp.where(causal, s, NEG_INF)
            _softmax_step(s, h, m_ref, l_ref, acc_ref, vo_ref[h // 2, pl.ds(start, kb), :])

    def full_block(j, _):
        block(j, masked=False)
        return 0

    lax.fori_loop(0, nb - 1, full_block, 0)
    block(nb - 1, masked=True)
    _write_pairs(o_ref, l_ref, acc_ref)


def _key_specs(keys, new_keys, batch):
    per_batch = lambda a: pl.BlockSpec((a.shape[0] // batch, a.shape[1]), lambda b, i: (b, 0))
    return [per_batch(a) for a in keys] + [per_batch(a) for a in new_keys]


def _fox(q, k, v, ck, *, batch, tq, kb, q_off, new_kv=()):
    n, width = q.shape
    t_q = n // batch
    s_len = ck.shape[1] * kb
    nq = t_q // tq
    cached = bool(new_kv)
    joined = [pltpu.VMEM((s_len, width), MXU_DTYPE)] * 2 if cached else []
    return pl.pallas_call(
        functools.partial(_fox_kernel, tq=tq, kb=kb, q_off=q_off, cached=cached),
        grid=(batch, nq),
        in_specs=[pl.BlockSpec((tq, width), lambda b, i: (b * nq + i, 0))]
        + _key_specs((k, v), new_kv, batch)
        + [pl.BlockSpec((1,) + ck.shape[1:], lambda b, i: (b, 0, 0, 0))],
        out_specs=pl.BlockSpec((tq, width), lambda b, i: (b * nq + i, 0)),
        out_shape=jax.ShapeDtypeStruct((n, width), MXU_DTYPE),
        scratch_shapes=joined
        + [pltpu.VMEM((N_HEADS_B // 2, s_len, 2 * LANES), MXU_DTYPE)]
        + _attention_scratch(tq, N_HEADS_B),
        compiler_params=pltpu.CompilerParams(
            dimension_semantics=("arbitrary", "arbitrary"), vmem_limit_bytes=V7X_VMEM_LIMIT_BYTES),
        name="fox_attention",
    )(q, k, v, *new_kv, ck)


def _dsa_kernel(*refs, tq, kb, q_off, k_sel, cached):
    qi_ref, wi_ref, qa_ref = refs[:3]
    if cached:
        kic_ref, kac_ref, vac_ref, kin_ref, kan_ref, van_ref, o_ref, ki_ref, ka_ref, va_ref = refs[3:13]
        scratch = refs[13:]
    else:
        ki_ref, ka_ref, va_ref, o_ref = refs[3:7]
        scratch = refs[7:]
    (key_ref, word_ref, bias_ref, thr_ref, cand_ref, lim_ref, cnt_ref, vo_ref,
     qm_ref, m_ref, l_ref, acc_ref) = scratch
    i = pl.program_id(1)

    @pl.when(i == 0)
    def _():
        if cached:
            _join_keys(kic_ref, kin_ref, ki_ref, repeat=LANES // IDX_DIM)
            _join_keys(kac_ref, kan_ref, ka_ref)
            _join_keys(vac_ref, van_ref, va_ref)
        _store_values_and_ones(va_ref, slice(0, LANES), vo_ref)

    nb = _num_key_blocks(i, tq, kb, q_off)
    k_f = jnp.float32(k_sel)
    rc = min(ROW_CHUNK, tq)
    chunks = [slice(r * rc, (r + 1) * rc) for r in range(tq // rc)]
    groups = [slice(g * LANES, (g + 1) * LANES) for g in range(kb // LANES)]

    def replicate(x):
        return jnp.broadcast_to(x, (tq, LANES))

    def pack_fields(f):
        return (f[:, :kb // 2] << 16) | f[:, kb // 2:]

    assert kb % (2 * LANES) == 0
    _store_half_masks(qi_ref, qm_ref)
    wi = wi_ref[...]

    def score_block(j, _):
        start = pl.multiple_of(j * kb, kb)
        kblk = ki_ref[pl.ds(start, kb), :]
        score = jnp.zeros((tq, kb), jnp.float32)
        for h in range(N_IDX_HEADS):
            score = score + wi[:, h:h + 1] * jnp.maximum(_nt_dot(qm_ref[h], kblk), 0.0)
        bits = lax.bitcast_convert_type(score, jnp.int32)
        key = jnp.where(bits < 0, bits ^ jnp.int32(0x7FFFFFFF), bits)
        row = q_off + i * tq + lax.broadcasted_iota(jnp.int32, (tq, kb), 0)
        col = j * kb + lax.broadcasted_iota(jnp.int32, (tq, kb), 1)
        admissible = (col >> _CHUNK_SHIFT) <= (row >> _CHUNK_SHIFT)
        key = jnp.where(admissible, key, jnp.int32(INT_MIN))
        key_ref[j] = key
        word_ref[j] = jnp.transpose(pack_fields((key >> (32 - FIELD_BITS)) + (FIELD_GUARD + FIELD_GUARD // 2)))
        return 0

    lax.fori_loop(0, nb, score_block, 0)

    def count(pred):
        def body(j, acc):
            parts = []
            for rows in chunks:
                hits = [jnp.where(pred(key_ref[j, rows, g], rows, j * kb + g.start), 1.0, 0.0) for g in groups]
                parts.append(functools.reduce(jnp.add, hits))
            return acc + jnp.concatenate(parts, axis=0)
        acc = lax.fori_loop(0, nb, body, jnp.zeros((tq, LANES), jnp.float32))
        return replicate(jnp.sum(acc, axis=-1, keepdims=True))

    def count_fields_at_least(cand):
        both = (cand << 16) | cand

        def body(j, acc):
            for r0 in range(0, kb // 2, ROW_CHUNK):
                words = word_ref[j, r0:r0 + ROW_CHUNK, :]
                flags = lax.shift_right_logical(words - both, FIELD_BITS) & FIELD_FLAGS
                acc = acc + jnp.sum(flags.reshape(ROW_CHUNK // 8, 8, tq), axis=0)
            return acc
        acc = lax.fori_loop(0, nb, body, jnp.zeros((8, tq), jnp.int32))
        return jnp.sum((acc & 0xFFFF) + lax.shift_right_logical(acc, 16), axis=0, keepdims=True)

    def bisect_fields(cnt):
        def step(b, carry):
            thr, cnt = carry
            cand = thr | (jnp.int32(1) << (jnp.int32(FIELD_BITS - 1) - b))
            c = count_fields_at_least(cand)
            take = c >= k_sel
            return jnp.where(take, cand, thr), jnp.where(take, c, cnt)

        return lax.fori_loop(0, FIELD_BITS, step, (jnp.zeros((1, tq), jnp.int32), cnt))

    def per_row(row):
        return jnp.transpose(jnp.broadcast_to(row, (LANES, tq)))

    top, cnt = bisect_fields(jnp.full((1, tq), k_sel, jnp.int32))
    lim_ref[...] = per_row(top << FIELD_BITS)

    def middle_fields(j, _):
        words = []
        for rows in chunks:
            base = lim_ref[rows, :] - (FIELD_GUARD + (1 << 29))
            mid = [jnp.clip((key_ref[j, rows, g] >> 2) - base, FIELD_GUARD, 2 * FIELD_GUARD - 1) for g in groups]
            words.append(pack_fields(jnp.concatenate(mid, axis=1)))
        word_ref[j] = jnp.transpose(jnp.concatenate(words, axis=0))
        return 0

    lax.fori_loop(0, nb, middle_fields, 0)
    middle, cnt = bisect_fields(cnt)
    prefix = (top << FIELD_BITS) | middle
    thr_ref[...] = per_row((prefix << 2) ^ jnp.int32(INT_MIN))
    cnt_ref[...] = per_row(cnt).astype(jnp.float32)

    def low_bit(b, _):
        cand = thr_ref[...] | (jnp.int32(2) >> b)
        cand_ref[...] = cand
        c = count(lambda key, rows, first: key >= cand_ref[rows, :])
        take = c >= k_f
        thr_ref[...] = jnp.where(take, cand, thr_ref[...])
        cnt_ref[...] = jnp.where(take, c, cnt_ref[...])
        return 0

    lax.fori_loop(0, 2, low_bit, 0)

    thr = thr_ref[...]
    has_thr = thr > jnp.int32(INT_MIN)
    surplus = jnp.logical_and(has_thr, cnt_ref[...] > k_f)

    def bias_without_surplus():
        cand_ref[...] = jnp.maximum(thr_ref[...], jnp.int32(INT_MIN + 1))

        def block(j, _):
            for rows in chunks:
                for g in groups:
                    bias_ref[j, rows, g] = jnp.where(key_ref[j, rows, g] >= cand_ref[rows, :], 0.0, NEG_INF)
            return 0

        lax.fori_loop(0, nb, block, 0)

    def bias_with_surplus():
        above = count(lambda key, rows, first: key > thr_ref[rows, :])
        cnt_ref[...] = jnp.where(has_thr, k_f - above, 0.0)
        r = lax.broadcasted_iota(jnp.int32, (kb, kb), 0)
        c = lax.broadcasted_iota(jnp.int32, (kb, kb), 1)
        tri = (r <= c).astype(MXU_DTYPE)

        def block(j, before):
            key = key_ref[j]
            tied = key == _lane_tile(thr_ref[...], kb)
            rank = _dot(jnp.where(tied, 1.0, 0.0).astype(MXU_DTYPE), tri)
            take = jnp.logical_and(tied, rank + _lane_tile(before, kb) <= _lane_tile(cnt_ref[...], kb))
            sel = jnp.logical_or(key > _lane_tile(thr_ref[...], kb), take)
            bias_ref[j] = jnp.where(sel, 0.0, NEG_INF)
            return before + replicate(jnp.max(rank, axis=-1, keepdims=True))

        lax.fori_loop(0, nb, block, jnp.zeros((tq, LANES), jnp.float32))

    any_surplus = jnp.max(jnp.where(surplus, 1.0, 0.0)) > 0.0
    lax.cond(any_surplus, bias_with_surplus, bias_without_surplus)

    _store_half_masks(qa_ref, qm_ref)
    _init_state(m_ref, l_ref, acc_ref)

    def attend(j, _):
        start = pl.multiple_of(j * kb, kb)
        for h in range(N_HEADS_A):
            s = _nt_dot(qm_ref[h], ka_ref[pl.ds(start, kb), :]) + bias_ref[j]
            _softmax_step(s, h, m_ref, l_ref, acc_ref, vo_ref[pl.ds(start, kb), :])
        return 0

    lax.fori_loop(0, nb, attend, 0)
    _write_pairs(o_ref, l_ref, acc_ref)


def _dsa(qi, wi, qa, ki, ka, va, *, batch, tq, kb, q_off, k_sel, s_len, new_keys=()):
    n, width = qa.shape
    t_q = n // batch
    nq = t_q // tq
    nblk = s_len // kb
    cached = bool(new_keys)
    qspec = lambda wdt: pl.BlockSpec((tq, wdt), lambda b, i: (b * nq + i, 0))
    joined = [pltpu.VMEM((s_len, LANES), MXU_DTYPE)] * 3 if cached else []
    return pl.pallas_call(
        functools.partial(_dsa_kernel, tq=tq, kb=kb, q_off=q_off, k_sel=k_sel, cached=cached),
        grid=(batch, nq),
        in_specs=[qspec(width), qspec(N_IDX_HEADS), qspec(width)] + _key_specs((ki, ka, va), new_keys, batch),
        out_specs=qspec(width),
        out_shape=jax.ShapeDtypeStruct((n, width), MXU_DTYPE),
        scratch_shapes=joined + [
                        pltpu.VMEM((nblk, tq, kb), jnp.int32),
                        pltpu.VMEM((nblk, kb // 2, tq), jnp.int32),
                        pltpu.VMEM((nblk, tq, kb), jnp.float32),
                        pltpu.VMEM((tq, LANES), jnp.int32),
                        pltpu.VMEM((tq, LANES), jnp.int32),
                        pltpu.VMEM((tq, LANES), jnp.int32),
                        pltpu.VMEM((tq, LANES), jnp.float32),
                        pltpu.VMEM((s_len, 2 * LANES), MXU_DTYPE),
                        ] + _attention_scratch(tq, N_HEADS_A),
        compiler_params=pltpu.CompilerParams(
            dimension_semantics=("arbitrary", "arbitrary"), vmem_limit_bytes=V7X_VMEM_LIMIT_BYTES),
        name="dsa_attention",
    )(qi, wi, qa, ki, ka, va, *new_keys)


def _post_kernel(h_ref, oa_ref, ob_ref, woa_ref, wob_ref, g2_ref, b2_ref,
                 wg_ref, wu_ref, wd_ref, g3_ref, b3_ref, o_ref, act_ref):
    mix = _dot(oa_ref[...], woa_ref[...]) + _dot(ob_ref[...], wob_ref[...])
    h2 = _layer_norm(ALPHA * h_ref[...] + mix, g2_ref[...], b2_ref[...])
    o_ref[...] = _macaron_half(h2, wg_ref, wu_ref, wd_ref, g3_ref[...], b3_ref[...], act_ref)


def _post(h, oa, ob, woa, wob, g2, b2, wg, wu, wd, g3, b3):
    n, d = h.shape
    d_ff = wg.shape[1]
    tm = _row_tile(n)
    row = lambda wdt: pl.BlockSpec((tm, wdt), lambda i: (i, 0))
    return pl.pallas_call(
        _post_kernel,
        grid=(n // tm,),
        in_specs=[row(d), row(oa.shape[1]), row(ob.shape[1]),
                  _const_spec(woa.shape), _const_spec(wob.shape), _const_spec((1, d)), _const_spec((1, d)),
                  _const_spec((d, d_ff)), _const_spec((d, d_ff)), _const_spec((d_ff, d)),
                  _const_spec((1, d)), _const_spec((1, d))],
        out_specs=row(d),
        out_shape=jax.ShapeDtypeStruct((n, d), jnp.float32),
        scratch_shapes=[pltpu.VMEM((tm, d_ff), MXU_DTYPE)],
        compiler_params=pltpu.CompilerParams(
            dimension_semantics=("arbitrary",), vmem_limit_bytes=V7X_VMEM_LIMIT_BYTES),
        name="post",
    )(h, oa, ob, woa, wob, g2, b2, wg, wu, wd, g3, b3)


def _prepare_w_in(w_in):
    d = w_in.shape[0]
    sizes = (N_HEADS_A * HEAD_DIM, N_KV_A * HEAD_DIM, N_KV_A * HEAD_DIM, N_IDX_HEADS * IDX_DIM, IDX_DIM,
             N_IDX_HEADS, N_HEADS_B * HEAD_DIM, N_HEADS_B * HEAD_DIM, N_HEADS_B * HEAD_DIM, N_HEADS_B)
    offs = [0]
    for s in sizes:
        offs.append(offs[-1] + s)
    qa, ka, va, qi, ki, wi, qb, kb, vb, fb = (w_in[:, offs[k]:offs[k + 1]] for k in range(10))
    qa = qa.reshape(d, N_HEADS_A, HEAD_DIM)[:, jnp.array(_QA_HEAD_ORDER)].reshape(d, -1)
    pad = jnp.zeros((d, LANES - N_IDX_HEADS - N_HEADS_B), w_in.dtype)
    w = jnp.concatenate([qa, ka, va, qi, ki, ki, wi, fb, pad, qb, kb, vb], axis=1)
    assert w.shape[1] == _C_END
    return w.astype(MXU_DTYPE)


def _rope_tables(pos):
    half = ROT_DIM // 2
    inv_freq = ROPE_THETA ** (-jnp.arange(half, dtype=jnp.float32) * 2.0 / ROT_DIM)
    ang = pos.astype(jnp.float32)[:, None] * inv_freq[None, :]
    cos, sin = jnp.cos(ang), jnp.sin(ang)
    ones = jnp.ones((pos.shape[0], HEAD_DIM - ROT_DIM), jnp.float32)
    cos64 = jnp.concatenate([cos, cos, ones], axis=1)
    sin64 = jnp.concatenate([-sin, sin, jnp.zeros_like(ones)], axis=1)
    return jnp.tile(cos64, (1, LANES // HEAD_DIM)), jnp.tile(sin64, (1, LANES // HEAD_DIM))


def _project_tokens(h, w_in_p, bf_p, pos, rows_per_seq):
    n = h.shape[0]
    tm = _row_tile(n)
    cos, sin = _rope_tables(pos)
    if tm <= rows_per_seq:
        assert rows_per_seq % tm == 0
        tab_blocks = rows_per_seq // tm
    else:
        assert tm % rows_per_seq == 0
        cos = jnp.tile(cos, (tm // rows_per_seq, 1))
        sin = jnp.tile(sin, (tm // rows_per_seq, 1))
        tab_blocks = 1
    return _project(h, w_in_p, bf_p, cos, sin, tm, tab_blocks)


def _blocked_cum(logf_keys, kb):
    b, s, h = logf_keys.shape
    cum = _cumsum_keys(jnp.transpose(logf_keys, (0, 2, 1)))
    return jnp.transpose(cum.reshape(b, h, s // kb, kb), (0, 2, 1, 3))


def _pad_keys(x, s_pad):
    return jnp.pad(x, ((0, 0), (0, s_pad - x.shape[1]), (0, 0)))


def kernel(x_prompt, x_sample, cache_k_a, cache_v_a, cache_kidx_a, cache_k_b, cache_v_b, cache_logf_b,
           w_in, b_f, w_out, ln1_g, ln1_b, ffn1_w_gate, ffn1_w_up, ffn1_w_down,
           ln2_g, ln2_b, ln3_g, ln3_b, ffn2_w_gate, ffn2_w_up, ffn2_w_down):
    assert w_in.shape[0] == DEPTH
    bsz, seq, d = x_prompt.shape
    dbs, dseq, _ = x_sample.shape
    past = cache_k_a.shape[2]
    bf16 = MXU_DTYPE

    w_in_p = _prepare_w_in(w_in[0])
    bf_p = jnp.zeros((1, LANES), jnp.float32).at[0, N_IDX_HEADS:N_IDX_HEADS + N_HEADS_B].set(b_f[0])
    order = jnp.array(_QA_HEAD_ORDER)
    w_out_a = w_out[0, :N_HEADS_A * HEAD_DIM].reshape(N_HEADS_A, HEAD_DIM, d)[order].reshape(-1, d).astype(bf16)
    w_out_b = w_out[0, N_HEADS_A * HEAD_DIM:].astype(bf16)
    ffn1 = (ffn1_w_gate[0].astype(bf16), ffn1_w_up[0].astype(bf16), ffn1_w_down[0].astype(bf16))
    ffn2 = (ffn2_w_gate[0].astype(bf16), ffn2_w_up[0].astype(bf16), ffn2_w_down[0].astype(bf16))
    vec = lambda a: a[0].reshape(1, d)

    def layer(x, pos, rows_per_seq, attend):
        n = x.shape[0] * x.shape[1]
        h = _ffn_ln(x.reshape(n, d), *ffn1, vec(ln1_g), vec(ln1_b))
        proj = _project_tokens(h, w_in_p, bf_p, pos, rows_per_seq)
        oa, ob = attend(proj)
        y = _post(h, oa, ob, w_out_a, w_out_b, vec(ln2_g), vec(ln2_b), *ffn2, vec(ln3_g), vec(ln3_b))
        return y.reshape(x.shape), proj

    tq_p = min(Q_TILE, seq)
    kb_p = min(KEY_BLOCK, seq)

    def attend_prompt(proj):
        qa, _, _, qi, _, wi, qb, _, _, logf, kab, vab, kib, kbb, vbb = proj
        ck = _blocked_cum(logf.reshape(bsz, seq, N_HEADS_B), kb_p)
        oa = _dsa(qi, wi, qa, kib, kab, vab, batch=bsz, tq=tq_p, kb=kb_p, q_off=0,
                  k_sel=min(TOPK_MAX, seq // 4), s_len=seq)
        ob = _fox(qb, kbb, vbb, ck, batch=bsz, tq=tq_p, kb=kb_p, q_off=0)
        return oa, ob

    y_p, proj_p = layer(x_prompt, jnp.arange(seq, dtype=jnp.int32), seq, attend_prompt)

    total = past + dseq
    kb_s = min(SAMPLE_KEY_BLOCK, past)
    s_pad = -(-total // kb_s) * kb_s

    def attend_sample(proj):
        qa, _, _, qi, _, wi, qb, _, _, logf, kab, vab, kib, kbb, vbb = proj
        cached = lambda c: c[0].reshape(dbs * past, -1).astype(bf16)
        logf_all = _pad_keys(jnp.concatenate(
            [cache_logf_b[0].astype(jnp.float32), logf.reshape(dbs, dseq, -1)], axis=1), s_pad)
        ck = _blocked_cum(logf_all, kb_s)
        oa = _dsa(qi, wi, qa, cached(cache_kidx_a), cached(cache_k_a), cached(cache_v_a),
                  batch=dbs, tq=dseq, kb=kb_s, q_off=past, k_sel=min(TOPK_MAX, total // 4), s_len=s_pad,
                  new_keys=(kib, kab, vab))
        ob = _fox(qb, cached(cache_k_b), cached(cache_v_b), ck, batch=dbs, tq=dseq, kb=kb_s, q_off=past,
                  new_kv=(kbb, vbb))
        return oa, ob

    y_s, proj_s = layer(x_sample, past + jnp.arange(dseq, dtype=jnp.int32), dseq, attend_sample)

    def rows(proj, b, t):
        _, ka, va, _, kidx, _, _, kbv, vbv, logf = proj[:10]
        return (ka.reshape(1, b, t, N_KV_A, HEAD_DIM), va.reshape(1, b, t, N_KV_A, HEAD_DIM),
                kidx.reshape(1, b, t, IDX_DIM), kbv.reshape(1, b, t, N_HEADS_B, HEAD_DIM),
                vbv.reshape(1, b, t, N_HEADS_B, HEAD_DIM), logf.reshape(1, b, t, N_HEADS_B))

    return (y_p, y_s) + rows(proj_p, bsz, seq) + rows(proj_s, dbs, dseq)
```

```python
import functools
import math

import jax
import jax.numpy as jnp
from jax import lax
from jax.experimental import pallas as pl
from jax.experimental.pallas import tpu as pltpu

CHUNK = 64
_CHUNK_SHIFT = 6
HEAD_DIM = 64
N_HEADS_A = 8
N_KV_A = 2
N_IDX_HEADS = 8
IDX_DIM = 64
TOPK_MAX = 256
N_HEADS_B = 8
ROT_DIM = HEAD_DIM // 4
ROPE_THETA = 500000.0
LN_EPS = 1e-5
DEPTH = 1
ALPHA = (2.0 * DEPTH) ** 0.25

MXU_DTYPE = jnp.bfloat16

LANES = 128
V7X_VMEM_LIMIT_BYTES = 60 * 1024 * 1024

ROW_TILE = 512
FF_CHUNK = 256
Q_TILE = 512
KEY_BLOCK = 512
SAMPLE_KEY_BLOCK = 512
ROW_CHUNK = 64
STACK_ROWS = 512

INT_MIN = -(2 ** 31)
FIELD_BITS = 15
FIELD_GUARD = 1 << FIELD_BITS
FIELD_FLAGS = 0x00010001
NEG_INF = float("-inf")
F32_LOWEST = float(jnp.finfo(jnp.float32).min)
LOG2E = math.log2(math.e)

_C_QA, _C_KA, _C_VA, _C_QI, _C_KI, _C_WF, _C_QB, _C_KB, _C_VB, _C_END = (
    0, 512, 640, 768, 1280, 1408, 1536, 2048, 2560, 3072)
_QA_HEAD_ORDER = (0, 4, 1, 5, 2, 6, 3, 7)


def _nt_dot(a, b):
    return lax.dot_general(a, b, (((1,), (1,)), ((), ())), preferred_element_type=jnp.float32)


def _dot(a, b):
    return jnp.dot(a, b, preferred_element_type=jnp.float32)


def _layer_norm(x, g, b):
    mu = jnp.mean(x, axis=-1, keepdims=True)
    xc = x - mu
    var = jnp.mean(xc * xc, axis=-1, keepdims=True)
    return xc * lax.rsqrt(var + LN_EPS) * g + b


def _macaron_half(x, wg_ref, wu_ref, wd_ref, g, b, act_ref):
    xb = x.astype(MXU_DTYPE)
    d_ff = wg_ref.shape[1]
    for c in range(d_ff // FF_CHUNK):
        cols = slice(c * FF_CHUNK, (c + 1) * FF_CHUNK)
        gate = _dot(xb, wg_ref[:, cols])
        up = _dot(xb, wu_ref[:, cols])
        act_ref[:, cols] = (gate * jax.nn.sigmoid(gate) * up).astype(MXU_DTYPE)
    down = _dot(act_ref[...], wd_ref[...])
    return _layer_norm(ALPHA * x + 0.5 * down, g, b)


def _ffn_ln_kernel(x_ref, wg_ref, wu_ref, wd_ref, g_ref, b_ref, o_ref, act_ref):
    o_ref[...] = _macaron_half(x_ref[...], wg_ref, wu_ref, wd_ref, g_ref[...], b_ref[...], act_ref)


def _const_spec(shape):
    return pl.BlockSpec(shape, lambda *_: (0,) * len(shape), pipeline_mode=pl.Buffered(1))


def _row_tile(n):
    tm = ROW_TILE
    while n % tm:
        tm //= 2
    return tm


def _ffn_ln(x, wg, wu, wd, g, b):
    n, d = x.shape
    d_ff = wg.shape[1]
    tm = _row_tile(n)
    return pl.pallas_call(
        _ffn_ln_kernel,
        grid=(n // tm,),
        in_specs=[
            pl.BlockSpec((tm, d), lambda i: (i, 0)),
            _const_spec((d, d_ff)), _const_spec((d, d_ff)), _const_spec((d_ff, d)),
            _const_spec((1, d)), _const_spec((1, d)),
        ],
        out_specs=pl.BlockSpec((tm, d), lambda i: (i, 0)),
        out_shape=jax.ShapeDtypeStruct((n, d), jnp.float32),
        scratch_shapes=[pltpu.VMEM((tm, d_ff), MXU_DTYPE)],
        compiler_params=pltpu.CompilerParams(
            dimension_semantics=("arbitrary",), vmem_limit_bytes=V7X_VMEM_LIMIT_BYTES),
        name="ffn_ln",
    )(x, wg, wu, wd, g, b)


def _rope(x, cos, sin):
    lane = lax.broadcasted_iota(jnp.int32, x.shape, 1) & (HEAD_DIM - 1)
    partner = jnp.where(lane < ROT_DIM // 2,
                        pltpu.roll(x, LANES - ROT_DIM // 2, 1),
                        pltpu.roll(x, ROT_DIM // 2, 1))
    return x * cos + partner * sin


def _project_kernel(h_ref, w_ref, bf_ref, cos_ref, sin_ref,
                    qa_ref, ka_ref, va_ref, qi_ref, kidx_ref, wi_ref, qb_ref, kb_ref, vb_ref, logf_ref,
                    kab_ref, vab_ref, kib_ref, kbb_ref, vbb_ref):
    hb = h_ref[...].astype(MXU_DTYPE)
    cos = cos_ref[...]
    sin = sin_ref[...]
    q_scale = HEAD_DIM ** -0.5 * LOG2E

    def part(lo, hi):
        return _dot(hb, w_ref[:, lo:hi])

    def rope_groups(z):
        return jnp.concatenate(
            [_rope(z[:, g * LANES:(g + 1) * LANES], cos, sin) for g in range(z.shape[1] // LANES)], axis=1)

    qa_ref[...] = (rope_groups(part(_C_QA, _C_KA)) * q_scale).astype(MXU_DTYPE)
    def heads(z):
        return z.reshape(z.shape[0], z.shape[1] // HEAD_DIM, HEAD_DIM)

    ka = _rope(part(_C_KA, _C_VA), cos, sin)
    ka_ref[...] = heads(ka)
    kab_ref[...] = ka.astype(MXU_DTYPE)
    va = part(_C_VA, _C_QI)
    va_ref[...] = heads(va)
    vab_ref[...] = va.astype(MXU_DTYPE)
    qi_ref[...] = (rope_groups(part(_C_QI, _C_KI)) * (IDX_DIM ** -0.5)).astype(MXU_DTYPE)
    ki = _rope(part(_C_KI, _C_WF), cos, sin)
    kidx_ref[...] = ki[:, :IDX_DIM]
    kib_ref[...] = ki.astype(MXU_DTYPE)
    wf = part(_C_WF, _C_QB)
    wi_ref[...] = wf[:, :N_IDX_HEADS] * (N_IDX_HEADS ** -0.5)
    f = wf + bf_ref[...]
    logf = -(jnp.maximum(-f, 0.0) + jnp.log1p(jnp.exp(-jnp.abs(f))))
    logf_ref[...] = pltpu.roll(logf, LANES - N_IDX_HEADS, 1)[:, :N_HEADS_B]
    qb_ref[...] = (part(_C_QB, _C_KB) * q_scale).astype(MXU_DTYPE)
    kb = part(_C_KB, _C_VB)
    kb_ref[...] = heads(kb)
    kbb_ref[...] = kb.astype(MXU_DTYPE)
    vb = part(_C_VB, _C_END)
    vb_ref[...] = heads(vb)
    vbb_ref[...] = vb.astype(MXU_DTYPE)


def _project(h, w, bf, cos, sin, tm, tab_blocks):
    n, d = h.shape
    f32, bf16 = jnp.float32, MXU_DTYPE
    kv_a, kv_b = (N_KV_A, HEAD_DIM), (N_HEADS_B, HEAD_DIM)
    tails = [((512,), bf16), (kv_a, f32), (kv_a, f32), ((512,), bf16), ((IDX_DIM,), f32), ((N_IDX_HEADS,), f32),
             ((512,), bf16), (kv_b, f32), (kv_b, f32), ((N_HEADS_B,), f32),
             ((128,), bf16), ((128,), bf16), ((128,), bf16), ((512,), bf16), ((512,), bf16)]
    row = lambda *tail: pl.BlockSpec((tm,) + tail, lambda i: (i,) + (0,) * len(tail))
    tab = pl.BlockSpec((tm, LANES), lambda i: (i % tab_blocks, 0))
    return pl.pallas_call(
        _project_kernel,
        grid=(n // tm,),
        in_specs=[row(d), _const_spec(w.shape), _const_spec((1, LANES)), tab, tab],
        out_specs=[row(*tail) for tail, _ in tails],
        out_shape=[jax.ShapeDtypeStruct((n,) + tail, dt) for tail, dt in tails],
        compiler_params=pltpu.CompilerParams(
            dimension_semantics=("arbitrary",), vmem_limit_bytes=V7X_VMEM_LIMIT_BYTES),
        name="project",
    )(h, w, bf, cos, sin)


def _cumsum_kernel(x_ref, o_ref, *, seg):
    r = lax.broadcasted_iota(jnp.int32, (seg, seg), 0)
    c = lax.broadcasted_iota(jnp.int32, (seg, seg), 1)
    tri = (r <= c).astype(jnp.float32)
    carry = jnp.zeros((x_ref.shape[1], 1), jnp.float32)
    for s in range(x_ref.shape[2] // seg):
        cols = slice(s * seg, (s + 1) * seg)
        part = jnp.dot(x_ref[0, :, cols], tri, preferred_element_type=jnp.float32,
                       precision=lax.Precision.HIGHEST) + carry
        o_ref[0, :, cols] = part * LOG2E
        carry = part[:, seg - 1:seg]


def _cumsum_keys(x):
    b, h, s = x.shape
    seg = 2 * LANES if s % (2 * LANES) == 0 else LANES
    assert s % seg == 0
    return pl.pallas_call(
        functools.partial(_cumsum_kernel, seg=seg),
        grid=(b,),
        in_specs=[pl.BlockSpec((1, h, s), lambda i: (i, 0, 0))],
        out_specs=pl.BlockSpec((1, h, s), lambda i: (i, 0, 0)),
        out_shape=jax.ShapeDtypeStruct((b, h, s), jnp.float32),
        compiler_params=pltpu.CompilerParams(dimension_semantics=("arbitrary",)),
        name="cumsum_keys",
    )(x)


def _num_key_blocks(i, tq, kb, q_off):
    return (q_off + (i + 1) * tq + kb - 1) // kb


def _lane_groups(x):
    return [x[:, g * LANES:(g + 1) * LANES] for g in range(x.shape[1] // LANES)]


def _lane_tile(x, width):
    return jnp.concatenate([x] * (width // LANES), axis=1)


def _softmax_step(s, h, m_ref, l_ref, acc_ref, v_ones):
    groups = _lane_groups(s)
    smax = functools.reduce(jnp.maximum, groups)
    m_old = m_ref[h]
    m_new = jnp.maximum(m_old, jnp.max(smax, axis=-1, keepdims=True))
    m_safe = jnp.maximum(m_new, F32_LOWEST)
    corr = jnp.exp2(m_old - m_safe)
    p = jnp.concatenate([jnp.exp2(g - m_safe) for g in groups], axis=1).astype(MXU_DTYPE)
    pv = _dot(p, v_ones)
    acc_ref[h] = corr * acc_ref[h] + pv[:, :LANES]
    l_ref[h] = corr * l_ref[h] + pv[:, LANES:]
    m_ref[h] = m_new


def _store_values_and_ones(v_ref, lanes, vo_ref):
    vo_ref[:, :LANES] = v_ref[:, lanes]
    vo_ref[:, LANES:] = jnp.ones((vo_ref.shape[0], LANES), vo_ref.dtype)


def _store_half_masks(q_ref, qm_ref):
    for p in range(q_ref.shape[1] // LANES):
        q2 = q_ref[:, p * LANES:(p + 1) * LANES]
        lane = lax.broadcasted_iota(jnp.int32, q2.shape, 1)
        zero = jnp.zeros_like(q2)
        qm_ref[2 * p] = jnp.where(lane < HEAD_DIM, q2, zero)
        qm_ref[2 * p + 1] = jnp.where(lane >= HEAD_DIM, q2, zero)


def _init_state(m_ref, l_ref, acc_ref):
    m_ref[...] = jnp.full(m_ref.shape, NEG_INF, jnp.float32)
    l_ref[...] = jnp.zeros(l_ref.shape, jnp.float32)
    acc_ref[...] = jnp.zeros(acc_ref.shape, jnp.float32)


def _write_pairs(o_ref, l_ref, acc_ref, stack):
    tq = o_ref.shape[0]

    def slot(h):
        rows = slice((h % stack) * tq, (h % stack + 1) * tq)
        return acc_ref[h // stack, rows, :] / l_ref[h // stack, rows, :]

    for p in range(o_ref.shape[1] // LANES):
        lo, hi = slot(2 * p), slot(2 * p + 1)
        lane = lax.broadcasted_iota(jnp.int32, lo.shape, 1)
        o_ref[:, p * LANES:(p + 1) * LANES] = jnp.where(lane < HEAD_DIM, lo, hi).astype(o_ref.dtype)


def _stacked_queries(qm_ref, g, stack):
    q = qm_ref[g * stack:(g + 1) * stack]
    return q.reshape(q.shape[0] * q.shape[1], LANES)


def _head_stack(tq, n_sharing):
    return max(1, min(n_sharing, STACK_ROWS // tq))


def _attention_scratch(tq, n_heads, stack):
    state = (n_heads // stack, stack * tq, LANES)
    return [pltpu.VMEM((n_heads, tq, LANES), MXU_DTYPE),
            pltpu.VMEM(state, jnp.float32),
            pltpu.VMEM(state, jnp.float32),
            pltpu.VMEM(state, jnp.float32)]


def _fox_kernel(q_ref, k_ref, v_ref, ck_ref, o_ref, vo_ref, qm_ref, m_ref, l_ref, acc_ref,
                *, tq, kb, q_off, stack):
    i = pl.program_id(1)
    nb = _num_key_blocks(i, tq, kb, q_off)
    assert kb % tq == 0 and q_off % tq == 0

    @pl.when(i == 0)
    def _():
        for p in range(N_HEADS_B // 2):
            _store_values_and_ones(v_ref, slice(p * LANES, (p + 1) * LANES), vo_ref.at[p])

    _store_half_masks(q_ref, qm_ref)
    _init_state(m_ref, l_ref, acc_ref)

    def block(j, masked):
        start = pl.multiple_of(j * kb, kb)
        ck = ck_ref[0, j]
        if masked:
            row = q_off + i * tq + lax.broadcasted_iota(jnp.int32, (tq, kb), 0)
            col = j * kb + lax.broadcasted_iota(jnp.int32, (tq, kb), 1)
            causal = jnp.concatenate([col <= row] * stack, axis=0)
        for g in range(N_HEADS_B // stack):
            pair = g * stack // 2
            lanes = slice(pair * LANES, (pair + 1) * LANES)
            if stack == 1:
                forget = ck[g:g + 1, :]
            else:
                forget = jnp.concatenate([jnp.broadcast_to(ck[h:h + 1, :], (tq, kb))
                                          for h in range(g * stack, (g + 1) * stack)], axis=0)
            s = _nt_dot(_stacked_queries(qm_ref, g, stack), k_ref[pl.ds(start, kb), lanes]) - forget
            if masked:
                s = jnp.where(causal, s, NEG_INF)
            _softmax_step(s, g, m_ref, l_ref, acc_ref, vo_ref[pair, pl.ds(start, kb), :])

    def full_block(j, _):
        block(j, masked=False)
        return 0

    lax.fori_loop(0, nb - 1, full_block, 0)
    block(nb - 1, masked=True)
    _write_pairs(o_ref, l_ref, acc_ref, stack)


def _fox(q, k, v, ck, *, batch, tq, kb, q_off):
    n, width = q.shape
    t_q = n // batch
    s_len = k.shape[0] // batch
    nq = t_q // tq
    stack = _head_stack(tq, 2)
    return pl.pallas_call(
        functools.partial(_fox_kernel, tq=tq, kb=kb, q_off=q_off, stack=stack),
        grid=(batch, nq),
        in_specs=[
            pl.BlockSpec((tq, width), lambda b, i: (b * nq + i, 0)),
            pl.BlockSpec((s_len, width), lambda b, i: (b, 0)),
            pl.BlockSpec((s_len, width), lambda b, i: (b, 0)),
            pl.BlockSpec((1,) + ck.shape[1:], lambda b, i: (b, 0, 0, 0)),
        ],
        out_specs=pl.BlockSpec((tq, width), lambda b, i: (b * nq + i, 0)),
        out_shape=jax.ShapeDtypeStruct((n, width), MXU_DTYPE),
        scratch_shapes=[pltpu.VMEM((N_HEADS_B // 2, s_len, 2 * LANES), MXU_DTYPE)]
        + _attention_scratch(tq, N_HEADS_B, stack),
        compiler_params=pltpu.CompilerParams(
            dimension_semantics=("arbitrary", "arbitrary"), vmem_limit_bytes=V7X_VMEM_LIMIT_BYTES),
        name="fox_attention",
    )(q, k, v, ck)


def _dsa_kernel(qi_ref, wi_ref, qa_ref, ki_ref, ka_ref, va_ref, o_ref,
                key_ref, word_ref, bias_ref, thr_ref, cand_ref, lim_ref, cnt_ref, vo_ref,
                qm_ref, m_ref, l_ref, acc_ref,
                *, tq, kb, q_off, k_sel, stack):
    i = pl.program_id(1)
    nb = _num_key_blocks(i, tq, kb, q_off)
    k_f = jnp.float32(k_sel)
    rc = min(ROW_CHUNK, tq)
    chunks = [slice(r * rc, (r + 1) * rc) for r in range(tq // rc)]
    groups = [slice(g * LANES, (g + 1) * LANES) for g in range(kb // LANES)]

    def replicate(x):
        return jnp.broadcast_to(x, (tq, LANES))

    def pack_fields(f):
        return (f[:, :kb // 2] << 16) | f[:, kb // 2:]

    assert kb % (2 * LANES) == 0
    _store_half_masks(qi_ref, qm_ref)
    wi = wi_ref[...]

    def score_block(j, _):
        start = pl.multiple_of(j * kb, kb)
        kblk = ki_ref[pl.ds(start, kb), :]
        score = jnp.zeros((tq, kb), jnp.float32)
        if stack == N_IDX_HEADS:
            weights = jnp.concatenate([wi[:, h:h + 1] for h in range(N_IDX_HEADS)], axis=0)
            rel = weights * jnp.maximum(_nt_dot(_stacked_queries(qm_ref, 0, stack), kblk), 0.0)
            for h in range(N_IDX_HEADS):
                score = score + rel[h * tq:(h + 1) * tq, :]
        else:
            for h in range(N_IDX_HEADS):
                score = score + wi[:, h:h + 1] * jnp.maximum(_nt_dot(qm_ref[h], kblk), 0.0)
        bits = lax.bitcast_convert_type(score, jnp.int32)
        key = jnp.where(bits < 0, bits ^ jnp.int32(0x7FFFFFFF), bits)
        row = q_off + i * tq + lax.broadcasted_iota(jnp.int32, (tq, kb), 0)
        col = j * kb + lax.broadcasted_iota(jnp.int32, (tq, kb), 1)
        admissible = (col >> _CHUNK_SHIFT) <= (row >> _CHUNK_SHIFT)
        key = jnp.where(admissible, key, jnp.int32(INT_MIN))
        key_ref[j] = key
        word_ref[j] = jnp.transpose(pack_fields((key >> (32 - FIELD_BITS)) + (FIELD_GUARD + FIELD_GUARD // 2)))
        return 0

    lax.fori_loop(0, nb, score_block, 0)

    def count(pred):
        def body(j, acc):
            parts = []
            for rows in chunks:
                hits = [jnp.where(pred(key_ref[j, rows, g], rows, j * kb + g.start), 1.0, 0.0) for g in groups]
                parts.append(functools.reduce(jnp.add, hits))
            return acc + jnp.concatenate(parts, axis=0)
        acc = lax.fori_loop(0, nb, body, jnp.zeros((tq, LANES), jnp.float32))
        return replicate(jnp.sum(acc, axis=-1, keepdims=True))

    def count_fields_at_least(cand):
        both = (cand << 16) | cand

        def body(j, acc):
            for r0 in range(0, kb // 2, ROW_CHUNK):
                words = word_ref[j, r0:r0 + ROW_CHUNK, :]
                flags = lax.shift_right_logical(words - both, FIELD_BITS) & FIELD_FLAGS
                acc = acc + jnp.sum(flags.reshape(ROW_CHUNK // 8, 8, tq), axis=0)
            return acc
        acc = lax.fori_loop(0, nb, body, jnp.zeros((8, tq), jnp.int32))
        return jnp.sum((acc & 0xFFFF) + lax.shift_right_logical(acc, 16), axis=0, keepdims=True)

    def bisect_fields(cnt):
        def step(b, carry):
            thr, cnt = carry
            cand = thr | (jnp.int32(1) << (jnp.int32(FIELD_BITS - 1) - b))
            c = count_fields_at_least(cand)
            take = c >= k_sel
            return jnp.where(take, cand, thr), jnp.where(take, c, cnt)

        return lax.fori_loop(0, FIELD_BITS, step, (jnp.zeros((1, tq), jnp.int32), cnt))

    def per_row(row):
        return jnp.transpose(jnp.broadcast_to(row, (LANES, tq)))

    top, cnt = bisect_fields(jnp.full((1, tq), k_sel, jnp.int32))
    lim_ref[...] = per_row(top << FIELD_BITS)

    def middle_fields(j, _):
        words = []
        for rows in chunks:
            base = lim_ref[rows, :] - (FIELD_GUARD + (1 << 29))
            mid = [jnp.clip((key_ref[j, rows, g] >> 2) - base, FIELD_GUARD, 2 * FIELD_GUARD - 1) for g in groups]
            words.append(pack_fields(jnp.concatenate(mid, axis=1)))
        word_ref[j] = jnp.transpose(jnp.concatenate(words, axis=0))
        return 0

    lax.fori_loop(0, nb, middle_fields, 0)
    middle, cnt = bisect_fields(cnt)
    prefix = (top << FIELD_BITS) | middle
    thr_ref[...] = per_row((prefix << 2) ^ jnp.int32(INT_MIN))
    cnt_ref[...] = per_row(cnt).astype(jnp.float32)

    def low_bit(b, _):
        cand = thr_ref[...] | (jnp.int32(2) >> b)
        cand_ref[...] = cand
        c = count(lambda key, rows, first: key >= cand_ref[rows, :])
        take = c >= k_f
        thr_ref[...] = jnp.where(take, cand, thr_ref[...])
        cnt_ref[...] = jnp.where(take, c, cnt_ref[...])
        return 0

    lax.fori_loop(0, 2, low_bit, 0)

    thr = thr_ref[...]
    has_thr = thr > jnp.int32(INT_MIN)
    surplus = jnp.logical_and(has_thr, cnt_ref[...] > k_f)

    def bias_without_surplus():
        cand_ref[...] = jnp.maximum(thr_ref[...], jnp.int32(INT_MIN + 1))

        def block(j, _):
            for rows in chunks:
                for g in groups:
                    bias_ref[j, rows, g] = jnp.where(key_ref[j, rows, g] >= cand_ref[rows, :], 0.0, NEG_INF)
            return 0

        lax.fori_loop(0, nb, block, 0)

    def bias_with_surplus():
        above = count(lambda key, rows, first: key > thr_ref[rows, :])
        cnt_ref[...] = jnp.where(has_thr, k_f - above, 0.0)
        r = lax.broadcasted_iota(jnp.int32, (kb, kb), 0)
        c = lax.broadcasted_iota(jnp.int32, (kb, kb), 1)
        tri = (r <= c).astype(MXU_DTYPE)

        def block(j, before):
            key = key_ref[j]
            tied = key == _lane_tile(thr_ref[...], kb)
            rank = _dot(jnp.where(tied, 1.0, 0.0).astype(MXU_DTYPE), tri)
            take = jnp.logical_and(tied, rank + _lane_tile(before, kb) <= _lane_tile(cnt_ref[...], kb))
            sel = jnp.logical_or(key > _lane_tile(thr_ref[...], kb), take)
            bias_ref[j] = jnp.where(sel, 0.0, NEG_INF)
            return before + replicate(jnp.max(rank, axis=-1, keepdims=True))

        lax.fori_loop(0, nb, block, jnp.zeros((tq, LANES), jnp.float32))

    any_surplus = jnp.max(jnp.where(surplus, 1.0, 0.0)) > 0.0
    lax.cond(any_surplus, bias_with_surplus, bias_without_surplus)

    @pl.when(i == 0)
    def _():
        _store_values_and_ones(va_ref, slice(0, LANES), vo_ref)

    _store_half_masks(qa_ref, qm_ref)
    _init_state(m_ref, l_ref, acc_ref)

    def attend(j, _):
        start = pl.multiple_of(j * kb, kb)
        for g in range(N_HEADS_A // stack):
            bias = bias_ref[j] if stack == 1 else jnp.concatenate([bias_ref[j]] * stack, axis=0)
            s = _nt_dot(_stacked_queries(qm_ref, g, stack), ka_ref[pl.ds(start, kb), :]) + bias
            _softmax_step(s, g, m_ref, l_ref, acc_ref, vo_ref[pl.ds(start, kb), :])
        return 0

    lax.fori_loop(0, nb, attend, 0)
    _write_pairs(o_ref, l_ref, acc_ref, stack)


def _dsa(qi, wi, qa, ki, ka, va, *, batch, tq, kb, q_off, k_sel):
    n, width = qa.shape
    t_q = n // batch
    s_len = ka.shape[0] // batch
    nq = t_q // tq
    nblk = s_len // kb
    qspec = lambda wdt: pl.BlockSpec((tq, wdt), lambda b, i: (b * nq + i, 0))
    kspec = pl.BlockSpec((s_len, LANES), lambda b, i: (b, 0))
    assert N_IDX_HEADS == N_HEADS_A
    stack = _head_stack(tq, N_HEADS_A)
    return pl.pallas_call(
        functools.partial(_dsa_kernel, tq=tq, kb=kb, q_off=q_off, k_sel=k_sel, stack=stack),
        grid=(batch, nq),
        in_specs=[qspec(width), qspec(N_IDX_HEADS), qspec(width), kspec, kspec, kspec],
        out_specs=qspec(width),
        out_shape=jax.ShapeDtypeStruct((n, width), MXU_DTYPE),
        scratch_shapes=[pltpu.VMEM((nblk, tq, kb), jnp.int32),
                        pltpu.VMEM((nblk, kb // 2, tq), jnp.int32),
                        pltpu.VMEM((nblk, tq, kb), jnp.float32),
                        pltpu.VMEM((tq, LANES), jnp.int32),
                        pltpu.VMEM((tq, LANES), jnp.int32),
                        pltpu.VMEM((tq, LANES), jnp.int32),
                        pltpu.VMEM((tq, LANES), jnp.float32),
                        pltpu.VMEM((s_len, 2 * LANES), MXU_DTYPE),
                        ] + _attention_scratch(tq, N_HEADS_A, stack),
        compiler_params=pltpu.CompilerParams(
            dimension_semantics=("arbitrary", "arbitrary"), vmem_limit_bytes=V7X_VMEM_LIMIT_BYTES),
        name="dsa_attention",
    )(qi, wi, qa, ki, ka, va)


def _post_kernel(h_ref, oa_ref, ob_ref, woa_ref, wob_ref, g2_ref, b2_ref,
                 wg_ref, wu_ref, wd_ref, g3_ref, b3_ref, o_ref, act_ref):
    mix = _dot(oa_ref[...], woa_ref[...]) + _dot(ob_ref[...], wob_ref[...])
    h2 = _layer_norm(ALPHA * h_ref[...] + mix, g2_ref[...], b2_ref[...])
    o_ref[...] = _macaron_half(h2, wg_ref, wu_ref, wd_ref, g3_ref[...], b3_ref[...], act_ref)


def _post(h, oa, ob, woa, wob, g2, b2, wg, wu, wd, g3, b3):
    n, d = h.shape
    d_ff = wg.shape[1]
    tm = _row_tile(n)
    row = lambda wdt: pl.BlockSpec((tm, wdt), lambda i: (i, 0))
    return pl.pallas_call(
        _post_kernel,
        grid=(n // tm,),
        in_specs=[row(d), row(oa.shape[1]), row(ob.shape[1]),
                  _const_spec(woa.shape), _const_spec(wob.shape), _const_spec((1, d)), _const_spec((1, d)),
                  _const_spec((d, d_ff)), _const_spec((d, d_ff)), _const_spec((d_ff, d)),
                  _const_spec((1, d)), _const_spec((1, d))],
        out_specs=row(d),
        out_shape=jax.ShapeDtypeStruct((n, d), jnp.float32),
        scratch_shapes=[pltpu.VMEM((tm, d_ff), MXU_DTYPE)],
        compiler_params=pltpu.CompilerParams(
            dimension_semantics=("arbitrary",), vmem_limit_bytes=V7X_VMEM_LIMIT_BYTES),
        name="post",
    )(h, oa, ob, woa, wob, g2, b2, wg, wu, wd, g3, b3)


def _prepare_w_in(w_in):
    d = w_in.shape[0]
    sizes = (N_HEADS_A * HEAD_DIM, N_KV_A * HEAD_DIM, N_KV_A * HEAD_DIM, N_IDX_HEADS * IDX_DIM, IDX_DIM,
             N_IDX_HEADS, N_HEADS_B * HEAD_DIM, N_HEADS_B * HEAD_DIM, N_HEADS_B * HEAD_DIM, N_HEADS_B)
    offs = [0]
    for s in sizes:
        offs.append(offs[-1] + s)
    qa, ka, va, qi, ki, wi, qb, kb, vb, fb = (w_in[:, offs[k]:offs[k + 1]] for k in range(10))
    qa = qa.reshape(d, N_HEADS_A, HEAD_DIM)[:, jnp.array(_QA_HEAD_ORDER)].reshape(d, -1)
    pad = jnp.zeros((d, LANES - N_IDX_HEADS - N_HEADS_B), w_in.dtype)
    w = jnp.concatenate([qa, ka, va, qi, ki, ki, wi, fb, pad, qb, kb, vb], axis=1)
    assert w.shape[1] == _C_END
    return w.astype(MXU_DTYPE)


def _rope_tables(pos):
    half = ROT_DIM // 2
    inv_freq = ROPE_THETA ** (-jnp.arange(half, dtype=jnp.float32) * 2.0 / ROT_DIM)
    ang = pos.astype(jnp.float32)[:, None] * inv_freq[None, :]
    cos, sin = jnp.cos(ang), jnp.sin(ang)
    ones = jnp.ones((pos.shape[0], HEAD_DIM - ROT_DIM), jnp.float32)
    cos64 = jnp.concatenate([cos, cos, ones], axis=1)
    sin64 = jnp.concatenate([-sin, sin, jnp.zeros_like(ones)], axis=1)
    return jnp.tile(cos64, (1, LANES // HEAD_DIM)), jnp.tile(sin64, (1, LANES // HEAD_DIM))


def _project_tokens(h, w_in_p, bf_p, pos, rows_per_seq):
    n = h.shape[0]
    tm = _row_tile(n)
    cos, sin = _rope_tables(pos)
    if tm <= rows_per_seq:
        assert rows_per_seq % tm == 0
        tab_blocks = rows_per_seq // tm
    else:
        assert tm % rows_per_seq == 0
        cos = jnp.tile(cos, (tm // rows_per_seq, 1))
        sin = jnp.tile(sin, (tm // rows_per_seq, 1))
        tab_blocks = 1
    return _project(h, w_in_p, bf_p, cos, sin, tm, tab_blocks)


def _blocked_cum(logf_keys, kb):
    b, s, h = logf_keys.shape
    cum = _cumsum_keys(jnp.transpose(logf_keys, (0, 2, 1)))
    return jnp.transpose(cum.reshape(b, h, s // kb, kb), (0, 2, 1, 3))


def _pad_keys(x, s_pad):
    return jnp.pad(x, ((0, 0), (0, s_pad - x.shape[1]), (0, 0)))


def kernel(x_prompt, x_sample, cache_k_a, cache_v_a, cache_kidx_a, cache_k_b, cache_v_b, cache_logf_b,
           w_in, b_f, w_out, ln1_g, ln1_b, ffn1_w_gate, ffn1_w_up, ffn1_w_down,
           ln2_g, ln2_b, ln3_g, ln3_b, ffn2_w_gate, ffn2_w_up, ffn2_w_down):
    assert w_in.shape[0] == DEPTH
    bsz, seq, d = x_prompt.shape
    dbs, dseq, _ = x_sample.shape
    past = cache_k_a.shape[2]
    bf16 = MXU_DTYPE

    w_in_p = _prepare_w_in(w_in[0])
    bf_p = jnp.zeros((1, LANES), jnp.float32).at[0, N_IDX_HEADS:N_IDX_HEADS + N_HEADS_B].set(b_f[0])
    order = jnp.array(_QA_HEAD_ORDER)
    w_out_a = w_out[0, :N_HEADS_A * HEAD_DIM].reshape(N_HEADS_A, HEAD_DIM, d)[order].reshape(-1, d).astype(bf16)
    w_out_b = w_out[0, N_HEADS_A * HEAD_DIM:].astype(bf16)
    ffn1 = (ffn1_w_gate[0].astype(bf16), ffn1_w_up[0].astype(bf16), ffn1_w_down[0].astype(bf16))
    ffn2 = (ffn2_w_gate[0].astype(bf16), ffn2_w_up[0].astype(bf16), ffn2_w_down[0].astype(bf16))
    vec = lambda a: a[0].reshape(1, d)

    def layer(x, pos, rows_per_seq, attend):
        n = x.shape[0] * x.shape[1]
        h = _ffn_ln(x.reshape(n, d), *ffn1, vec(ln1_g), vec(ln1_b))
        proj = _project_tokens(h, w_in_p, bf_p, pos, rows_per_seq)
        oa, ob = attend(proj)
        y = _post(h, oa, ob, w_out_a, w_out_b, vec(ln2_g), vec(ln2_b), *ffn2, vec(ln3_g), vec(ln3_b))
        return y.reshape(x.shape), proj

    tq_p = min(Q_TILE, seq)
    kb_p = min(KEY_BLOCK, seq)

    def attend_prompt(proj):
        qa, _, _, qi, _, wi, qb, _, _, logf, kab, vab, kib, kbb, vbb = proj
        ck = _blocked_cum(logf.reshape(bsz, seq, N_HEADS_B), kb_p)
        oa = _dsa(qi, wi, qa, kib, kab, vab, batch=bsz, tq=tq_p, kb=kb_p, q_off=0,
                  k_sel=min(TOPK_MAX, seq // 4))
        ob = _fox(qb, kbb, vbb, ck, batch=bsz, tq=tq_p, kb=kb_p, q_off=0)
        return oa, ob

    y_p, proj_p = layer(x_prompt, jnp.arange(seq, dtype=jnp.int32), seq, attend_prompt)

    total = past + dseq
    kb_s = min(SAMPLE_KEY_BLOCK, past)
    s_pad = -(-total // kb_s) * kb_s

    def attend_sample(proj):
        qa, _, _, qi, _, wi, qb, _, _, logf, kab, vab, kib, kbb, vbb = proj
        new = lambda a: a.reshape(dbs, dseq, -1)
        keys = lambda c, nw: lax.dynamic_update_slice(
            _pad_keys(c.reshape(dbs, past, -1).astype(bf16), s_pad), new(nw), (0, past, 0)
        ).reshape(dbs * s_pad, -1)
        kidx2 = jnp.concatenate([cache_kidx_a[0], cache_kidx_a[0]], axis=-1)
        logf_all = _pad_keys(jnp.concatenate([cache_logf_b[0].astype(jnp.float32), new(logf)], axis=1), s_pad)
        ck = _blocked_cum(logf_all, kb_s)
        oa = _dsa(qi, wi, qa, keys(kidx2, kib), keys(cache_k_a[0], kab), keys(cache_v_a[0], vab),
                  batch=dbs, tq=dseq, kb=kb_s, q_off=past, k_sel=min(TOPK_MAX, total // 4))
        ob = _fox(qb, keys(cache_k_b[0], kbb), keys(cache_v_b[0], vbb), ck,
                  batch=dbs, tq=dseq, kb=kb_s, q_off=past)
        return oa, ob

    y_s, proj_s = layer(x_sample, past + jnp.arange(dseq, dtype=jnp.int32), dseq, attend_sample)

    def rows(proj, b, t):
        _, ka, va, _, kidx, _, _, kbv, vbv, logf = proj[:10]
        return (ka.reshape(1, b, t, N_KV_A, HEAD_DIM), va.reshape(1, b, t, N_KV_A, HEAD_DIM),
                kidx.reshape(1, b, t, IDX_DIM), kbv.reshape(1, b, t, N_HEADS_B, HEAD_DIM),
                vbv.reshape(1, b, t, N_HEADS_B, HEAD_DIM), logf.reshape(1, b, t, N_HEADS_B))

    return (y_p, y_s) + rows(proj_p, bsz, seq) + rows(proj_s, dbs, dseq)
```

```python
import functools
import math

import jax
import jax.numpy as jnp
from jax import lax
from jax.experimental import pallas as pl
from jax.experimental.pallas import tpu as pltpu

CHUNK = 64
_CHUNK_SHIFT = 6
HEAD_DIM = 64
N_HEADS_A = 8
N_KV_A = 2
N_IDX_HEADS = 8
IDX_DIM = 64
TOPK_MAX = 256
N_HEADS_B = 8
ROT_DIM = HEAD_DIM // 4
ROPE_THETA = 500000.0
LN_EPS = 1e-5
DEPTH = 1
ALPHA = (2.0 * DEPTH) ** 0.25

MXU_DTYPE = jnp.bfloat16

LANES = 128
V7X_VMEM_LIMIT_BYTES = 60 * 1024 * 1024

ROW_TILE = 512
FF_CHUNK = 256
Q_TILE = 512
KEY_BLOCK = 512
SAMPLE_KEY_BLOCK = 512
ROW_CHUNK = 64
STACK_ROWS = 512

INT_MIN = -(2 ** 31)
FIELD_BITS = 15
FIELD_GUARD = 1 << FIELD_BITS
FIELD_FLAGS = 0x00010001
NEG_INF = float("-inf")
F32_LOWEST = float(jnp.finfo(jnp.float32).min)
LOG2E = math.log2(math.e)

_C_QA, _C_KA, _C_VA, _C_QI, _C_KI, _C_WF, _C_QB, _C_KB, _C_VB, _C_END = (
    0, 512, 640, 768, 1280, 1408, 1536, 2048, 2560, 3072)
_QA_HEAD_ORDER = (0, 4, 1, 5, 2, 6, 3, 7)


def _nt_dot(a, b):
    return lax.dot_general(a, b, (((1,), (1,)), ((), ())), preferred_element_type=jnp.float32)


def _dot(a, b):
    return jnp.dot(a, b, preferred_element_type=jnp.float32)


def _layer_norm(x, g, b):
    mu = jnp.mean(x, axis=-1, keepdims=True)
    xc = x - mu
    var = jnp.mean(xc * xc, axis=-1, keepdims=True)
    return xc * lax.rsqrt(var + LN_EPS) * g + b


def _macaron_half(x, wg_ref, wu_ref, wd_ref, g, b, act_ref):
    xb = x.astype(MXU_DTYPE)
    d_ff = wg_ref.shape[1]
    for c in range(d_ff // FF_CHUNK):
        cols = slice(c * FF_CHUNK, (c + 1) * FF_CHUNK)
        gate = _dot(xb, wg_ref[:, cols])
        up = _dot(xb, wu_ref[:, cols])
        act_ref[:, cols] = (gate * jax.nn.sigmoid(gate) * up).astype(MXU_DTYPE)
    down = _dot(act_ref[...], wd_ref[...])
    return _layer_norm(ALPHA * x + 0.5 * down, g, b)


def _const_spec(shape):
    return pl.BlockSpec(shape, lambda *_: (0,) * len(shape), pipeline_mode=pl.Buffered(1))


def _row_tile(n):
    tm = ROW_TILE
    while n % tm:
        tm //= 2
    return tm


def _rope(x, cos, sin):
    lane = lax.broadcasted_iota(jnp.int32, x.shape, 1) & (HEAD_DIM - 1)
    partner = jnp.where(lane < ROT_DIM // 2,
                        pltpu.roll(x, LANES - ROT_DIM // 2, 1),
                        pltpu.roll(x, ROT_DIM // 2, 1))
    return x * cos + partner * sin


def _project_kernel(h_ref, w_ref, bf_ref, cos_ref, sin_ref,
                    qa_ref, ka_ref, va_ref, qi_ref, kidx_ref, wi_ref, qb_ref, kb_ref, vb_ref, logf_ref,
                    kab_ref, vab_ref, kib_ref, kbb_ref, vbb_ref):
    hb = h_ref[...].astype(MXU_DTYPE)
    cos = cos_ref[...]
    sin = sin_ref[...]
    q_scale = HEAD_DIM ** -0.5 * LOG2E

    def part(lo, hi):
        return _dot(hb, w_ref[:, lo:hi])

    def rope_groups(z):
        return jnp.concatenate(
            [_rope(z[:, g * LANES:(g + 1) * LANES], cos, sin) for g in range(z.shape[1] // LANES)], axis=1)

    qa_ref[...] = (rope_groups(part(_C_QA, _C_KA)) * q_scale).astype(MXU_DTYPE)
    def heads(z):
        return z.reshape(z.shape[0], z.shape[1] // HEAD_DIM, HEAD_DIM)

    ka = _rope(part(_C_KA, _C_VA), cos, sin)
    ka_ref[...] = heads(ka)
    kab_ref[...] = ka.astype(MXU_DTYPE)
    va = part(_C_VA, _C_QI)
    va_ref[...] = heads(va)
    vab_ref[...] = va.astype(MXU_DTYPE)
    qi_ref[...] = (rope_groups(part(_C_QI, _C_KI)) * (IDX_DIM ** -0.5)).astype(MXU_DTYPE)
    ki = _rope(part(_C_KI, _C_WF), cos, sin)
    kidx_ref[...] = ki[:, :IDX_DIM]
    kib_ref[...] = ki.astype(MXU_DTYPE)
    wf = part(_C_WF, _C_QB)
    wi_ref[...] = wf[:, :N_IDX_HEADS] * (N_IDX_HEADS ** -0.5)
    f = wf + bf_ref[...]
    logf = -(jnp.maximum(-f, 0.0) + jnp.log1p(jnp.exp(-jnp.abs(f))))
    logf_ref[...] = pltpu.roll(logf, LANES - N_IDX_HEADS, 1)[:, :N_HEADS_B]
    qb_ref[...] = (part(_C_QB, _C_KB) * q_scale).astype(MXU_DTYPE)
    kb = part(_C_KB, _C_VB)
    kb_ref[...] = heads(kb)
    kbb_ref[...] = kb.astype(MXU_DTYPE)
    vb = part(_C_VB, _C_END)
    vb_ref[...] = heads(vb)
    vbb_ref[...] = vb.astype(MXU_DTYPE)


def _ffn_project_kernel(x_ref, wg_ref, wu_ref, wd_ref, g_ref, b_ref, w_ref, bf_ref, cos_ref, sin_ref,
                        h_ref, *rest):
    *proj_refs, act_ref = rest
    h_ref[...] = _macaron_half(x_ref[...], wg_ref, wu_ref, wd_ref, g_ref[...], b_ref[...], act_ref)
    _project_kernel(h_ref, w_ref, bf_ref, cos_ref, sin_ref, *proj_refs)


def _ffn_project(x, wg, wu, wd, g, b, w, bf, cos, sin, tm, tab_blocks):
    n, d = x.shape
    d_ff = wg.shape[1]
    f32, bf16 = jnp.float32, MXU_DTYPE
    kv_a, kv_b = (N_KV_A, HEAD_DIM), (N_HEADS_B, HEAD_DIM)
    tails = [((512,), bf16), (kv_a, f32), (kv_a, f32), ((512,), bf16), ((IDX_DIM,), f32), ((N_IDX_HEADS,), f32),
             ((512,), bf16), (kv_b, f32), (kv_b, f32), ((N_HEADS_B,), f32),
             ((128,), bf16), ((128,), bf16), ((128,), bf16), ((512,), bf16), ((512,), bf16)]
    row = lambda *tail: pl.BlockSpec((tm,) + tail, lambda i: (i,) + (0,) * len(tail))
    tab = pl.BlockSpec((tm, LANES), lambda i: (i % tab_blocks, 0))
    tails = [((d,), f32)] + tails
    outs = pl.pallas_call(
        _ffn_project_kernel,
        grid=(n // tm,),
        in_specs=[row(d), _const_spec((d, d_ff)), _const_spec((d, d_ff)), _const_spec((d_ff, d)),
                  _const_spec((1, d)), _const_spec((1, d)),
                  _const_spec(w.shape), _const_spec((1, LANES)), tab, tab],
        out_specs=[row(*tail) for tail, _ in tails],
        out_shape=[jax.ShapeDtypeStruct((n,) + tail, dt) for tail, dt in tails],
        scratch_shapes=[pltpu.VMEM((tm, d_ff), MXU_DTYPE)],
        compiler_params=pltpu.CompilerParams(
            dimension_semantics=("arbitrary",), vmem_limit_bytes=V7X_VMEM_LIMIT_BYTES),
        name="ffn_project",
    )(x, wg, wu, wd, g, b, w, bf, cos, sin)
    return outs[0], outs[1:]


def _cumsum_kernel(x_ref, o_ref, *, seg):
    r = lax.broadcasted_iota(jnp.int32, (seg, seg), 0)
    c = lax.broadcasted_iota(jnp.int32, (seg, seg), 1)
    tri = (r <= c).astype(jnp.float32)
    carry = jnp.zeros((x_ref.shape[1], 1), jnp.float32)
    for s in range(x_ref.shape[2] // seg):
        cols = slice(s * seg, (s + 1) * seg)
        part = jnp.dot(x_ref[0, :, cols], tri, preferred_element_type=jnp.float32,
                       precision=lax.Precision.HIGHEST) + carry
        o_ref[0, :, cols] = part * LOG2E
        carry = part[:, seg - 1:seg]


def _cumsum_keys(x):
    b, h, s = x.shape
    seg = 2 * LANES if s % (2 * LANES) == 0 else LANES
    assert s % seg == 0
    return pl.pallas_call(
        functools.partial(_cumsum_kernel, seg=seg),
        grid=(b,),
        in_specs=[pl.BlockSpec((1, h, s), lambda i: (i, 0, 0))],
        out_specs=pl.BlockSpec((1, h, s), lambda i: (i, 0, 0)),
        out_shape=jax.ShapeDtypeStruct((b, h, s), jnp.float32),
        compiler_params=pltpu.CompilerParams(dimension_semantics=("arbitrary",)),
        name="cumsum_keys",
    )(x)


def _num_key_blocks(i, tq, kb, q_off):
    return (q_off + (i + 1) * tq + kb - 1) // kb


def _lane_groups(x):
    return [x[:, g * LANES:(g + 1) * LANES] for g in range(x.shape[1] // LANES)]


def _lane_tile(x, width):
    return jnp.concatenate([x] * (width // LANES), axis=1)


def _softmax_step(s, h, m_ref, l_ref, acc_ref, v_ones):
    groups = _lane_groups(s)
    smax = functools.reduce(jnp.maximum, groups)
    m_old = m_ref[h]
    m_new = jnp.maximum(m_old, jnp.max(smax, axis=-1, keepdims=True))
    m_safe = jnp.maximum(m_new, F32_LOWEST)
    corr = jnp.exp2(m_old - m_safe)
    p = jnp.concatenate([jnp.exp2(g - m_safe) for g in groups], axis=1).astype(MXU_DTYPE)
    pv = _dot(p, v_ones)
    acc_ref[h] = corr * acc_ref[h] + pv[:, :LANES]
    l_ref[h] = corr * l_ref[h] + pv[:, LANES:]
    m_ref[h] = m_new


def _store_values_and_ones(v_ref, lanes, vo_ref):
    vo_ref[:, :LANES] = v_ref[:, lanes]
    vo_ref[:, LANES:] = jnp.ones((vo_ref.shape[0], LANES), vo_ref.dtype)


def _store_half_masks(q_ref, qm_ref):
    for p in range(q_ref.shape[1] // LANES):
        q2 = q_ref[:, p * LANES:(p + 1) * LANES]
        lane = lax.broadcasted_iota(jnp.int32, q2.shape, 1)
        zero = jnp.zeros_like(q2)
        qm_ref[2 * p] = jnp.where(lane < HEAD_DIM, q2, zero)
        qm_ref[2 * p + 1] = jnp.where(lane >= HEAD_DIM, q2, zero)


def _init_state(m_ref, l_ref, acc_ref):
    m_ref[...] = jnp.full(m_ref.shape, NEG_INF, jnp.float32)
    l_ref[...] = jnp.zeros(l_ref.shape, jnp.float32)
    acc_ref[...] = jnp.zeros(acc_ref.shape, jnp.float32)


def _write_pairs(o_ref, l_ref, acc_ref, stack):
    tq = o_ref.shape[0]

    def slot(h):
        rows = slice((h % stack) * tq, (h % stack + 1) * tq)
        return acc_ref[h // stack, rows, :] / l_ref[h // stack, rows, :]

    for p in range(o_ref.shape[1] // LANES):
        lo, hi = slot(2 * p), slot(2 * p + 1)
        lane = lax.broadcasted_iota(jnp.int32, lo.shape, 1)
        o_ref[:, p * LANES:(p + 1) * LANES] = jnp.where(lane < HEAD_DIM, lo, hi).astype(o_ref.dtype)


def _stacked_queries(qm_ref, g, stack):
    q = qm_ref[g * stack:(g + 1) * stack]
    return q.reshape(q.shape[0] * q.shape[1], LANES)


def _head_stack(tq, n_sharing):
    return max(1, min(n_sharing, STACK_ROWS // tq))


def _attention_scratch(tq, n_heads, stack):
    state = (n_heads // stack, stack * tq, LANES)
    return [pltpu.VMEM((n_heads, tq, LANES), MXU_DTYPE),
            pltpu.VMEM(state, jnp.float32),
            pltpu.VMEM(state, jnp.float32),
            pltpu.VMEM(state, jnp.float32)]


def _fox_kernel(q_ref, k_ref, v_ref, ck_ref, o_ref, vo_ref, qm_ref, m_ref, l_ref, acc_ref,
                *, tq, kb, q_off, stack):
    i = pl.program_id(1)
    nb = _num_key_blocks(i, tq, kb, q_off)
    assert kb % tq == 0 and q_off % tq == 0

    @pl.when(i == 0)
    def _():
        for p in range(N_HEADS_B // 2):
            _store_values_and_ones(v_ref, slice(p * LANES, (p + 1) * LANES), vo_ref.at[p])

    _store_half_masks(q_ref, qm_ref)
    _init_state(m_ref, l_ref, acc_ref)

    def block(j, masked):
        start = pl.multiple_of(j * kb, kb)
        ck = ck_ref[0, j]
        if masked:
            row = q_off + i * tq + lax.broadcasted_iota(jnp.int32, (tq, kb), 0)
            col = j * kb + lax.broadcasted_iota(jnp.int32, (tq, kb), 1)
            causal = jnp.concatenate([col <= row] * stack, axis=0)
        for g in range(N_HEADS_B // stack):
            pair = g * stack // 2
            lanes = slice(pair * LANES, (pair + 1) * LANES)
            if stack == 1:
                forget = ck[g:g + 1, :]
            else:
                forget = jnp.concatenate([jnp.broadcast_to(ck[h:h + 1, :], (tq, kb))
                                          for h in range(g * stack, (g + 1) * stack)], axis=0)
            s = _nt_dot(_stacked_queries(qm_ref, g, stack), k_ref[pl.ds(start, kb), lanes]) - forget
            if masked:
                s = jnp.where(causal, s, NEG_INF)
            _softmax_step(s, g, m_ref, l_ref, acc_ref, vo_ref[pair, pl.ds(start, kb), :])

    def full_block(j, _):
        block(j, masked=False)
        return 0

    lax.fori_loop(0, nb - 1, full_block, 0)
    block(nb - 1, masked=True)
    _write_pairs(o_ref, l_ref, acc_ref, stack)


def _fox(q, k, v, ck, *, batch, tq, kb, q_off):
    n, width = q.shape
    t_q = n // batch
    s_len = k.shape[0] // batch
    nq = t_q // tq
    stack = _head_stack(tq, 2)
    return pl.pallas_call(
        functools.partial(_fox_kernel, tq=tq, kb=kb, q_off=q_off, stack=stack),
        grid=(batch, nq),
        in_specs=[
            pl.BlockSpec((tq, width), lambda b, i: (b * nq + i, 0)),
            pl.BlockSpec((s_len, width), lambda b, i: (b, 0)),
            pl.BlockSpec((s_len, width), lambda b, i: (b, 0)),
            pl.BlockSpec((1,) + ck.shape[1:], lambda b, i: (b, 0, 0, 0)),
        ],
        out_specs=pl.BlockSpec((tq, width), lambda b, i: (b * nq + i, 0)),
        out_shape=jax.ShapeDtypeStruct((n, width), MXU_DTYPE),
        scratch_shapes=[pltpu.VMEM((N_HEADS_B // 2, s_len, 2 * LANES), MXU_DTYPE)]
        + _attention_scratch(tq, N_HEADS_B, stack),
        compiler_params=pltpu.CompilerParams(
            dimension_semantics=("arbitrary", "arbitrary"), vmem_limit_bytes=V7X_VMEM_LIMIT_BYTES),
        name="fox_attention",
    )(q, k, v, ck)


def _dsa_kernel(qi_ref, wi_ref, qa_ref, ki_ref, ka_ref, va_ref, o_ref,
                key_ref, word_ref, bias_ref, thr_ref, cand_ref, lim_ref, cnt_ref, vo_ref,
                qm_ref, m_ref, l_ref, acc_ref,
                *, tq, kb, q_off, k_sel, stack):
    i = pl.program_id(1)
    nb = _num_key_blocks(i, tq, kb, q_off)
    k_f = jnp.float32(k_sel)
    rc = min(ROW_CHUNK, tq)
    chunks = [slice(r * rc, (r + 1) * rc) for r in range(tq // rc)]
    groups = [slice(g * LANES, (g + 1) * LANES) for g in range(kb // LANES)]

    def replicate(x):
        return jnp.broadcast_to(x, (tq, LANES))

    def pack_fields(f):
        return (f[:, :kb // 2] << 16) | f[:, kb // 2:]

    assert kb % (2 * LANES) == 0
    _store_half_masks(qi_ref, qm_ref)
    wi = wi_ref[...]

    def score_block(j, _):
        start = pl.multiple_of(j * kb, kb)
        kblk = ki_ref[pl.ds(start, kb), :]
        score = jnp.zeros((tq, kb), jnp.float32)
        if stack == N_IDX_HEADS:
            weights = jnp.concatenate([wi[:, h:h + 1] for h in range(N_IDX_HEADS)], axis=0)
            rel = weights * jnp.maximum(_nt_dot(_stacked_queries(qm_ref, 0, stack), kblk), 0.0)
            for h in range(N_IDX_HEADS):
                score = score + rel[h * tq:(h + 1) * tq, :]
        else:
            for h in range(N_IDX_HEADS):
                score = score + wi[:, h:h + 1] * jnp.maximum(_nt_dot(qm_ref[h], kblk), 0.0)
        bits = lax.bitcast_convert_type(score, jnp.int32)
        key = jnp.where(bits < 0, bits ^ jnp.int32(0x7FFFFFFF), bits)
        row = q_off + i * tq + lax.broadcasted_iota(jnp.int32, (tq, kb), 0)
        col = j * kb + lax.broadcasted_iota(jnp.int32, (tq, kb), 1)
        admissible = (col >> _CHUNK_SHIFT) <= (row >> _CHUNK_SHIFT)
        key = jnp.where(admissible, key, jnp.int32(INT_MIN))
        key_ref[j] = key
        word_ref[j] = jnp.transpose(pack_fields((key >> (32 - FIELD_BITS)) + (FIELD_GUARD + FIELD_GUARD // 2)))
        return 0

    lax.fori_loop(0, nb, score_block, 0)

    def count(pred):
        def body(j, acc):
            parts = []
            for rows in chunks:
                hits = [jnp.where(pred(key_ref[j, rows, g], rows, j * kb + g.start), 1.0, 0.0) for g in groups]
                parts.append(functools.reduce(jnp.add, hits))
            return acc + jnp.concatenate(parts, axis=0)
        acc = lax.fori_loop(0, nb, body, jnp.zeros((tq, LANES), jnp.float32))
        return replicate(jnp.sum(acc, axis=-1, keepdims=True))

    def count_fields_at_least(cand):
        both = (cand << 16) | cand

        def body(j, acc):
            for r0 in range(0, kb // 2, ROW_CHUNK):
                words = word_ref[j, r0:r0 + ROW_CHUNK, :]
                flags = lax.shift_right_logical(words - both, FIELD_BITS) & FIELD_FLAGS
                acc = acc + jnp.sum(flags.reshape(ROW_CHUNK // 8, 8, tq), axis=0)
            return acc
        acc = lax.fori_loop(0, nb, body, jnp.zeros((8, tq), jnp.int32))
        return jnp.sum((acc & 0xFFFF) + lax.shift_right_logical(acc, 16), axis=0, keepdims=True)

    def bisect_fields(cnt):
        def step(b, carry):
            thr, cnt = carry
            cand = thr | (jnp.int32(1) << (jnp.int32(FIELD_BITS - 1) - b))
            c = count_fields_at_least(cand)
            take = c >= k_sel
            return jnp.where(take, cand, thr), jnp.where(take, c, cnt)

        return lax.fori_loop(0, FIELD_BITS, step, (jnp.zeros((1, tq), jnp.int32), cnt))

    def per_row(row):
        return jnp.transpose(jnp.broadcast_to(row, (LANES, tq)))

    top, cnt = bisect_fields(jnp.full((1, tq), k_sel, jnp.int32))
    lim_ref[...] = per_row(top << FIELD_BITS)

    def middle_fields(j, _):
        words = []
        for rows in chunks:
            base = lim_ref[rows, :] - (FIELD_GUARD + (1 << 29))
            mid = [jnp.clip((key_ref[j, rows, g] >> 2) - base, FIELD_GUARD, 2 * FIELD_GUARD - 1) for g in groups]
            words.append(pack_fields(jnp.concatenate(mid, axis=1)))
        word_ref[j] = jnp.transpose(jnp.concatenate(words, axis=0))
        return 0

    lax.fori_loop(0, nb, middle_fields, 0)
    middle, cnt = bisect_fields(cnt)
    prefix = (top << FIELD_BITS) | middle
    thr_ref[...] = per_row((prefix << 2) ^ jnp.int32(INT_MIN))
    cnt_ref[...] = per_row(cnt).astype(jnp.float32)

    def low_bit(b, _):
        cand = thr_ref[...] | (jnp.int32(2) >> b)
        cand_ref[...] = cand
        c = count(lambda key, rows, first: key >= cand_ref[rows, :])
        take = c >= k_f
        thr_ref[...] = jnp.where(take, cand, thr_ref[...])
        cnt_ref[...] = jnp.where(take, c, cnt_ref[...])
        return 0

    lax.fori_loop(0, 2, low_bit, 0)

    thr = thr_ref[...]
    has_thr = thr > jnp.int32(INT_MIN)
    surplus = jnp.logical_and(has_thr, cnt_ref[...] > k_f)

    def bias_without_surplus():
        cand_ref[...] = jnp.maximum(thr_ref[...], jnp.int32(INT_MIN + 1))

        def block(j, _):
            for rows in chunks:
                for g in groups:
                    bias_ref[j, rows, g] = jnp.where(key_ref[j, rows, g] >= cand_ref[rows, :], 0.0, NEG_INF)
            return 0

        lax.fori_loop(0, nb, block, 0)

    def bias_with_surplus():
        above = count(lambda key, rows, first: key > thr_ref[rows, :])
        cnt_ref[...] = jnp.where(has_thr, k_f - above, 0.0)
        r = lax.broadcasted_iota(jnp.int32, (kb, kb), 0)
        c = lax.broadcasted_iota(jnp.int32, (kb, kb), 1)
        tri = (r <= c).astype(MXU_DTYPE)

        def block(j, before):
            key = key_ref[j]
            tied = key == _lane_tile(thr_ref[...], kb)
            rank = _dot(jnp.where(tied, 1.0, 0.0).astype(MXU_DTYPE), tri)
            take = jnp.logical_and(tied, rank + _lane_tile(before, kb) <= _lane_tile(cnt_ref[...], kb))
            sel = jnp.logical_or(key > _lane_tile(thr_ref[...], kb), take)
            bias_ref[j] = jnp.where(sel, 0.0, NEG_INF)
            return before + replicate(jnp.max(rank, axis=-1, keepdims=True))

        lax.fori_loop(0, nb, block, jnp.zeros((tq, LANES), jnp.float32))

    any_surplus = jnp.max(jnp.where(surplus, 1.0, 0.0)) > 0.0
    lax.cond(any_surplus, bias_with_surplus, bias_without_surplus)

    @pl.when(i == 0)
    def _():
        _store_values_and_ones(va_ref, slice(0, LANES), vo_ref)

    _store_half_masks(qa_ref, qm_ref)
    _init_state(m_ref, l_ref, acc_ref)

    def attend(j, _):
        start = pl.multiple_of(j * kb, kb)
        for g in range(N_HEADS_A // stack):
            bias = bias_ref[j] if stack == 1 else jnp.concatenate([bias_ref[j]] * stack, axis=0)
            s = _nt_dot(_stacked_queries(qm_ref, g, stack), ka_ref[pl.ds(start, kb), :]) + bias
            _softmax_step(s, g, m_ref, l_ref, acc_ref, vo_ref[pl.ds(start, kb), :])
        return 0

    lax.fori_loop(0, nb, attend, 0)
    _write_pairs(o_ref, l_ref, acc_ref, stack)


def _dsa(qi, wi, qa, ki, ka, va, *, batch, tq, kb, q_off, k_sel):
    n, width = qa.shape
    t_q = n // batch
    s_len = ka.shape[0] // batch
    nq = t_q // tq
    nblk = s_len // kb
    qspec = lambda wdt: pl.BlockSpec((tq, wdt), lambda b, i: (b * nq + i, 0))
    kspec = pl.BlockSpec((s_len, LANES), lambda b, i: (b, 0))
    assert N_IDX_HEADS == N_HEADS_A
    stack = _head_stack(tq, N_HEADS_A)
    return pl.pallas_call(
        functools.partial(_dsa_kernel, tq=tq, kb=kb, q_off=q_off, k_sel=k_sel, stack=stack),
        grid=(batch, nq),
        in_specs=[qspec(width), qspec(N_IDX_HEADS), qspec(width), kspec, kspec, kspec],
        out_specs=qspec(width),
        out_shape=jax.ShapeDtypeStruct((n, width), MXU_DTYPE),
        scratch_shapes=[pltpu.VMEM((nblk, tq, kb), jnp.int32),
                        pltpu.VMEM((nblk, kb // 2, tq), jnp.int32),
                        pltpu.VMEM((nblk, tq, kb), jnp.float32),
                        pltpu.VMEM((tq, LANES), jnp.int32),
                        pltpu.VMEM((tq, LANES), jnp.int32),
                        pltpu.VMEM((tq, LANES), jnp.int32),
                        pltpu.VMEM((tq, LANES), jnp.float32),
                        pltpu.VMEM((s_len, 2 * LANES), MXU_DTYPE),
                        ] + _attention_scratch(tq, N_HEADS_A, stack),
        compiler_params=pltpu.CompilerParams(
            dimension_semantics=("arbitrary", "arbitrary"), vmem_limit_bytes=V7X_VMEM_LIMIT_BYTES),
        name="dsa_attention",
    )(qi, wi, qa, ki, ka, va)


def _post_kernel(h_ref, oa_ref, ob_ref, woa_ref, wob_ref, g2_ref, b2_ref,
                 wg_ref, wu_ref, wd_ref, g3_ref, b3_ref, o_ref, act_ref):
    mix = _dot(oa_ref[...], woa_ref[...]) + _dot(ob_ref[...], wob_ref[...])
    h2 = _layer_norm(ALPHA * h_ref[...] + mix, g2_ref[...], b2_ref[...])
    o_ref[...] = _macaron_half(h2, wg_ref, wu_ref, wd_ref, g3_ref[...], b3_ref[...], act_ref)


def _post(h, oa, ob, woa, wob, g2, b2, wg, wu, wd, g3, b3):
    n, d = h.shape
    d_ff = wg.shape[1]
    tm = _row_tile(n)
    row = lambda wdt: pl.BlockSpec((tm, wdt), lambda i: (i, 0))
    return pl.pallas_call(
        _post_kernel,
        grid=(n // tm,),
        in_specs=[row(d), row(oa.shape[1]), row(ob.shape[1]),
                  _const_spec(woa.shape), _const_spec(wob.shape), _const_spec((1, d)), _const_spec((1, d)),
                  _const_spec((d, d_ff)), _const_spec((d, d_ff)), _const_spec((d_ff, d)),
                  _const_spec((1, d)), _const_spec((1, d))],
        out_specs=row(d),
        out_shape=jax.ShapeDtypeStruct((n, d), jnp.float32),
        scratch_shapes=[pltpu.VMEM((tm, d_ff), MXU_DTYPE)],
        compiler_params=pltpu.CompilerParams(
            dimension_semantics=("arbitrary",), vmem_limit_bytes=V7X_VMEM_LIMIT_BYTES),
        name="post",
    )(h, oa, ob, woa, wob, g2, b2, wg, wu, wd, g3, b3)


def _prepare_w_in(w_in):
    d = w_in.shape[0]
    sizes = (N_HEADS_A * HEAD_DIM, N_KV_A * HEAD_DIM, N_KV_A * HEAD_DIM, N_IDX_HEADS * IDX_DIM, IDX_DIM,
             N_IDX_HEADS, N_HEADS_B * HEAD_DIM, N_HEADS_B * HEAD_DIM, N_HEADS_B * HEAD_DIM, N_HEADS_B)
    offs = [0]
    for s in sizes:
        offs.append(offs[-1] + s)
    qa, ka, va, qi, ki, wi, qb, kb, vb, fb = (w_in[:, offs[k]:offs[k + 1]] for k in range(10))
    qa = qa.reshape(d, N_HEADS_A, HEAD_DIM)[:, jnp.array(_QA_HEAD_ORDER)].reshape(d, -1)
    pad = jnp.zeros((d, LANES - N_IDX_HEADS - N_HEADS_B), w_in.dtype)
    w = jnp.concatenate([qa, ka, va, qi, ki, ki, wi, fb, pad, qb, kb, vb], axis=1)
    assert w.shape[1] == _C_END
    return w.astype(MXU_DTYPE)


def _rope_tables(pos):
    half = ROT_DIM // 2
    inv_freq = ROPE_THETA ** (-jnp.arange(half, dtype=jnp.float32) * 2.0 / ROT_DIM)
    ang = pos.astype(jnp.float32)[:, None] * inv_freq[None, :]
    cos, sin = jnp.cos(ang), jnp.sin(ang)
    ones = jnp.ones((pos.shape[0], HEAD_DIM - ROT_DIM), jnp.float32)
    cos64 = jnp.concatenate([cos, cos, ones], axis=1)
    sin64 = jnp.concatenate([-sin, sin, jnp.zeros_like(ones)], axis=1)
    return jnp.tile(cos64, (1, LANES // HEAD_DIM)), jnp.tile(sin64, (1, LANES // HEAD_DIM))


def _ffn_project_tokens(x, ffn, g, b, w_in_p, bf_p, pos, rows_per_seq):
    n = x.shape[0]
    tm = _row_tile(n)
    cos, sin = _rope_tables(pos)
    if tm <= rows_per_seq:
        assert rows_per_seq % tm == 0
        tab_blocks = rows_per_seq // tm
    else:
        assert tm % rows_per_seq == 0
        cos = jnp.tile(cos, (tm // rows_per_seq, 1))
        sin = jnp.tile(sin, (tm // rows_per_seq, 1))
        tab_blocks = 1
    return _ffn_project(x, *ffn, g, b, w_in_p, bf_p, cos, sin, tm, tab_blocks)


def _blocked_cum(logf_keys, kb):
    b, s, h = logf_keys.shape
    cum = _cumsum_keys(jnp.transpose(logf_keys, (0, 2, 1)))
    return jnp.transpose(cum.reshape(b, h, s // kb, kb), (0, 2, 1, 3))


def _pad_keys(x, s_pad):
    return jnp.pad(x, ((0, 0), (0, s_pad - x.shape[1]), (0, 0)))


def kernel(x_prompt, x_sample, cache_k_a, cache_v_a, cache_kidx_a, cache_k_b, cache_v_b, cache_logf_b,
           w_in, b_f, w_out, ln1_g, ln1_b, ffn1_w_gate, ffn1_w_up, ffn1_w_down,
           ln2_g, ln2_b, ln3_g, ln3_b, ffn2_w_gate, ffn2_w_up, ffn2_w_down):
    assert w_in.shape[0] == DEPTH
    bsz, seq, d = x_prompt.shape
    dbs, dseq, _ = x_sample.shape
    past = cache_k_a.shape[2]
    bf16 = MXU_DTYPE

    w_in_p = _prepare_w_in(w_in[0])
    bf_p = jnp.zeros((1, LANES), jnp.float32).at[0, N_IDX_HEADS:N_IDX_HEADS + N_HEADS_B].set(b_f[0])
    order = jnp.array(_QA_HEAD_ORDER)
    w_out_a = w_out[0, :N_HEADS_A * HEAD_DIM].reshape(N_HEADS_A, HEAD_DIM, d)[order].reshape(-1, d).astype(bf16)
    w_out_b = w_out[0, N_HEADS_A * HEAD_DIM:].astype(bf16)
    ffn1 = (ffn1_w_gate[0].astype(bf16), ffn1_w_up[0].astype(bf16), ffn1_w_down[0].astype(bf16))
    ffn2 = (ffn2_w_gate[0].astype(bf16), ffn2_w_up[0].astype(bf16), ffn2_w_down[0].astype(bf16))
    vec = lambda a: a[0].reshape(1, d)

    def layer(x, pos, rows_per_seq, attend):
        n = x.shape[0] * x.shape[1]
        h, proj = _ffn_project_tokens(x.reshape(n, d), ffn1, vec(ln1_g), vec(ln1_b), w_in_p, bf_p, pos, rows_per_seq)
        oa, ob = attend(proj)
        y = _post(h, oa, ob, w_out_a, w_out_b, vec(ln2_g), vec(ln2_b), *ffn2, vec(ln3_g), vec(ln3_b))
        return y.reshape(x.shape), proj

    tq_p = min(Q_TILE, seq)
    kb_p = min(KEY_BLOCK, seq)

    def attend_prompt(proj):
        qa, _, _, qi, _, wi, qb, _, _, logf, kab, vab, kib, kbb, vbb = proj
        ck = _blocked_cum(logf.reshape(bsz, seq, N_HEADS_B), kb_p)
        oa = _dsa(qi, wi, qa, kib, kab, vab, batch=bsz, tq=tq_p, kb=kb_p, q_off=0,
                  k_sel=min(TOPK_MAX, seq // 4))
        ob = _fox(qb, kbb, vbb, ck, batch=bsz, tq=tq_p, kb=kb_p, q_off=0)
        return oa, ob

    y_p, proj_p = layer(x_prompt, jnp.arange(seq, dtype=jnp.int32), seq, attend_prompt)

    total = past + dseq
    kb_s = min(SAMPLE_KEY_BLOCK, past)
    s_pad = -(-total // kb_s) * kb_s

    def attend_sample(proj):
        qa, _, _, qi, _, wi, qb, _, _, logf, kab, vab, kib, kbb, vbb = proj
        new = lambda a: a.reshape(dbs, dseq, -1)
        keys = lambda c, nw: lax.dynamic_update_slice(
            _pad_keys(c.reshape(dbs, past, -1).astype(bf16), s_pad), new(nw), (0, past, 0)
        ).reshape(dbs * s_pad, -1)
        kidx2 = jnp.concatenate([cache_kidx_a[0], cache_kidx_a[0]], axis=-1)
        logf_all = _pad_keys(jnp.concatenate([cache_logf_b[0].astype(jnp.float32), new(logf)], axis=1), s_pad)
        ck = _blocked_cum(logf_all, kb_s)
        oa = _dsa(qi, wi, qa, keys(kidx2, kib), keys(cache_k_a[0], kab), keys(cache_v_a[0], vab),
                  batch=dbs, tq=dseq, kb=kb_s, q_off=past, k_sel=min(TOPK_MAX, total // 4))
        ob = _fox(qb, keys(cache_k_b[0], kbb), keys(cache_v_b[0], vbb), ck,
                  batch=dbs, tq=dseq, kb=kb_s, q_off=past)
        return oa, ob

    y_s, proj_s = layer(x_sample, past + jnp.arange(dseq, dtype=jnp.int32), dseq, attend_sample)

    def rows(proj, b, t):
        _, ka, va, _, kidx, _, _, kbv, vbv, logf = proj[:10]
        return (ka.reshape(1, b, t, N_KV_A, HEAD_DIM), va.reshape(1, b, t, N_KV_A, HEAD_DIM),
                kidx.reshape(1, b, t, IDX_DIM), kbv.reshape(1, b, t, N_HEADS_B, HEAD_DIM),
                vbv.reshape(1, b, t, N_HEADS_B, HEAD_DIM), logf.reshape(1, b, t, N_HEADS_B))

    return (y_p, y_s) + rows(proj_p, bsz, seq) + rows(proj_s, dbs, dseq)
```

```python
import functools
import math

import jax
import jax.numpy as jnp
from jax import lax
from jax.experimental import pallas as pl
from jax.experimental.pallas import tpu as pltpu

CHUNK = 64
_CHUNK_SHIFT = 6
HEAD_DIM = 64
N_HEADS_A = 8
N_KV_A = 2
N_IDX_HEADS = 8
IDX_DIM = 64
TOPK_MAX = 256
N_HEADS_B = 8
ROT_DIM = HEAD_DIM // 4
ROPE_THETA = 500000.0
LN_EPS = 1e-5
DEPTH = 1
ALPHA = (2.0 * DEPTH) ** 0.25

MXU_DTYPE = jnp.bfloat16

LANES = 128
V7X_VMEM_LIMIT_BYTES = 60 * 1024 * 1024

ROW_TILE = 512
FF_CHUNK = 256
Q_TILE = 512
KEY_BLOCK = 512
SAMPLE_KEY_BLOCK = 512
ROW_CHUNK = 64
STACK_ROWS = 512

INT_MIN = -(2 ** 31)
FIELD_BITS = 15
FIELD_GUARD = 1 << FIELD_BITS
FIELD_FLAGS = 0x00010001
NEG_INF = float("-inf")
F32_LOWEST = float(jnp.finfo(jnp.float32).min)
LOG2E = math.log2(math.e)

_C_QA, _C_KA, _C_VA, _C_QI, _C_KI, _C_WF, _C_QB, _C_KB, _C_VB, _C_END = (
    0, 512, 640, 768, 1280, 1408, 1536, 2048, 2560, 3072)
_QA_HEAD_ORDER = (0, 4, 1, 5, 2, 6, 3, 7)


def _nt_dot(a, b):
    return lax.dot_general(a, b, (((1,), (1,)), ((), ())), preferred_element_type=jnp.float32)


def _dot(a, b):
    return jnp.dot(a, b, preferred_element_type=jnp.float32)


def _layer_norm(x, g, b):
    mu = jnp.mean(x, axis=-1, keepdims=True)
    xc = x - mu
    var = jnp.mean(xc * xc, axis=-1, keepdims=True)
    return xc * lax.rsqrt(var + LN_EPS) * g + b


def _macaron_half(x, wg_ref, wu_ref, wd_ref, g, b, act_ref):
    xb = x.astype(MXU_DTYPE)
    d_ff = wg_ref.shape[1]
    for c in range(d_ff // FF_CHUNK):
        cols = slice(c * FF_CHUNK, (c + 1) * FF_CHUNK)
        gate = _dot(xb, wg_ref[:, cols])
        up = _dot(xb, wu_ref[:, cols])
        act_ref[:, cols] = (gate * jax.nn.sigmoid(gate) * up).astype(MXU_DTYPE)
    down = _dot(act_ref[...], wd_ref[...])
    return _layer_norm(ALPHA * x + 0.5 * down, g, b)


def _const_spec(shape):
    return pl.BlockSpec(shape, lambda *_: (0,) * len(shape), pipeline_mode=pl.Buffered(1))


def _row_tile(n):
    tm = ROW_TILE
    while n % tm:
        tm //= 2
    return tm


def _rope(x, cos, sin):
    lane = lax.broadcasted_iota(jnp.int32, x.shape, 1) & (HEAD_DIM - 1)
    partner = jnp.where(lane < ROT_DIM // 2,
                        pltpu.roll(x, LANES - ROT_DIM // 2, 1),
                        pltpu.roll(x, ROT_DIM // 2, 1))
    return x * cos + partner * sin


def _project_kernel(h_ref, w_ref, bf_ref, cos_ref, sin_ref,
                    qa_ref, ka_ref, va_ref, qi_ref, kidx_ref, wi_ref, qb_ref, kb_ref, vb_ref, logf_ref,
                    kab_ref, vab_ref, kib_ref, kbb_ref, vbb_ref):
    hb = h_ref[...].astype(MXU_DTYPE)
    cos = cos_ref[...]
    sin = sin_ref[...]
    q_scale = HEAD_DIM ** -0.5 * LOG2E

    def part(lo, hi):
        return _dot(hb, w_ref[:, lo:hi])

    def rope_groups(z):
        return jnp.concatenate(
            [_rope(z[:, g * LANES:(g + 1) * LANES], cos, sin) for g in range(z.shape[1] // LANES)], axis=1)

    qa_ref[...] = (rope_groups(part(_C_QA, _C_KA)) * q_scale).astype(MXU_DTYPE)
    def heads(z):
        return z.reshape(z.shape[0], z.shape[1] // HEAD_DIM, HEAD_DIM)

    ka = _rope(part(_C_KA, _C_VA), cos, sin)
    ka_ref[...] = heads(ka)
    kab_ref[...] = ka.astype(MXU_DTYPE)
    va = part(_C_VA, _C_QI)
    va_ref[...] = heads(va)
    vab_ref[...] = va.astype(MXU_DTYPE)
    qi_ref[...] = (rope_groups(part(_C_QI, _C_KI)) * (IDX_DIM ** -0.5)).astype(MXU_DTYPE)
    ki = _rope(part(_C_KI, _C_WF), cos, sin)
    kidx_ref[...] = ki[:, :IDX_DIM]
    kib_ref[...] = ki.astype(MXU_DTYPE)
    wf = part(_C_WF, _C_QB)
    wi_ref[...] = wf[:, :N_IDX_HEADS] * (N_IDX_HEADS ** -0.5)
    f = wf + bf_ref[...]
    logf = -(jnp.maximum(-f, 0.0) + jnp.log1p(jnp.exp(-jnp.abs(f))))
    logf_ref[...] = pltpu.roll(logf, LANES - N_IDX_HEADS, 1)[:, :N_HEADS_B]
    qb_ref[...] = (part(_C_QB, _C_KB) * q_scale).astype(MXU_DTYPE)
    kb = part(_C_KB, _C_VB)
    kb_ref[...] = heads(kb)
    kbb_ref[...] = kb.astype(MXU_DTYPE)
    vb = part(_C_VB, _C_END)
    vb_ref[...] = heads(vb)
    vbb_ref[...] = vb.astype(MXU_DTYPE)


def _ffn_project_kernel(x_ref, wg_ref, wu_ref, wd_ref, g_ref, b_ref, w_ref, bf_ref, cos_ref, sin_ref,
                        h_ref, *rest):
    *proj_refs, act_ref = rest
    h_ref[...] = _macaron_half(x_ref[...], wg_ref, wu_ref, wd_ref, g_ref[...], b_ref[...], act_ref)
    _project_kernel(h_ref, w_ref, bf_ref, cos_ref, sin_ref, *proj_refs)


def _ffn_project(x, wg, wu, wd, g, b, w, bf, cos, sin, tm, tab_blocks):
    n, d = x.shape
    d_ff = wg.shape[1]
    f32, bf16 = jnp.float32, MXU_DTYPE
    kv_a, kv_b = (N_KV_A, HEAD_DIM), (N_HEADS_B, HEAD_DIM)
    tails = [((512,), bf16), (kv_a, f32), (kv_a, f32), ((512,), bf16), ((IDX_DIM,), f32), ((N_IDX_HEADS,), f32),
             ((512,), bf16), (kv_b, f32), (kv_b, f32), ((N_HEADS_B,), f32),
             ((128,), bf16), ((128,), bf16), ((128,), bf16), ((512,), bf16), ((512,), bf16)]
    row = lambda *tail: pl.BlockSpec((tm,) + tail, lambda i: (i,) + (0,) * len(tail))
    tab = pl.BlockSpec((tm, LANES), lambda i: (i % tab_blocks, 0))
    tails = [((d,), f32)] + tails
    outs = pl.pallas_call(
        _ffn_project_kernel,
        grid=(n // tm,),
        in_specs=[row(d), _const_spec((d, d_ff)), _const_spec((d, d_ff)), _const_spec((d_ff, d)),
                  _const_spec((1, d)), _const_spec((1, d)),
                  _const_spec(w.shape), _const_spec((1, LANES)), tab, tab],
        out_specs=[row(*tail) for tail, _ in tails],
        out_shape=[jax.ShapeDtypeStruct((n,) + tail, dt) for tail, dt in tails],
        scratch_shapes=[pltpu.VMEM((tm, d_ff), MXU_DTYPE)],
        compiler_params=pltpu.CompilerParams(
            dimension_semantics=("arbitrary",), vmem_limit_bytes=V7X_VMEM_LIMIT_BYTES),
        name="ffn_project",
    )(x, wg, wu, wd, g, b, w, bf, cos, sin)
    return outs[0], outs[1:]


def _cumsum_kernel(x_ref, o_ref, *, seg):
    r = lax.broadcasted_iota(jnp.int32, (seg, seg), 0)
    c = lax.broadcasted_iota(jnp.int32, (seg, seg), 1)
    tri = (r <= c).astype(jnp.float32)
    carry = jnp.zeros((x_ref.shape[0], 1), jnp.float32)
    for s in range(x_ref.shape[1] // seg):
        cols = slice(s * seg, (s + 1) * seg)
        part = jnp.dot(x_ref[:, cols], tri, preferred_element_type=jnp.float32,
                       precision=lax.Precision.HIGHEST) + carry
        o_ref[:, cols] = part * LOG2E
        carry = part[:, seg - 1:seg]


def _cumsum_keys(x):
    b, h, s = x.shape
    seg = 2 * LANES if s % (2 * LANES) == 0 else LANES
    assert s % seg == 0
    rows = b * h
    tr = min(rows, 2 * LANES)
    assert rows % tr == 0
    return pl.pallas_call(
        functools.partial(_cumsum_kernel, seg=seg),
        grid=(rows // tr,),
        in_specs=[pl.BlockSpec((tr, s), lambda i: (i, 0))],
        out_specs=pl.BlockSpec((tr, s), lambda i: (i, 0)),
        out_shape=jax.ShapeDtypeStruct((rows, s), jnp.float32),
        compiler_params=pltpu.CompilerParams(dimension_semantics=("arbitrary",)),
        name="cumsum_keys",
    )(x.reshape(rows, s)).reshape(b, h, s)


def _num_key_blocks(i, tq, kb, q_off):
    return (q_off + (i + 1) * tq + kb - 1) // kb


def _lane_groups(x):
    return [x[:, g * LANES:(g + 1) * LANES] for g in range(x.shape[1] // LANES)]


def _lane_tile(x, width):
    return jnp.concatenate([x] * (width // LANES), axis=1)


def _softmax_step(s, h, m_ref, l_ref, acc_ref, v_ones):
    groups = _lane_groups(s)
    smax = functools.reduce(jnp.maximum, groups)
    m_old = m_ref[h]
    m_new = jnp.maximum(m_old, jnp.max(smax, axis=-1, keepdims=True))
    m_safe = jnp.maximum(m_new, F32_LOWEST)
    corr = jnp.exp2(m_old - m_safe)
    p = jnp.concatenate([jnp.exp2(g - m_safe) for g in groups], axis=1).astype(MXU_DTYPE)
    pv = _dot(p, v_ones)
    acc_ref[h] = corr * acc_ref[h] + pv[:, :LANES]
    l_ref[h] = corr * l_ref[h] + pv[:, LANES:]
    m_ref[h] = m_new


def _store_values_and_ones(v_ref, lanes, vo_ref):
    vo_ref[:, :LANES] = v_ref[:, lanes]
    vo_ref[:, LANES:] = jnp.ones((vo_ref.shape[0], LANES), vo_ref.dtype)


def _store_half_masks(q_ref, qm_ref):
    for p in range(q_ref.shape[1] // LANES):
        q2 = q_ref[:, p * LANES:(p + 1) * LANES]
        lane = lax.broadcasted_iota(jnp.int32, q2.shape, 1)
        zero = jnp.zeros_like(q2)
        qm_ref[2 * p] = jnp.where(lane < HEAD_DIM, q2, zero)
        qm_ref[2 * p + 1] = jnp.where(lane >= HEAD_DIM, q2, zero)


def _init_state(m_ref, l_ref, acc_ref):
    m_ref[...] = jnp.full(m_ref.shape, NEG_INF, jnp.float32)
    l_ref[...] = jnp.zeros(l_ref.shape, jnp.float32)
    acc_ref[...] = jnp.zeros(acc_ref.shape, jnp.float32)


def _write_pairs(o_ref, l_ref, acc_ref, stack):
    tq = o_ref.shape[0]

    def slot(h):
        rows = slice((h % stack) * tq, (h % stack + 1) * tq)
        return acc_ref[h // stack, rows, :] / l_ref[h // stack, rows, :]

    for p in range(o_ref.shape[1] // LANES):
        lo, hi = slot(2 * p), slot(2 * p + 1)
        lane = lax.broadcasted_iota(jnp.int32, lo.shape, 1)
        o_ref[:, p * LANES:(p + 1) * LANES] = jnp.where(lane < HEAD_DIM, lo, hi).astype(o_ref.dtype)


def _stacked_queries(qm_ref, g, stack):
    q = qm_ref[g * stack:(g + 1) * stack]
    return q.reshape(q.shape[0] * q.shape[1], LANES)


def _head_stack(tq, n_sharing):
    return max(1, min(n_sharing, STACK_ROWS // tq))


def _attention_scratch(tq, n_heads, stack):
    state = (n_heads // stack, stack * tq, LANES)
    return [pltpu.VMEM((n_heads, tq, LANES), MXU_DTYPE),
            pltpu.VMEM(state, jnp.float32),
            pltpu.VMEM(state, jnp.float32),
            pltpu.VMEM(state, jnp.float32)]


def _fox_kernel(q_ref, k_ref, v_ref, ck_ref, o_ref, vo_ref, qm_ref, m_ref, l_ref, acc_ref,
                *, tq, kb, q_off, stack):
    i = pl.program_id(1)
    nb = _num_key_blocks(i, tq, kb, q_off)
    assert kb % tq == 0 and q_off % tq == 0

    @pl.when(i == 0)
    def _():
        for p in range(N_HEADS_B // 2):
            _store_values_and_ones(v_ref, slice(p * LANES, (p + 1) * LANES), vo_ref.at[p])

    _store_half_masks(q_ref, qm_ref)
    _init_state(m_ref, l_ref, acc_ref)

    def block(j, masked):
        start = pl.multiple_of(j * kb, kb)
        ck = ck_ref[0, j]
        if masked:
            row = q_off + i * tq + lax.broadcasted_iota(jnp.int32, (tq, kb), 0)
            col = j * kb + lax.broadcasted_iota(jnp.int32, (tq, kb), 1)
            causal = jnp.concatenate([col <= row] * stack, axis=0)
        for g in range(N_HEADS_B // stack):
            pair = g * stack // 2
            lanes = slice(pair * LANES, (pair + 1) * LANES)
            if stack == 1:
                forget = ck[g:g + 1, :]
            else:
                forget = jnp.concatenate([jnp.broadcast_to(ck[h:h + 1, :], (tq, kb))
                                          for h in range(g * stack, (g + 1) * stack)], axis=0)
            s = _nt_dot(_stacked_queries(qm_ref, g, stack), k_ref[pl.ds(start, kb), lanes]) - forget
            if masked:
                s = jnp.where(causal, s, NEG_INF)
            _softmax_step(s, g, m_ref, l_ref, acc_ref, vo_ref[pair, pl.ds(start, kb), :])

    def full_block(j, _):
        block(j, masked=False)
        return 0

    lax.fori_loop(0, nb - 1, full_block, 0)
    block(nb - 1, masked=True)
    _write_pairs(o_ref, l_ref, acc_ref, stack)


def _fox(q, k, v, ck, *, batch, tq, kb, q_off):
    n, width = q.shape
    t_q = n // batch
    s_len = k.shape[0] // batch
    nq = t_q // tq
    stack = _head_stack(tq, 2)
    return pl.pallas_call(
        functools.partial(_fox_kernel, tq=tq, kb=kb, q_off=q_off, stack=stack),
        grid=(batch, nq),
        in_specs=[
            pl.BlockSpec((tq, width), lambda b, i: (b * nq + i, 0)),
            pl.BlockSpec((s_len, width), lambda b, i: (b, 0)),
            pl.BlockSpec((s_len, width), lambda b, i: (b, 0)),
            pl.BlockSpec((1,) + ck.shape[1:], lambda b, i: (b, 0, 0, 0)),
        ],
        out_specs=pl.BlockSpec((tq, width), lambda b, i: (b * nq + i, 0)),
        out_shape=jax.ShapeDtypeStruct((n, width), MXU_DTYPE),
        scratch_shapes=[pltpu.VMEM((N_HEADS_B // 2, s_len, 2 * LANES), MXU_DTYPE)]
        + _attention_scratch(tq, N_HEADS_B, stack),
        compiler_params=pltpu.CompilerParams(
            dimension_semantics=("arbitrary", "arbitrary"), vmem_limit_bytes=V7X_VMEM_LIMIT_BYTES),
        name="fox_attention",
    )(q, k, v, ck)


def _dsa_kernel(qi_ref, wi_ref, qa_ref, ki_ref, ka_ref, va_ref, o_ref,
                key_ref, word_ref, bias_ref, thr_ref, cand_ref, lim_ref, cnt_ref, vo_ref,
                qm_ref, m_ref, l_ref, acc_ref,
                *, tq, kb, q_off, k_sel, stack):
    i = pl.program_id(1)
    nb = _num_key_blocks(i, tq, kb, q_off)
    k_f = jnp.float32(k_sel)
    rc = min(ROW_CHUNK, tq)
    chunks = [slice(r * rc, (r + 1) * rc) for r in range(tq // rc)]
    groups = [slice(g * LANES, (g + 1) * LANES) for g in range(kb // LANES)]

    def replicate(x):
        return jnp.broadcast_to(x, (tq, LANES))

    def pack_fields(f):
        return (f[:, :kb // 2] << 16) | f[:, kb // 2:]

    assert kb % (2 * LANES) == 0
    _store_half_masks(qi_ref, qm_ref)
    wi = wi_ref[...]

    def score_block(j, _):
        start = pl.multiple_of(j * kb, kb)
        kblk = ki_ref[pl.ds(start, kb), :]
        score = jnp.zeros((tq, kb), jnp.float32)
        if stack == N_IDX_HEADS:
            weights = jnp.concatenate([wi[:, h:h + 1] for h in range(N_IDX_HEADS)], axis=0)
            rel = weights * jnp.maximum(_nt_dot(_stacked_queries(qm_ref, 0, stack), kblk), 0.0)
            for h in range(N_IDX_HEADS):
                score = score + rel[h * tq:(h + 1) * tq, :]
        else:
            for h in range(N_IDX_HEADS):
                score = score + wi[:, h:h + 1] * jnp.maximum(_nt_dot(qm_ref[h], kblk), 0.0)
        bits = lax.bitcast_convert_type(score, jnp.int32)
        key = jnp.where(bits < 0, bits ^ jnp.int32(0x7FFFFFFF), bits)
        row = q_off + i * tq + lax.broadcasted_iota(jnp.int32, (tq, kb), 0)
        col = j * kb + lax.broadcasted_iota(jnp.int32, (tq, kb), 1)
        admissible = (col >> _CHUNK_SHIFT) <= (row >> _CHUNK_SHIFT)
        key = jnp.where(admissible, key, jnp.int32(INT_MIN))
        key_ref[j] = key
        word_ref[j] = jnp.transpose(pack_fields((key >> (32 - FIELD_BITS)) + (FIELD_GUARD + FIELD_GUARD // 2)))
        return 0

    lax.fori_loop(0, nb, score_block, 0)

    def count(pred):
        def body(j, acc):
            parts = []
            for rows in chunks:
                hits = [jnp.where(pred(key_ref[j, rows, g], rows, j * kb + g.start), 1.0, 0.0) for g in groups]
                parts.append(functools.reduce(jnp.add, hits))
            return acc + jnp.concatenate(parts, axis=0)
        acc = lax.fori_loop(0, nb, body, jnp.zeros((tq, LANES), jnp.float32))
        return replicate(jnp.sum(acc, axis=-1, keepdims=True))

    def count_fields_at_least(cand):
        both = (cand << 16) | cand

        def body(j, acc):
            for r0 in range(0, kb // 2, ROW_CHUNK):
                words = word_ref[j, r0:r0 + ROW_CHUNK, :]
                flags = lax.shift_right_logical(words - both, FIELD_BITS) & FIELD_FLAGS
                acc = acc + jnp.sum(flags.reshape(ROW_CHUNK // 8, 8, tq), axis=0)
            return acc
        acc = lax.fori_loop(0, nb, body, jnp.zeros((8, tq), jnp.int32))
        return jnp.sum((acc & 0xFFFF) + lax.shift_right_logical(acc, 16), axis=0, keepdims=True)

    def bisect_fields(cnt):
        def step(b, carry):
            thr, cnt = carry
            cand = thr | (jnp.int32(1) << (jnp.int32(FIELD_BITS - 1) - b))
            c = count_fields_at_least(cand)
            take = c >= k_sel
            return jnp.where(take, cand, thr), jnp.where(take, c, cnt)

        return lax.fori_loop(0, FIELD_BITS, step, (jnp.zeros((1, tq), jnp.int32), cnt))

    def per_row(row):
        return jnp.transpose(jnp.broadcast_to(row, (LANES, tq)))

    top, cnt = bisect_fields(jnp.full((1, tq), k_sel, jnp.int32))
    lim_ref[...] = per_row(top << FIELD_BITS)

    def middle_fields(j, _):
        words = []
        for rows in chunks:
            base = lim_ref[rows, :] - (FIELD_GUARD + (1 << 29))
            mid = [jnp.clip((key_ref[j, rows, g] >> 2) - base, FIELD_GUARD, 2 * FIELD_GUARD - 1) for g in groups]
            words.append(pack_fields(jnp.concatenate(mid, axis=1)))
        word_ref[j] = jnp.transpose(jnp.concatenate(words, axis=0))
        return 0

    lax.fori_loop(0, nb, middle_fields, 0)
    middle, cnt = bisect_fields(cnt)
    prefix = (top << FIELD_BITS) | middle
    thr_ref[...] = per_row((prefix << 2) ^ jnp.int32(INT_MIN))
    cnt_ref[...] = per_row(cnt).astype(jnp.float32)

    def low_bit(b, _):
        cand = thr_ref[...] | (jnp.int32(2) >> b)
        cand_ref[...] = cand
        c = count(lambda key, rows, first: key >= cand_ref[rows, :])
        take = c >= k_f
        thr_ref[...] = jnp.where(take, cand, thr_ref[...])
        cnt_ref[...] = jnp.where(take, c, cnt_ref[...])
        return 0

    lax.fori_loop(0, 2, low_bit, 0)

    thr = thr_ref[...]
    has_thr = thr > jnp.int32(INT_MIN)
    surplus = jnp.logical_and(has_thr, cnt_ref[...] > k_f)

    def bias_without_surplus():
        cand_ref[...] = jnp.maximum(thr_ref[...], jnp.int32(INT_MIN + 1))

        def block(j, _):
            for rows in chunks:
                for g in groups:
                    bias_ref[j, rows, g] = jnp.where(key_ref[j, rows, g] >= cand_ref[rows, :], 0.0, NEG_INF)
            return 0

        lax.fori_loop(0, nb, block, 0)

    def bias_with_surplus():
        above = count(lambda key, rows, first: key > thr_ref[rows, :])
        cnt_ref[...] = jnp.where(has_thr, k_f - above, 0.0)
        r = lax.broadcasted_iota(jnp.int32, (kb, kb), 0)
        c = lax.broadcasted_iota(jnp.int32, (kb, kb), 1)
        tri = (r <= c).astype(MXU_DTYPE)

        def block(j, before):
            key = key_ref[j]
            tied = key == _lane_tile(thr_ref[...], kb)
            rank = _dot(jnp.where(tied, 1.0, 0.0).astype(MXU_DTYPE), tri)
            take = jnp.logical_and(tied, rank + _lane_tile(before, kb) <= _lane_tile(cnt_ref[...], kb))
            sel = jnp.logical_or(key > _lane_tile(thr_ref[...], kb), take)
            bias_ref[j] = jnp.where(sel, 0.0, NEG_INF)
            return before + replicate(jnp.max(rank, axis=-1, keepdims=True))

        lax.fori_loop(0, nb, block, jnp.zeros((tq, LANES), jnp.float32))

    any_surplus = jnp.max(jnp.where(surplus, 1.0, 0.0)) > 0.0
    lax.cond(any_surplus, bias_with_surplus, bias_without_surplus)

    @pl.when(i == 0)
    def _():
        _store_values_and_ones(va_ref, slice(0, LANES), vo_ref)

    _store_half_masks(qa_ref, qm_ref)
    _init_state(m_ref, l_ref, acc_ref)

    def attend(j, _):
        start = pl.multiple_of(j * kb, kb)
        for g in range(N_HEADS_A // stack):
            bias = bias_ref[j] if stack == 1 else jnp.concatenate([bias_ref[j]] * stack, axis=0)
            s = _nt_dot(_stacked_queries(qm_ref, g, stack), ka_ref[pl.ds(start, kb), :]) + bias
            _softmax_step(s, g, m_ref, l_ref, acc_ref, vo_ref[pl.ds(start, kb), :])
        return 0

    lax.fori_loop(0, nb, attend, 0)
    _write_pairs(o_ref, l_ref, acc_ref, stack)


def _dsa(qi, wi, qa, ki, ka, va, *, batch, tq, kb, q_off, k_sel):
    n, width = qa.shape
    t_q = n // batch
    s_len = ka.shape[0] // batch
    nq = t_q // tq
    nblk = s_len // kb
    qspec = lambda wdt: pl.BlockSpec((tq, wdt), lambda b, i: (b * nq + i, 0))
    kspec = pl.BlockSpec((s_len, LANES), lambda b, i: (b, 0))
    assert N_IDX_HEADS == N_HEADS_A
    stack = _head_stack(tq, N_HEADS_A)
    return pl.pallas_call(
        functools.partial(_dsa_kernel, tq=tq, kb=kb, q_off=q_off, k_sel=k_sel, stack=stack),
        grid=(batch, nq),
        in_specs=[qspec(width), qspec(N_IDX_HEADS), qspec(width), kspec, kspec, kspec],
        out_specs=qspec(width),
        out_shape=jax.ShapeDtypeStruct((n, width), MXU_DTYPE),
        scratch_shapes=[pltpu.VMEM((nblk, tq, kb), jnp.int32),
                        pltpu.VMEM((nblk, kb // 2, tq), jnp.int32),
                        pltpu.VMEM((nblk, tq, kb), jnp.float32),
                        pltpu.VMEM((tq, LANES), jnp.int32),
                        pltpu.VMEM((tq, LANES), jnp.int32),
                        pltpu.VMEM((tq, LANES), jnp.int32),
                        pltpu.VMEM((tq, LANES), jnp.float32),
                        pltpu.VMEM((s_len, 2 * LANES), MXU_DTYPE),
                        ] + _attention_scratch(tq, N_HEADS_A, stack),
        compiler_params=pltpu.CompilerParams(
            dimension_semantics=("arbitrary", "arbitrary"), vmem_limit_bytes=V7X_VMEM_LIMIT_BYTES),
        name="dsa_attention",
    )(qi, wi, qa, ki, ka, va)


def _post_kernel(h_ref, oa_ref, ob_ref, woa_ref, wob_ref, g2_ref, b2_ref,
                 wg_ref, wu_ref, wd_ref, g3_ref, b3_ref, o_ref, act_ref):
    mix = _dot(oa_ref[...], woa_ref[...]) + _dot(ob_ref[...], wob_ref[...])
    h2 = _layer_norm(ALPHA * h_ref[...] + mix, g2_ref[...], b2_ref[...])
    o_ref[...] = _macaron_half(h2, wg_ref, wu_ref, wd_ref, g3_ref[...], b3_ref[...], act_ref)


def _post(h, oa, ob, woa, wob, g2, b2, wg, wu, wd, g3, b3):
    n, d = h.shape
    d_ff = wg.shape[1]
    tm = _row_tile(n)
    row = lambda wdt: pl.BlockSpec((tm, wdt), lambda i: (i, 0))
    return pl.pallas_call(
        _post_kernel,
        grid=(n // tm,),
        in_specs=[row(d), row(oa.shape[1]), row(ob.shape[1]),
                  _const_spec(woa.shape), _const_spec(wob.shape), _const_spec((1, d)), _const_spec((1, d)),
                  _const_spec((d, d_ff)), _const_spec((d, d_ff)), _const_spec((d_ff, d)),
                  _const_spec((1, d)), _const_spec((1, d))],
        out_specs=row(d),
        out_shape=jax.ShapeDtypeStruct((n, d), jnp.float32),
        scratch_shapes=[pltpu.VMEM((tm, d_ff), MXU_DTYPE)],
        compiler_params=pltpu.CompilerParams(
            dimension_semantics=("arbitrary",), vmem_limit_bytes=V7X_VMEM_LIMIT_BYTES),
        name="post",
    )(h, oa, ob, woa, wob, g2, b2, wg, wu, wd, g3, b3)


def _prepare_w_in(w_in):
    d = w_in.shape[0]
    sizes = (N_HEADS_A * HEAD_DIM, N_KV_A * HEAD_DIM, N_KV_A * HEAD_DIM, N_IDX_HEADS * IDX_DIM, IDX_DIM,
             N_IDX_HEADS, N_HEADS_B * HEAD_DIM, N_HEADS_B * HEAD_DIM, N_HEADS_B * HEAD_DIM, N_HEADS_B)
    offs = [0]
    for s in sizes:
        offs.append(offs[-1] + s)
    qa, ka, va, qi, ki, wi, qb, kb, vb, fb = (w_in[:, offs[k]:offs[k + 1]] for k in range(10))
    qa = qa.reshape(d, N_HEADS_A, HEAD_DIM)[:, jnp.array(_QA_HEAD_ORDER)].reshape(d, -1)
    pad = jnp.zeros((d, LANES - N_IDX_HEADS - N_HEADS_B), w_in.dtype)
    w = jnp.concatenate([qa, ka, va, qi, ki, ki, wi, fb, pad, qb, kb, vb], axis=1)
    assert w.shape[1] == _C_END
    return w.astype(MXU_DTYPE)


def _rope_tables(pos):
    half = ROT_DIM // 2
    inv_freq = ROPE_THETA ** (-jnp.arange(half, dtype=jnp.float32) * 2.0 / ROT_DIM)
    ang = pos.astype(jnp.float32)[:, None] * inv_freq[None, :]
    cos, sin = jnp.cos(ang), jnp.sin(ang)
    ones = jnp.ones((pos.shape[0], HEAD_DIM - ROT_DIM), jnp.float32)
    cos64 = jnp.concatenate([cos, cos, ones], axis=1)
    sin64 = jnp.concatenate([-sin, sin, jnp.zeros_like(ones)], axis=1)
    return jnp.tile(cos64, (1, LANES // HEAD_DIM)), jnp.tile(sin64, (1, LANES // HEAD_DIM))


def _ffn_project_tokens(x, ffn, g, b, w_in_p, bf_p, pos, rows_per_seq):
    n = x.shape[0]
    tm = _row_tile(n)
    cos, sin = _rope_tables(pos)
    if tm <= rows_per_seq:
        assert rows_per_seq % tm == 0
        tab_blocks = rows_per_seq // tm
    else:
        assert tm % rows_per_seq == 0
        cos = jnp.tile(cos, (tm // rows_per_seq, 1))
        sin = jnp.tile(sin, (tm // rows_per_seq, 1))
        tab_blocks = 1
    return _ffn_project(x, *ffn, g, b, w_in_p, bf_p, cos, sin, tm, tab_blocks)


def _blocked_cum(logf_keys, kb):
    b, s, h = logf_keys.shape
    cum = _cumsum_keys(jnp.transpose(logf_keys, (0, 2, 1)))
    return jnp.transpose(cum.reshape(b, h, s // kb, kb), (0, 2, 1, 3))


def _pad_keys(x, s_pad):
    return jnp.pad(x, ((0, 0), (0, s_pad - x.shape[1]), (0, 0)))


def kernel(x_prompt, x_sample, cache_k_a, cache_v_a, cache_kidx_a, cache_k_b, cache_v_b, cache_logf_b,
           w_in, b_f, w_out, ln1_g, ln1_b, ffn1_w_gate, ffn1_w_up, ffn1_w_down,
           ln2_g, ln2_b, ln3_g, ln3_b, ffn2_w_gate, ffn2_w_up, ffn2_w_down):
    assert w_in.shape[0] == DEPTH
    bsz, seq, d = x_prompt.shape
    dbs, dseq, _ = x_sample.shape
    past = cache_k_a.shape[2]
    bf16 = MXU_DTYPE

    w_in_p = _prepare_w_in(w_in[0])
    bf_p = jnp.zeros((1, LANES), jnp.float32).at[0, N_IDX_HEADS:N_IDX_HEADS + N_HEADS_B].set(b_f[0])
    order = jnp.array(_QA_HEAD_ORDER)
    w_out_a = w_out[0, :N_HEADS_A * HEAD_DIM].reshape(N_HEADS_A, HEAD_DIM, d)[order].reshape(-1, d).astype(bf16)
    w_out_b = w_out[0, N_HEADS_A * HEAD_DIM:].astype(bf16)
    ffn1 = (ffn1_w_gate[0].astype(bf16), ffn1_w_up[0].astype(bf16), ffn1_w_down[0].astype(bf16))
    ffn2 = (ffn2_w_gate[0].astype(bf16), ffn2_w_up[0].astype(bf16), ffn2_w_down[0].astype(bf16))
    vec = lambda a: a[0].reshape(1, d)

    def layer(x, pos, rows_per_seq, attend):
        n = x.shape[0] * x.shape[1]
        h, proj = _ffn_project_tokens(x.reshape(n, d), ffn1, vec(ln1_g), vec(ln1_b), w_in_p, bf_p, pos, rows_per_seq)
        oa, ob = attend(proj)
        y = _post(h, oa, ob, w_out_a, w_out_b, vec(ln2_g), vec(ln2_b), *ffn2, vec(ln3_g), vec(ln3_b))
        return y.reshape(x.shape), proj

    tq_p = min(Q_TILE, seq)
    kb_p = min(KEY_BLOCK, seq)

    def attend_prompt(proj):
        qa, _, _, qi, _, wi, qb, _, _, logf, kab, vab, kib, kbb, vbb = proj
        ck = _blocked_cum(logf.reshape(bsz, seq, N_HEADS_B), kb_p)
        oa = _dsa(qi, wi, qa, kib, kab, vab, batch=bsz, tq=tq_p, kb=kb_p, q_off=0,
                  k_sel=min(TOPK_MAX, seq // 4))
        ob = _fox(qb, kbb, vbb, ck, batch=bsz, tq=tq_p, kb=kb_p, q_off=0)
        return oa, ob

    y_p, proj_p = layer(x_prompt, jnp.arange(seq, dtype=jnp.int32), seq, attend_prompt)

    total = past + dseq
    kb_s = min(SAMPLE_KEY_BLOCK, past)
    s_pad = -(-total // kb_s) * kb_s

    def attend_sample(proj):
        qa, _, _, qi, _, wi, qb, _, _, logf, kab, vab, kib, kbb, vbb = proj
        new = lambda a: a.reshape(dbs, dseq, -1)
        keys = lambda c, nw: lax.dynamic_update_slice(
            _pad_keys(c.reshape(dbs, past, -1).astype(bf16), s_pad), new(nw), (0, past, 0)
        ).reshape(dbs * s_pad, -1)
        kidx2 = jnp.concatenate([cache_kidx_a[0], cache_kidx_a[0]], axis=-1)
        logf_all = _pad_keys(jnp.concatenate([cache_logf_b[0].astype(jnp.float32), new(logf)], axis=1), s_pad)
        ck = _blocked_cum(logf_all, kb_s)
        oa = _dsa(qi, wi, qa, keys(kidx2, kib), keys(cache_k_a[0], kab), keys(cache_v_a[0], vab),
                  batch=dbs, tq=dseq, kb=kb_s, q_off=past, k_sel=min(TOPK_MAX, total // 4))
        ob = _fox(qb, keys(cache_k_b[0], kbb), keys(cache_v_b[0], vbb), ck,
                  batch=dbs, tq=dseq, kb=kb_s, q_off=past)
        return oa, ob

    y_s, proj_s = layer(x_sample, past + jnp.arange(dseq, dtype=jnp.int32), dseq, attend_sample)

    def rows(proj, b, t):
        _, ka, va, _, kidx, _, _, kbv, vbv, logf = proj[:10]
        return (ka.reshape(1, b, t, N_KV_A, HEAD_DIM), va.reshape(1, b, t, N_KV_A, HEAD_DIM),
                kidx.reshape(1, b, t, IDX_DIM), kbv.reshape(1, b, t, N_HEADS_B, HEAD_DIM),
                vbv.reshape(1, b, t, N_HEADS_B, HEAD_DIM), logf.reshape(1, b, t, N_HEADS_B))

    return (y_p, y_s) + rows(proj_p, bsz, seq) + rows(proj_s, dbs, dseq)
```
